```python
import math
import jax, jax.numpy as jnp
from jax import lax
import numpy as np

D_MODEL = 1024
BATCH = 8
SEQ = 4096
DEPTH = 2

N_MIXERS = 2
N_A = (DEPTH + 1) // 2
N_B = DEPTH // 2
MLA_HEADS = 16
MLA_Q_RANK = 512
MLA_KV_RANK = 256
MLA_NOPE = 64
MLA_ROPE = 32
MLA_V = 64
MLA_IN = MLA_Q_RANK + MLA_KV_RANK + MLA_ROPE
DSA_HEADS = 16
DSA_KV_HEADS = 4
DSA_GROUP = DSA_HEADS // DSA_KV_HEADS
DSA_HEAD_DIM = 64
IDX_HEADS = 8
IDX_DIM = 64
TOPK_MAX = 256
DSA_IN = (DSA_HEADS * DSA_HEAD_DIM + 2 * DSA_KV_HEADS * DSA_HEAD_DIM
          + IDX_HEADS * IDX_DIM + IDX_DIM + IDX_HEADS)
MEM_LEN = 256
XA_HEADS = 4
XA_HEAD_DIM = D_MODEL // XA_HEADS
FFN_HIDDEN = -(-8 * D_MODEL // (3 * 256)) * 256
REL_BUCKETS = 32
REL_MAX_DIST = 128
Q_BLOCK = 128
ROPE_BASE = 10000.0
LN_EPS = 1e-5
RMS_EPS = 1e-6
ALPHA = (2 * DEPTH) ** 0.25
BETA = (8 * DEPTH) ** -0.25

kernel_name = "hybrid_mla_dsa_deepnorm_block"


def layer_norm(x, g, b):
    x32 = x.astype(jnp.float32)
    mu = jnp.mean(x32, axis=-1, keepdims=True)
    var = jnp.mean(jnp.square(x32 - mu), axis=-1, keepdims=True)
    y = (x32 - mu) * lax.rsqrt(var + LN_EPS) * g.astype(jnp.float32) + b.astype(jnp.float32)
    return y.astype(x.dtype)


def rms_norm(x, g):
    x32 = x.astype(jnp.float32)
    y = x32 * lax.rsqrt(jnp.mean(jnp.square(x32), axis=-1, keepdims=True) + RMS_EPS)
    return (y * g.astype(jnp.float32)).astype(x.dtype)


def apply_rope(x, pos):
    r = x.shape[-1]
    inv_freq = ROPE_BASE ** (-jnp.arange(0, r, 2, dtype=jnp.float32) / r)
    ang = pos.astype(jnp.float32)[..., None] * inv_freq
    cos = jnp.cos(ang)[:, :, None, :]
    sin = jnp.sin(ang)[:, :, None, :]
    x32 = x.astype(jnp.float32)
    x1, x2 = x32[..., : r // 2], x32[..., r // 2:]
    out = jnp.concatenate([x1 * cos - x2 * sin, x1 * sin + x2 * cos], axis=-1)
    return out.astype(x.dtype)


def t5_bucket(dist):
    max_exact = REL_BUCKETS // 2
    n = jnp.maximum(dist, 0)
    nf = jnp.maximum(n, 1).astype(jnp.float32)
    large = max_exact + (jnp.log(nf / max_exact) / math.log(REL_MAX_DIST / max_exact)
                         * (REL_BUCKETS - max_exact)).astype(jnp.int32)
    large = jnp.minimum(large, REL_BUCKETS - 1)
    return jnp.where(n < max_exact, n, large)


def to_blocks(a, nb):
    return jnp.moveaxis(a.reshape(a.shape[0], nb, Q_BLOCK, *a.shape[2:]), 1, 0)


def from_blocks(a):
    a = jnp.moveaxis(a, 0, 1)
    return a.reshape(a.shape[0], a.shape[1] * a.shape[2], *a.shape[3:])


gather_rows = jax.vmap(lambda table, idx: table[idx])


def dense_causal_attention(q, k, v, scale):
    s_len = q.shape[1]
    nb = s_len // Q_BLOCK
    key_idx = jnp.arange(s_len)

    def one_block(args):
        q_blk, start = args
        logits = jnp.einsum('bqhd,bkhd->bhqk', q_blk, k).astype(jnp.float32) * scale
        q_idx = start + jnp.arange(Q_BLOCK)
        causal = key_idx[None, :] <= q_idx[:, None]
        logits = jnp.where(causal[None, None], logits, -jnp.inf)
        p = jax.nn.softmax(logits, axis=-1).astype(v.dtype)
        return jnp.einsum('bhqk,bkhd->bqhd', p, v)

    out = lax.map(one_block, (to_blocks(q, nb), jnp.arange(nb) * Q_BLOCK))
    return from_blocks(out)


def mla_mixer(x, pos, w_in, q_norm, w_uq, kv_norm, w_ukv, w_o):
    b, s, _ = x.shape
    h = x @ w_in
    c_q = h[..., :MLA_Q_RANK]
    c_kv = h[..., MLA_Q_RANK:MLA_Q_RANK + MLA_KV_RANK]
    k_rope = h[..., MLA_Q_RANK + MLA_KV_RANK:]
    q = (rms_norm(c_q, q_norm) @ w_uq).reshape(b, s, MLA_HEADS, MLA_NOPE + MLA_ROPE)
    q = jnp.concatenate([q[..., :MLA_NOPE], apply_rope(q[..., MLA_NOPE:], pos)], axis=-1)
    kv = (rms_norm(c_kv, kv_norm) @ w_ukv).reshape(b, s, MLA_HEADS, MLA_NOPE + MLA_V)
    k_nope, v = kv[..., :MLA_NOPE], kv[..., MLA_NOPE:]
    k_rope = apply_rope(k_rope[:, :, None, :], pos)
    k = jnp.concatenate([k_nope, jnp.broadcast_to(k_rope, (b, s, MLA_HEADS, MLA_ROPE))], axis=-1)
    out = dense_causal_attention(q, k, v, (MLA_NOPE + MLA_ROPE) ** -0.5)
    return out.reshape(b, s, MLA_HEADS * MLA_V) @ w_o


def dsa_mixer(x, pos, rel_bias, w_in, w_o):
    b, s, _ = x.shape
    nb = s // Q_BLOCK
    topk = min(TOPK_MAX, s // 4)
    h = x @ w_in
    o1 = DSA_HEADS * DSA_HEAD_DIM
    o2 = o1 + DSA_KV_HEADS * DSA_HEAD_DIM
    o3 = o2 + DSA_KV_HEADS * DSA_HEAD_DIM
    o4 = o3 + IDX_HEADS * IDX_DIM
    o5 = o4 + IDX_DIM
    q = h[..., :o1].reshape(b, s, DSA_HEADS, DSA_HEAD_DIM)
    k = h[..., o1:o2].reshape(b, s, DSA_KV_HEADS, DSA_HEAD_DIM)
    v = h[..., o2:o3].reshape(b, s, DSA_KV_HEADS, DSA_HEAD_DIM)
    q_idx = h[..., o3:o4].reshape(b, s, IDX_HEADS, IDX_DIM)
    k_idx = h[..., o4:o5]
    w_idx = h[..., o5:] * (IDX_HEADS ** -0.5)
    key_idx = jnp.arange(s)

    def one_block(args):
        q_blk, qi_blk, wi_blk, pos_blk, start = args
        t_idx = start + jnp.arange(Q_BLOCK)
        causal = key_idx[None, :] <= t_idx[:, None]
        dots = jnp.einsum('bqhd,bkd->bqhk', qi_blk, k_idx).astype(jnp.float32) * (IDX_DIM ** -0.5)
        score = jnp.einsum('bqh,bqhk->bqk', wi_blk.astype(jnp.float32), jax.nn.relu(dots))
        score = jnp.where(causal[None], score, -jnp.inf)
        _, sel = lax.top_k(score, topk)
        sel_ok = sel <= t_idx[None, :, None]
        k_sel = gather_rows(k, sel)
        v_sel = gather_rows(v, sel)
        pos_sel = gather_rows(pos, sel)
        qg = q_blk.reshape(b, Q_BLOCK, DSA_KV_HEADS, DSA_GROUP, DSA_HEAD_DIM)
        logits = jnp.einsum('bqngd,bqknd->bqngk', qg, k_sel).astype(jnp.float32) * (DSA_HEAD_DIM ** -0.5)
        bucket = t5_bucket(pos_blk[:, :, None] - pos_sel)
        bias = rel_bias[bucket].astype(jnp.float32)
        bias = bias.reshape(b, Q_BLOCK, topk, DSA_KV_HEADS, DSA_GROUP).transpose(0, 1, 3, 4, 2)
        logits = jnp.where(sel_ok[:, :, None, None, :], logits + bias, -jnp.inf)
        p = jax.nn.softmax(logits, axis=-1).astype(v.dtype)
        o = jnp.einsum('bqngk,bqknd->bqngd', p, v_sel)
        return o.reshape(b, Q_BLOCK, DSA_HEADS * DSA_HEAD_DIM)

    out = lax.map(one_block, (to_blocks(q, nb), to_blocks(q_idx, nb), to_blocks(w_idx, nb),
                              to_blocks(pos, nb), jnp.arange(nb) * Q_BLOCK))
    return from_blocks(out) @ w_o


def memory_cross_attention(x, mem, w_q, w_kv, w_o):
    b, s, _ = x.shape
    m = mem.shape[1]
    q = (x @ w_q).reshape(b, s, XA_HEADS, XA_HEAD_DIM)
    kv = (mem @ w_kv).reshape(b, m, 2, XA_HEADS, XA_HEAD_DIM)
    k, v = kv[:, :, 0], kv[:, :, 1]
    logits = jnp.einsum('bshd,bmhd->bhsm', q, k).astype(jnp.float32) * (XA_HEAD_DIM ** -0.5)
    p = jax.nn.softmax(logits, axis=-1).astype(v.dtype)
    o = jnp.einsum('bhsm,bmhd->bshd', p, v)
    return o.reshape(b, s, D_MODEL) @ w_o


def swiglu_ffn(x, w_in, w_down):
    h = x @ w_in
    g, u = h[..., :FFN_HIDDEN], h[..., FFN_HIDDEN:]
    return (jax.nn.silu(g) * u) @ w_down


def setup_inputs(seed: int = 0) -> dict:
    key = jax.random.key(seed)
    ks = jax.random.split(key, 24)
    f32 = jnp.float32

    def w(k, shape, fan_in, scale=1.0):
        return jax.random.normal(k, shape, f32) * (fan_in ** -0.5) * scale

    x = jax.random.normal(ks[0], (BATCH, SEQ, D_MODEL), f32)
    mem = jax.random.normal(ks[1], (BATCH, MEM_LEN, D_MODEL), f32)
    offset = jax.random.randint(ks[2], (BATCH, 1), 0, 1024, dtype=jnp.int32)
    positions = (offset + jnp.arange(SEQ, dtype=jnp.int32)[None, :]).astype(jnp.int32)
    rel_bias = jax.random.normal(ks[3], (REL_BUCKETS, DSA_HEADS), f32) * 0.5
    return {
        "x": x,
        "mem": mem,
        "positions": positions,
        "rel_bias": rel_bias,
        "mla_w_in": w(ks[4], (N_A, D_MODEL, MLA_IN), D_MODEL),
        "mla_q_norm": 1.0 + 0.01 * jax.random.normal(ks[5], (N_A, MLA_Q_RANK), f32),
        "mla_w_uq": w(ks[6], (N_A, MLA_Q_RANK, MLA_HEADS * (MLA_NOPE + MLA_ROPE)), MLA_Q_RANK),
        "mla_kv_norm": 1.0 + 0.01 * jax.random.normal(ks[7], (N_A, MLA_KV_RANK), f32),
        "mla_w_ukv": w(ks[8], (N_A, MLA_KV_RANK, MLA_HEADS * (MLA_NOPE + MLA_V)), MLA_KV_RANK),
        "mla_w_o": w(ks[9], (N_A, MLA_HEADS * MLA_V, D_MODEL), MLA_HEADS * MLA_V, BETA),
        "dsa_w_in": w(ks[10], (N_B, D_MODEL, DSA_IN), D_MODEL),
        "dsa_w_o": w(ks[11], (N_B, DSA_HEADS * DSA_HEAD_DIM, D_MODEL), DSA_HEADS * DSA_HEAD_DIM, BETA),
        "xa_w_q": w(ks[12], (DEPTH, D_MODEL, D_MODEL), D_MODEL),
        "xa_w_kv": w(ks[13], (DEPTH, D_MODEL, 2 * D_MODEL), D_MODEL),
        "xa_w_o": w(ks[14], (DEPTH, D_MODEL, D_MODEL), D_MODEL, BETA),
        "ffn_w_in": w(ks[15], (DEPTH, D_MODEL, 2 * FFN_HIDDEN), D_MODEL),
        "ffn_w_down": w(ks[16], (DEPTH, FFN_HIDDEN, D_MODEL), FFN_HIDDEN, BETA),
        "ln_g": 1.0 + 0.01 * jax.random.normal(ks[17], (DEPTH, 3, D_MODEL), f32),
        "ln_b": 0.01 * jax.random.normal(ks[18], (DEPTH, 3, D_MODEL), f32),
    }


def reference(x, mem, positions, rel_bias, mla_w_in, mla_q_norm, mla_w_uq, mla_kv_norm,
              mla_w_ukv, mla_w_o, dsa_w_in, dsa_w_o, xa_w_q, xa_w_kv, xa_w_o,
              ffn_w_in, ffn_w_down, ln_g, ln_b):
    for i in range(DEPTH):
        j = i // N_MIXERS
        if i % N_MIXERS == 0:
            y = mla_mixer(x, positions, mla_w_in[j], mla_q_norm[j], mla_w_uq[j],
                          mla_kv_norm[j], mla_w_ukv[j], mla_w_o[j])
        else:
            y = dsa_mixer(x, positions, rel_bias, dsa_w_in[j], dsa_w_o[j])
        x = layer_norm(ALPHA * x + y, ln_g[i, 0], ln_b[i, 0])
        y = memory_cross_attention(x, mem, xa_w_q[i], xa_w_kv[i], xa_w_o[i])
        x = layer_norm(ALPHA * x + y, ln_g[i, 1], ln_b[i, 1])
        y = swiglu_ffn(x, ffn_w_in[i], ffn_w_down[i])
        x = layer_norm(ALPHA * x + y, ln_g[i, 2], ln_b[i, 2])
    return x
```

```python
import functools
import math

import numpy as np
import jax
import jax.numpy as jnp
from jax import lax
from jax.experimental import pallas as pl
from jax.experimental.pallas import tpu as pltpu

D_MODEL = 1024
DEPTH = 2
MLA_HEADS = 16
MLA_Q_RANK = 512
MLA_KV_RANK = 256
MLA_NOPE = 64
MLA_ROPE = 32
MLA_V = 64
DSA_HEADS = 16
DSA_KV_HEADS = 4
DSA_GROUP = DSA_HEADS // DSA_KV_HEADS
DSA_HEAD_DIM = 64
IDX_HEADS = 8
IDX_DIM = 64
TOPK_MAX = 256
XA_HEADS = 4
XA_HEAD_DIM = D_MODEL // XA_HEADS
FFN_HIDDEN = -(-8 * D_MODEL // (3 * 256)) * 256
REL_BUCKETS = 32
REL_MAX_DIST = 128
ROPE_BASE = 10000.0
LN_EPS = 1e-5
RMS_EPS = 1e-6
ALPHA = (2 * DEPTH) ** 0.25

LANES = 128
HALF = LANES // 2
VMEM_LIMIT = 56 * 1024 * 1024

LOG2E = math.log2(math.e)
NEG_BIG = -1e30
F32 = jnp.float32
BF16 = jnp.bfloat16
I32 = jnp.int32
_NT = (((1,), (1,)), ((), ()))


def _dot(a, b):
    return jnp.dot(a, b, preferred_element_type=F32)


def _dot_nt(a, b):
    return lax.dot_general(a, b, _NT, preferred_element_type=F32)


def _layer_norm(y, g, b):
    mu = jnp.mean(y, axis=-1, keepdims=True)
    yc = y - mu
    var = jnp.mean(yc * yc, axis=-1, keepdims=True)
    return yc * lax.rsqrt(var + LN_EPS) * g + b


def _rms_norm(y, g):
    return y * lax.rsqrt(jnp.mean(y * y, axis=-1, keepdims=True) + RMS_EPS) * g


def _params(*sem):
    return pltpu.CompilerParams(dimension_semantics=sem, vmem_limit_bytes=VMEM_LIMIT)


def _const_spec(shape):
    nd = len(shape)
    return pl.BlockSpec(shape, lambda *_: (0,) * nd)


def _take_cols(w, idx):
    idx = np.asarray(idx)
    wp = jnp.concatenate([w, jnp.zeros((w.shape[0], 1), w.dtype)], axis=1)
    return wp[:, np.where(idx < 0, w.shape[1], idx)]


_R = MLA_ROPE // 2
_MLA_W = MLA_Q_RANK + MLA_KV_RANK


def _mla_proj_kernel(x_ref, pos_ref, win_ref, qg_ref, kvg_ref, wuq_ref, wuk_ref, wuv_ref,
                     freq_ref, sgn_ref, q_ref, k_ref, v_ref):
    xb = x_ref[0].astype(BF16)
    h = _dot(xb, win_ref[...])
    cq = _rms_norm(h[:, :MLA_Q_RANK], qg_ref[...]).astype(BF16)
    ckv = _rms_norm(h[:, MLA_Q_RANK:_MLA_W], kvg_ref[...]).astype(BF16)
    kr = h[:, _MLA_W:]
    ang = pos_ref[0].astype(F32) * freq_ref[...]
    cos = jnp.cos(ang)
    sin = jnp.sin(ang) * sgn_ref[...]

    def rope(t):
        return t * cos + pltpu.roll(t, HALF, 1) * sin

    kr = rope(kr)
    q_all = _dot(cq, wuq_ref[...])
    k_all = _dot(ckv, wuk_ref[...])
    v_all = _dot(ckv, wuv_ref[...])
    qscale = (MLA_NOPE + MLA_ROPE) ** -0.5 * LOG2E
    for hd in range(MLA_HEADS):
        sl = slice(hd * LANES, (hd + 1) * LANES)
        q_ref[0, hd] = (rope(q_all[:, sl]) * qscale).astype(BF16)
        k_ref[0, hd] = (k_all[:, sl] + kr).astype(BF16)
    for p in range(MLA_HEADS // 2):
        v_ref[0, p] = v_all[:, p * LANES:(p + 1) * LANES].astype(BF16)


def _mla_proj(x, pos3, w_in, q_norm, w_uq, kv_norm, w_ukv, tm):
    b, s, _ = x.shape
    hq = MLA_NOPE + MLA_ROPE
    in_idx = list(range(_MLA_W)) + [_MLA_W + j for j in range(_R)] + [-1] * (HALF - _R) \
        + [_MLA_W + _R + j for j in range(_R)] + [-1] * (HALF - _R)
    q_idx, k_idx, v_idx = [], [], []
    n_lo = HALF - _R
    n_hi = MLA_NOPE - n_lo
    tail = [-1] * (HALF - _R - n_hi)
    for hd in range(MLA_HEADS):
        qb = hd * hq
        q_idx += [qb + MLA_NOPE + j for j in range(_R)] + [qb + j for j in range(n_lo)] \
            + [qb + MLA_NOPE + _R + j for j in range(_R)] + [qb + n_lo + j for j in range(n_hi)] + tail
        kb = hd * (MLA_NOPE + MLA_V)
        k_idx += [-1] * _R + [kb + j for j in range(n_lo)] + [-1] * _R + [kb + n_lo + j for j in range(n_hi)] + tail
        v_idx += [kb + MLA_NOPE + j for j in range(MLA_V)]
    win = _take_cols(w_in, in_idx).astype(BF16)
    wuq = _take_cols(w_uq, q_idx).astype(BF16)
    wuk = _take_cols(w_ukv, k_idx).astype(BF16)
    wuv = _take_cols(w_ukv, v_idx).astype(BF16)
    inv_freq = ROPE_BASE ** (-jnp.arange(0, MLA_ROPE, 2, dtype=F32) / MLA_ROPE)
    zpad = jnp.zeros((HALF - _R,), F32)
    freq = jnp.concatenate([inv_freq, zpad, inv_freq, zpad])[None]
    sgn = jnp.concatenate([-jnp.ones((_R,), F32), zpad, jnp.ones((_R,), F32), zpad])[None]
    nq = MLA_HEADS * LANES
    grid = (b, s // tm)
    out_shape = (jax.ShapeDtypeStruct((b, MLA_HEADS, s, LANES), BF16),
                 jax.ShapeDtypeStruct((b, MLA_HEADS, s, LANES), BF16),
                 jax.ShapeDtypeStruct((b, MLA_HEADS // 2, s, LANES), BF16))
    return pl.pallas_call(
        _mla_proj_kernel, out_shape=out_shape, grid=grid,
        in_specs=[pl.BlockSpec((1, tm, D_MODEL), lambda bi, i: (bi, i, 0)),
                  pl.BlockSpec((1, tm, 1), lambda bi, i: (bi, i, 0)),
                  _const_spec((D_MODEL, _MLA_W + LANES)),
                  _const_spec((1, MLA_Q_RANK)), _const_spec((1, MLA_KV_RANK)),
                  _const_spec((MLA_Q_RANK, nq)), _const_spec((MLA_KV_RANK, nq)),
                  _const_spec((MLA_KV_RANK, MLA_HEADS * MLA_V)),
                  _const_spec((1, LANES)), _const_spec((1, LANES))],
        out_specs=(pl.BlockSpec((1, MLA_HEADS, tm, LANES), lambda bi, i: (bi, 0, i, 0)),
                   pl.BlockSpec((1, MLA_HEADS, tm, LANES), lambda bi, i: (bi, 0, i, 0)),
                   pl.BlockSpec((1, MLA_HEADS // 2, tm, LANES), lambda bi, i: (bi, 0, i, 0))),
        compiler_params=_params("parallel", "parallel"), name="mla_proj",
    )(x, pos3, win, q_norm[None], kv_norm[None], wuq, wuk, wuv, freq, sgn)


def _flash_kernel(q_ref, k_ref, v_ref, o_ref, *, t):
    i = pl.program_id(2)
    row = lax.broadcasted_iota(I32, (t, t), 0)
    col = lax.broadcasted_iota(I32, (t, t), 1)
    lane = lax.broadcasted_iota(I32, (t, LANES), 1)
    outs = []
    for hh in range(2):
        q = q_ref[0, hh]

        def update(j, carry, diag):
            m, l, acc = carry
            start = pl.multiple_of(j * t, t)
            kc = k_ref[0, hh, pl.ds(start, t), :]
            vc = v_ref[0, 0, pl.ds(start, t), :]
            sc = _dot_nt(q, kc)
            if diag:
                sc = jnp.where(col <= row, sc, NEG_BIG)
            m_new = jnp.maximum(m, jnp.max(sc, axis=-1, keepdims=True))
            a = jnp.exp2(m - m_new)
            p = jnp.exp2(sc - m_new)
            l = a * l + jnp.sum(p, axis=-1, keepdims=True)
            acc = a * acc + _dot(p.astype(BF16), vc)
            return m_new, l, acc

        init = (jnp.full((t, 1), NEG_BIG, F32), jnp.zeros((t, 1), F32), jnp.zeros((t, LANES), F32))
        carry = lax.fori_loop(0, i, lambda j, c: update(j, c, False), init)
        m, l, acc = update(i, carry, True)
        outs.append(acc / l)
    o_ref[0] = jnp.where(lane < HALF, outs[0], outs[1]).astype(BF16)


def _flash_mla(q, k, v, t):
    b, h, s, _ = q.shape
    return pl.pallas_call(
        functools.partial(_flash_kernel, t=t),
        out_shape=jax.ShapeDtypeStruct((b, s, h * MLA_V), BF16),
        grid=(b, h // 2, s // t),
        in_specs=[pl.BlockSpec((1, 2, t, LANES), lambda bi, p, i: (bi, p, i, 0)),
                  pl.BlockSpec((1, 2, s, LANES), lambda bi, p, i: (bi, p, 0, 0)),
                  pl.BlockSpec((1, 1, s, LANES), lambda bi, p, i: (bi, p, 0, 0))],
        out_specs=pl.BlockSpec((1, t, LANES), lambda bi, p, i: (bi, i, p)),
        compiler_params=_params("parallel", "parallel", "arbitrary"), name="mla_flash",
    )(q, k, v)


_DQ0 = 0
_DK0 = DSA_HEADS * LANES
_DV0 = _DK0 + DSA_KV_HEADS * LANES
_DI0 = _DV0 + DSA_KV_HEADS * LANES
_DW0 = _DI0 + IDX_HEADS * LANES
_DN = _DW0 + LANES


def _dsa_proj_kernel(x_ref, w_ref, qhot_ref, krow_ref, wsc_ref, qa_ref, ka_ref, v2_ref, qi_ref, kw_ref, wf_ref):
    xb = x_ref[0].astype(BF16)
    h = _dot(xb, w_ref[...])
    qscale = DSA_HEAD_DIM ** -0.5 * LOG2E
    for hd in range(DSA_HEADS):
        sl = slice(_DQ0 + hd * LANES, _DQ0 + (hd + 1) * LANES)
        qa_ref[0, hd] = (h[:, sl] * qscale + qhot_ref[hd:hd + 1, :]).astype(BF16)
    for g in range(DSA_KV_HEADS):
        ka_ref[0, g] = (h[:, _DK0 + g * LANES:_DK0 + (g + 1) * LANES] + krow_ref[g:g + 1, :]).astype(BF16)
        v2_ref[0, g] = h[:, _DV0 + g * LANES:_DV0 + (g + 1) * LANES].astype(BF16)
    for hd in range(IDX_HEADS):
        qi_ref[0, hd] = (h[:, _DI0 + hd * LANES:_DI0 + (hd + 1) * LANES] * (IDX_DIM ** -0.5)).astype(BF16)
    kw = h[:, _DW0:]
    kw_ref[0] = kw.astype(BF16)
    wf_ref[0] = kw * wsc_ref[...]


def _dsa_proj(x, w_in, rel_bias, tm):
    b, s, _ = x.shape
    o1 = DSA_HEADS * DSA_HEAD_DIM
    o2 = o1 + DSA_KV_HEADS * DSA_HEAD_DIM
    o3 = o2 + DSA_KV_HEADS * DSA_HEAD_DIM
    o4 = o3 + IDX_HEADS * IDX_DIM
    o5 = o4 + IDX_DIM
    idx = []
    for hd in range(DSA_HEADS):
        idx += [hd * DSA_HEAD_DIM + j for j in range(DSA_HEAD_DIM)] + [-1] * HALF
    for g in range(DSA_KV_HEADS):
        idx += [o1 + g * DSA_HEAD_DIM + j for j in range(DSA_HEAD_DIM)] + [-1] * HALF
    for g in range(DSA_KV_HEADS):
        idx += [o2 + g * DSA_HEAD_DIM + j for j in range(DSA_HEAD_DIM)] * 2
    for hd in range(IDX_HEADS):
        idx += [o3 + hd * IDX_DIM + j for j in range(IDX_DIM)] + [-1] * HALF
    idx += [o4 + j for j in range(IDX_DIM)] + [o5 + j for j in range(IDX_HEADS)] + [-1] * (HALF - IDX_HEADS)
    assert len(idx) == _DN
    w = _take_cols(w_in, idx).astype(BF16)
    cfar = rel_bias[REL_BUCKETS - 1] * LOG2E
    chi = cfar.astype(BF16).astype(F32)
    clo = (cfar - chi).astype(BF16).astype(F32)
    qhot = np.zeros((DSA_HEADS, LANES), np.float32)
    for hd in range(DSA_HEADS):
        qhot[hd, HALF + hd % DSA_GROUP] = 1.0
        qhot[hd, HALF + DSA_GROUP + hd % DSA_GROUP] = 1.0
    zpad = jnp.zeros((DSA_KV_HEADS, LANES - HALF - 2 * DSA_GROUP), F32)
    krow = jnp.concatenate([jnp.zeros((DSA_KV_HEADS, HALF), F32), chi.reshape(DSA_KV_HEADS, DSA_GROUP),
                            clo.reshape(DSA_KV_HEADS, DSA_GROUP), zpad], axis=1)
    wsc = np.zeros((1, LANES), np.float32)
    wsc[0, HALF:HALF + IDX_HEADS] = IDX_HEADS ** -0.5
    hspec = lambda n: pl.BlockSpec((1, n, tm, LANES), lambda bi, i: (bi, 0, i, 0))
    tspec = pl.BlockSpec((1, tm, LANES), lambda bi, i: (bi, i, 0))
    out_shape = (jax.ShapeDtypeStruct((b, DSA_HEADS, s, LANES), BF16),
                 jax.ShapeDtypeStruct((b, DSA_KV_HEADS, s, LANES), BF16),
                 jax.ShapeDtypeStruct((b, DSA_KV_HEADS, s, LANES), BF16),
                 jax.ShapeDtypeStruct((b, IDX_HEADS, s, LANES), BF16),
                 jax.ShapeDtypeStruct((b, s, LANES), BF16),
                 jax.ShapeDtypeStruct((b, s, LANES), F32))
    return pl.pallas_call(
        _dsa_proj_kernel, out_shape=out_shape, grid=(b, s // tm),
        in_specs=[pl.BlockSpec((1, tm, D_MODEL), lambda bi, i: (bi, i, 0)),
                  _const_spec((D_MODEL, _DN)), _const_spec((DSA_HEADS, LANES)),
                  _const_spec((DSA_KV_HEADS, LANES)), _const_spec((1, LANES))],
        out_specs=(hspec(DSA_HEADS), hspec(DSA_KV_HEADS), hspec(DSA_KV_HEADS), hspec(IDX_HEADS), tspec, tspec),
        compiler_params=_params("parallel", "parallel"), name="dsa_proj",
    )(x, w, jnp.asarray(qhot), krow, jnp.asarray(wsc))


def _t5_tiles_kernel(rb_ref, o_ref):
    d = pl.program_id(0)
    hd = pl.program_id(1)
    row = lax.broadcasted_iota(I32, (LANES, LANES), 0)
    col = lax.broadcasted_iota(I32, (LANES, LANES), 1)
    n = jnp.maximum(d * LANES + row - col, 0)
    max_exact = REL_BUCKETS // 2
    nf = jnp.maximum(n, 1).astype(F32)
    large = max_exact + (jnp.log(nf / max_exact) / math.log(REL_MAX_DIST / max_exact)
                         * (REL_BUCKETS - max_exact)).astype(I32)
    large = jnp.minimum(large, REL_BUCKETS - 1)
    bucket = jnp.where(n < max_exact, n, large)
    far = rb_ref[REL_BUCKETS - 1, hd]
    acc = jnp.zeros((LANES, LANES), F32)
    for bk in range(REL_BUCKETS - 1):
        acc = jnp.where(bucket == bk, rb_ref[bk, hd] - far, acc)
    o_ref[0, 0] = acc * LOG2E


def _t5_tiles(rel_bias):
    return pl.pallas_call(
        _t5_tiles_kernel,
        out_shape=jax.ShapeDtypeStruct((2, DSA_HEADS, LANES, LANES), F32),
        grid=(2, DSA_HEADS),
        in_specs=[pl.BlockSpec(memory_space=pltpu.SMEM)],
        out_specs=pl.BlockSpec((1, 1, LANES, LANES), lambda d, h: (d, h, 0, 0)),
        compiler_params=_params("parallel", "parallel"), name="t5_tiles",
    )(rel_bias)


_SENT_KEY = int(np.int32(np.uint32(0xFF800000) ^ np.uint32(0x7FFFFFFF)))
_INT_MIN = -(2 ** 31)


def _dsa_attn_kernel(qa_ref, qi_ref, wf_ref, ka_ref, v2_ref, kw_ref, tb_ref, o_ref,
                     key_sc, m_sc, l_sc, acc_sc, cut_sc, *, t, topk, s_len):
    i = pl.program_id(1)
    nsub = t // LANES
    row = lax.broadcasted_iota(I32, (t, t), 0)
    col = lax.broadcasted_iota(I32, (t, t), 1)

    qi = qi_ref[0].reshape(IDX_HEADS * t, LANES)
    wv = wf_ref[0]
    wcols = [wv[:, HALF + hd:HALF + hd + 1] for hd in range(IDX_HEADS)]

    def score_keys(j):
        kc = kw_ref[0, pl.ds(pl.multiple_of(j * t, t), t), :]
        d = _dot_nt(qi, kc)
        sc = wcols[0] * jnp.maximum(d[0:t], 0.0)
        for hd in range(1, IDX_HEADS):
            sc = sc + wcols[hd] * jnp.maximum(d[hd * t:(hd + 1) * t], 0.0)
        bits = pltpu.bitcast(sc + 0.0, I32)
        return bits ^ ((bits >> 31) & 0x7FFFFFFF)

    def p1(j, c):
        key_sc[j] = score_keys(j)
        return c

    lax.fori_loop(0, i, p1, 0)
    key_sc[i] = jnp.where(col <= row, score_keys(i), _SENT_KEY)

    def lane_fold(c):
        part = c[:, 0:LANES]
        for g in range(1, nsub):
            part = part + c[:, g * LANES:(g + 1) * LANES]
        return part

    def count(pred):
        def body(j, acc):
            return acc + lane_fold(jnp.where(pred(key_sc[j], j), 1.0, 0.0))
        acc = lax.fori_loop(0, i + 1, body, jnp.zeros((t, LANES), F32))
        return jnp.sum(acc, axis=-1, keepdims=True)

    kf = float(topk)
    cnt0 = count(lambda kc, j: kc >= 0)
    prefix = jnp.where(cnt0 >= kf, 0, _INT_MIN).astype(I32)

    def bit_body(bi, prefix):
        cand = prefix + lax.shift_left(jnp.int32(1), 30 - bi)
        cnt = count(lambda kc, j: kc >= cand)
        return jnp.where(cnt >= kf, cand, prefix)

    thr = lax.fori_loop(0, 31, bit_body, prefix)
    cnt_gt = count(lambda kc, j: kc > thr)
    cnt_ge = count(lambda kc, j: kc >= thr)
    need = kf - cnt_gt
    excess = jnp.where((cnt_ge > kf) & (thr > _SENT_KEY), 1.0, 0.0)
    cut_sc[...] = jnp.full((t, LANES), s_len, I32)

    @pl.when(jnp.max(excess) > 0.0)
    def _():
        nbits = max(1, (s_len - 1).bit_length())

        def tie_body(bi, c):
            cand = c + lax.shift_left(jnp.int32(1), nbits - 1 - bi)
            cnt = count(lambda kc, j: (kc == thr) & (col + j * t < cand))
            return jnp.where(cnt < need, cand, c)

        c = lax.fori_loop(0, nbits, tie_body, jnp.zeros((t, 1), I32))
        cut_sc[...] = jnp.broadcast_to(c, (t, LANES))

    cut = cut_sc[:, 0:1]

    def to_mask(j, causal):
        kc = key_sc[j]
        sel = (kc > thr) | ((kc == thr) & (col + j * t <= cut))
        if causal:
            sel = sel & (col <= row)
        key_sc[j] = pltpu.bitcast(jnp.where(sel, 0.0, NEG_BIG), I32)

    def p2(j, c):
        to_mask(j, False)
        return c

    lax.fori_loop(0, i, p2, 0)
    to_mask(i, True)

    m_sc[...] = jnp.full(m_sc.shape, NEG_BIG, F32)
    l_sc[...] = jnp.zeros(l_sc.shape, F32)
    acc_sc[...] = jnp.zeros(acc_sc.shape, F32)
    gt = DSA_GROUP * t

    def attend(j, delta0):
        start = pl.multiple_of(j * t, t)
        madd = pltpu.bitcast(key_sc[j], F32)
        for g in range(DSA_KV_HEADS):
            q = qa_ref[0, g * DSA_GROUP:(g + 1) * DSA_GROUP].reshape(gt, LANES)
            kc = ka_ref[0, g, pl.ds(start, t), :]
            vc = v2_ref[0, g, pl.ds(start, t), :]
            sc = _dot_nt(q, kc)
            rows = []
            for r in range(DSA_GROUP):
                sr = sc[r * t:(r + 1) * t] + madd
                if delta0 is not None:
                    hd = g * DSA_GROUP + r
                    blocks = []
                    for rs in range(nsub):
                        rowb = []
                        for cs in range(nsub):
                            blk = sr[rs * LANES:(rs + 1) * LANES, cs * LANES:(cs + 1) * LANES]
                            dl = delta0 + rs - cs
                            if dl in (0, 1):
                                blk = blk + tb_ref[dl, hd]
                            rowb.append(blk)
                        blocks.append(jnp.concatenate(rowb, axis=1) if nsub > 1 else rowb[0])
                    sr = jnp.concatenate(blocks, axis=0) if nsub > 1 else blocks[0]
                rows.append(sr)
            sc = jnp.concatenate(rows, axis=0)
            m_prev = m_sc[g]
            m_cur = jnp.max(sc, axis=-1, keepdims=True)
            m_new = jnp.maximum(m_prev, m_cur)
            a = jnp.exp2(m_prev - m_new)
            p = jnp.exp2(sc - m_new[:, 0:1])
            l_sc[g] = a * l_sc[g] + jnp.sum(p, axis=-1, keepdims=True)
            acc_sc[g] = a * acc_sc[g] + _dot(p.astype(BF16), vc)
            m_sc[g] = m_new

    def p3(j, c):
        attend(j, None)
        return c

    lax.fori_loop(0, jnp.maximum(i - 1, 0), p3, 0)

    @pl.when(i >= 1)
    def _():
        attend(i - 1, nsub)

    attend(i, 0)

    lane = lax.broadcasted_iota(I32, (t, LANES), 1)
    for g in range(DSA_KV_HEADS):
        o = acc_sc[g] / l_sc[g]
        for pr in range(DSA_GROUP // 2):
            lo = o[(2 * pr) * t:(2 * pr + 1) * t]
            hi = o[(2 * pr + 1) * t:(2 * pr + 2) * t]
            cblk = g * (DSA_GROUP // 2) + pr
            o_ref[0, :, cblk * LANES:(cblk + 1) * LANES] = jnp.where(lane < HALF, lo, hi).astype(BF16)


def _dsa_attn(qa, ka, v2, qi, kw, wf, tb, t, topk):
    b, _, s, _ = qa.shape
    nch = s // t
    kern = functools.partial(_dsa_attn_kernel, t=t, topk=topk, s_len=s)
    return pl.pallas_call(
        kern, out_shape=jax.ShapeDtypeStruct((b, s, DSA_HEADS * DSA_HEAD_DIM), BF16),
        grid=(b, nch),
        in_specs=[pl.BlockSpec((1, DSA_HEADS, t, LANES), lambda bi, i: (bi, 0, i, 0)),
                  pl.BlockSpec((1, IDX_HEADS, t, LANES), lambda bi, i: (bi, 0, i, 0)),
                  pl.BlockSpec((1, t, LANES), lambda bi, i: (bi, i, 0)),
                  pl.BlockSpec((1, DSA_KV_HEADS, s, LANES), lambda bi, i: (bi, 0, 0, 0)),
                  pl.BlockSpec((1, DSA_KV_HEADS, s, LANES), lambda bi, i: (bi, 0, 0, 0)),
                  pl.BlockSpec((1, s, LANES), lambda bi, i: (bi, 0, 0)),
                  _const_spec((2, DSA_HEADS, LANES, LANES))],
        out_specs=pl.BlockSpec((1, t, DSA_HEADS * DSA_HEAD_DIM), lambda bi, i: (bi, i, 0)),
        scratch_shapes=[pltpu.VMEM((nch, t, t), I32),
                        pltpu.VMEM((DSA_KV_HEADS, DSA_GROUP * t, LANES), F32),
                        pltpu.VMEM((DSA_KV_HEADS, DSA_GROUP * t, LANES), F32),
                        pltpu.VMEM((DSA_KV_HEADS, DSA_GROUP * t, LANES), F32),
                        pltpu.VMEM((t, LANES), I32)],
        compiler_params=_params("parallel", "arbitrary"), name="dsa_attn",
    )(qa, qi, wf, ka, v2, kw, tb)


def _mm_kernel(a_ref, w_ref, o_ref):
    o_ref[...] = _dot(a_ref[...].astype(BF16), w_ref[...]).astype(BF16)


def _mm(a, w, tm):
    m, k = a.shape
    n = w.shape[1]
    return pl.pallas_call(
        _mm_kernel, out_shape=jax.ShapeDtypeStruct((m, n), BF16), grid=(m // tm,),
        in_specs=[pl.BlockSpec((tm, k), lambda i: (i, 0)), _const_spec((k, n))],
        out_specs=pl.BlockSpec((tm, n), lambda i: (i, 0)),
        compiler_params=_params("parallel"), name="mem_kv",
    )(a, w)


def _post_kernel(a_ref, x_ref, wo_ref, wq_ref, kv_ref, wxo_ref, g_ref, b_ref, o_ref, oc_sc):
    y = _dot(a_ref[0], wo_ref[...])
    x1 = _layer_norm(ALPHA * x_ref[0] + y, g_ref[0:1, :], b_ref[0:1, :])
    q = (_dot(x1.astype(BF16), wq_ref[...]) * (XA_HEAD_DIM ** -0.5 * LOG2E)).astype(BF16)
    for hd in range(XA_HEADS):
        sl = slice(hd * XA_HEAD_DIM, (hd + 1) * XA_HEAD_DIM)
        sc = _dot_nt(q[:, sl], kv_ref[0, :, sl])
        m = jnp.max(sc, axis=-1, keepdims=True)
        p = jnp.exp2(sc - m)
        l = jnp.sum(p, axis=-1, keepdims=True)
        vsl = slice(D_MODEL + hd * XA_HEAD_DIM, D_MODEL + (hd + 1) * XA_HEAD_DIM)
        oc_sc[:, sl] = (_dot(p.astype(BF16), kv_ref[0, :, vsl]) / l).astype(BF16)
    y2 = _dot(oc_sc[...], wxo_ref[...])
    o_ref[0] = _layer_norm(ALPHA * x1 + y2, g_ref[1:2, :], b_ref[1:2, :])


def _post(a, x, w_o, w_q, kv, w_xo, g, bta, tm):
    b, s, _ = x.shape
    mlen = kv.shape[1]
    tok = lambda bi, i: (bi, i, 0)
    return pl.pallas_call(
        _post_kernel, out_shape=jax.ShapeDtypeStruct((b, s, D_MODEL), F32), grid=(b, s // tm),
        in_specs=[pl.BlockSpec((1, tm, D_MODEL), tok), pl.BlockSpec((1, tm, D_MODEL), tok),
                  _const_spec((D_MODEL, D_MODEL)), _const_spec((D_MODEL, D_MODEL)),
                  pl.BlockSpec((1, mlen, 2 * D_MODEL), lambda bi, i: (bi, 0, 0)),
                  _const_spec((D_MODEL, D_MODEL)), _const_spec((3, D_MODEL)), _const_spec((3, D_MODEL))],
        out_specs=pl.BlockSpec((1, tm, D_MODEL), tok),
        scratch_shapes=[pltpu.VMEM((tm, D_MODEL), BF16)],
        compiler_params=_params("parallel", "parallel"), name="post_attn",
    )(a, x, w_o, w_q, kv, w_xo, g, bta)


FFN_CHUNK = 256


def _ffn_kernel(x_ref, wg_ref, wu_ref, wd_ref, g_ref, b_ref, o_ref, acc_sc):
    x = x_ref[...]
    xb = x.astype(BF16)
    acc_sc[...] = jnp.zeros(acc_sc.shape, F32)

    def body(c, carry):
        gte = _dot(xb, wg_ref[c])
        up = _dot(xb, wu_ref[c])
        act = (gte * jax.nn.sigmoid(gte) * up).astype(BF16)
        acc_sc[...] += _dot(act, wd_ref[c])
        return carry

    lax.fori_loop(0, wg_ref.shape[0], body, 0)
    o_ref[...] = _layer_norm(ALPHA * x + acc_sc[...], g_ref[2:3, :], b_ref[2:3, :])


def _ffn(x2d, w_in, w_down, g, bta, tm):
    n = x2d.shape[0]
    nc = FFN_HIDDEN // FFN_CHUNK
    wg = w_in[:, :FFN_HIDDEN].reshape(D_MODEL, nc, FFN_CHUNK).transpose(1, 0, 2).astype(BF16)
    wu = w_in[:, FFN_HIDDEN:].reshape(D_MODEL, nc, FFN_CHUNK).transpose(1, 0, 2).astype(BF16)
    wd = w_down.reshape(nc, FFN_CHUNK, D_MODEL).astype(BF16)
    return pl.pallas_call(
        _ffn_kernel, out_shape=jax.ShapeDtypeStruct((n, D_MODEL), F32), grid=(n // tm,),
        in_specs=[pl.BlockSpec((tm, D_MODEL), lambda i: (i, 0)),
                  _const_spec((nc, D_MODEL, FFN_CHUNK)), _const_spec((nc, D_MODEL, FFN_CHUNK)),
                  _const_spec((nc, FFN_CHUNK, D_MODEL)), _const_spec((3, D_MODEL)), _const_spec((3, D_MODEL))],
        out_specs=pl.BlockSpec((tm, D_MODEL), lambda i: (i, 0)),
        scratch_shapes=[pltpu.VMEM((tm, D_MODEL), F32)],
        compiler_params=_params("parallel"), name="ffn",
    )(x2d, wg, wu, wd, g, bta)


def _tile(n, pref):
    t = min(n, pref)
    assert n % t == 0, (n, pref)
    return t


def kernel(x, mem, positions, rel_bias, mla_w_in, mla_q_norm, mla_w_uq, mla_kv_norm, mla_w_ukv, mla_w_o,
           dsa_w_in, dsa_w_o, xa_w_q, xa_w_kv, xa_w_o, ffn_w_in, ffn_w_down, ln_g, ln_b):
    b, s, d = x.shape
    assert d == D_MODEL and s % LANES == 0
    mlen = mem.shape[1]
    topk = min(TOPK_MAX, s // 4)
    t_attn = _tile(s, 256)
    tm_proj = _tile(s, 256)
    tm_post = _tile(s, 512)
    tm_ffn = _tile(b * s, 512)
    pos3 = positions.reshape(b, s, 1)
    tb = _t5_tiles(rel_bias)
    for i in range(DEPTH):
        j = i // 2
        if i % 2 == 0:
            q, k, v = _mla_proj(x, pos3, mla_w_in[j], mla_q_norm[j], mla_w_uq[j], mla_kv_norm[j],
                                mla_w_ukv[j], tm_proj)
            a = _flash_mla(q, k, v, t_attn)
            w_o = mla_w_o[j]
        else:
            qa, ka, v2, qi, kw, wf = _dsa_proj(x, dsa_w_in[j], rel_bias, tm_proj)
            a = _dsa_attn(qa, ka, v2, qi, kw, wf, tb, t_attn, topk)
            w_o = dsa_w_o[j]
        kv = _mm(mem.reshape(b * mlen, d), xa_w_kv[i].astype(BF16), _tile(b * mlen, 512)).reshape(b, mlen, 2 * d)
        x = _post(a, x, w_o.astype(BF16), xa_w_q[i].astype(BF16), kv, xa_w_o[i].astype(BF16),
                  ln_g[i], ln_b[i], tm_post)
        x = _ffn(x.reshape(b * s, d), ffn_w_in[i], ffn_w_down[i], ln_g[i], ln_b[i], tm_ffn).reshape(b, s, d)
    return x
```

```python
import functools
import math

import numpy as np
import jax
import jax.numpy as jnp
from jax import lax
from jax.experimental import pallas as pl
from jax.experimental.pallas import tpu as pltpu

D_MODEL = 1024
DEPTH = 2
MLA_HEADS = 16
MLA_Q_RANK = 512
MLA_KV_RANK = 256
MLA_NOPE = 64
MLA_ROPE = 32
MLA_V = 64
DSA_HEADS = 16
DSA_KV_HEADS = 4
DSA_GROUP = DSA_HEADS // DSA_KV_HEADS
DSA_HEAD_DIM = 64
IDX_HEADS = 8
IDX_DIM = 64
TOPK_MAX = 256
XA_HEADS = 4
XA_HEAD_DIM = D_MODEL // XA_HEADS
FFN_HIDDEN = -(-8 * D_MODEL // (3 * 256)) * 256
REL_BUCKETS = 32
REL_MAX_DIST = 128
ROPE_BASE = 10000.0
LN_EPS = 1e-5
RMS_EPS = 1e-6
ALPHA = (2 * DEPTH) ** 0.25

LANES = 128
HALF = LANES // 2
VMEM_LIMIT = 56 * 1024 * 1024

LOG2E = math.log2(math.e)
NEG_BIG = -1e30
F32 = jnp.float32
BF16 = jnp.bfloat16
I32 = jnp.int32
_NT = (((1,), (1,)), ((), ()))


def _dot(a, b):
    return jnp.dot(a, b, preferred_element_type=F32)


def _dot_nt(a, b):
    return lax.dot_general(a, b, _NT, preferred_element_type=F32)


def _layer_norm(y, g, b):
    mu = jnp.mean(y, axis=-1, keepdims=True)
    yc = y - mu
    var = jnp.mean(yc * yc, axis=-1, keepdims=True)
    return yc * lax.rsqrt(var + LN_EPS) * g + b


def _rms_norm(y, g):
    return y * lax.rsqrt(jnp.mean(y * y, axis=-1, keepdims=True) + RMS_EPS) * g


def _params(*sem):
    return pltpu.CompilerParams(dimension_semantics=sem, vmem_limit_bytes=VMEM_LIMIT)


def _const_spec(shape):
    nd = len(shape)
    return pl.BlockSpec(shape, lambda *_: (0,) * nd)


def _take_cols(w, idx):
    idx = np.asarray(idx)
    wp = jnp.concatenate([w, jnp.zeros((w.shape[0], 1), w.dtype)], axis=1)
    return wp[:, np.where(idx < 0, w.shape[1], idx)]


_R = MLA_ROPE // 2
_MLA_W = MLA_Q_RANK + MLA_KV_RANK


def _mla_proj_kernel(x_ref, pos_ref, win_ref, qg_ref, kvg_ref, wuq_ref, wuk_ref, wuv_ref,
                     freq_ref, sgn_ref, q_ref, k_ref, v_ref):
    xb = x_ref[0].astype(BF16)
    h = _dot(xb, win_ref[...])
    cq = _rms_norm(h[:, :MLA_Q_RANK], qg_ref[...]).astype(BF16)
    ckv = _rms_norm(h[:, MLA_Q_RANK:_MLA_W], kvg_ref[...]).astype(BF16)
    kr = h[:, _MLA_W:]
    ang = pos_ref[0].astype(F32) * freq_ref[...]
    cos = jnp.cos(ang)
    sin = jnp.sin(ang) * sgn_ref[...]

    def rope(t):
        return t * cos + pltpu.roll(t, HALF, 1) * sin

    kr = rope(kr)
    q_all = _dot(cq, wuq_ref[...])
    k_all = _dot(ckv, wuk_ref[...])
    v_all = _dot(ckv, wuv_ref[...])
    qscale = (MLA_NOPE + MLA_ROPE) ** -0.5 * LOG2E
    ones_hi = (lax.broadcasted_iota(I32, (1, LANES), 1) >= HALF).astype(F32)
    for hd in range(MLA_HEADS):
        sl = slice(hd * LANES, (hd + 1) * LANES)
        q_ref[0, hd] = (rope(q_all[:, sl]) * qscale).astype(BF16)
        k_ref[0, hd] = (k_all[:, sl] + kr).astype(BF16)
        v_ref[0, hd] = (v_all[:, sl] + ones_hi).astype(BF16)


def _mla_proj(x, pos3, w_in, q_norm, w_uq, kv_norm, w_ukv, tm):
    b, s, _ = x.shape
    hq = MLA_NOPE + MLA_ROPE
    in_idx = list(range(_MLA_W)) + [_MLA_W + j for j in range(_R)] + [-1] * (HALF - _R) \
        + [_MLA_W + _R + j for j in range(_R)] + [-1] * (HALF - _R)
    q_idx, k_idx, v_idx = [], [], []
    n_lo = HALF - _R
    n_hi = MLA_NOPE - n_lo
    tail = [-1] * (HALF - _R - n_hi)
    for hd in range(MLA_HEADS):
        qb = hd * hq
        q_idx += [qb + MLA_NOPE + j for j in range(_R)] + [qb + j for j in range(n_lo)] \
            + [qb + MLA_NOPE + _R + j for j in range(_R)] + [qb + n_lo + j for j in range(n_hi)] + tail
        kb = hd * (MLA_NOPE + MLA_V)
        k_idx += [-1] * _R + [kb + j for j in range(n_lo)] + [-1] * _R + [kb + n_lo + j for j in range(n_hi)] + tail
        v_idx += [kb + MLA_NOPE + j for j in range(MLA_V)] + [-1] * (LANES - MLA_V)
    win = _take_cols(w_in, in_idx).astype(BF16)
    wuq = _take_cols(w_uq, q_idx).astype(BF16)
    wuk = _take_cols(w_ukv, k_idx).astype(BF16)
    wuv = _take_cols(w_ukv, v_idx).astype(BF16)
    inv_freq = ROPE_BASE ** (-jnp.arange(0, MLA_ROPE, 2, dtype=F32) / MLA_ROPE)
    zpad = jnp.zeros((HALF - _R,), F32)
    freq = jnp.concatenate([inv_freq, zpad, inv_freq, zpad])[None]
    sgn = jnp.concatenate([-jnp.ones((_R,), F32), zpad, jnp.ones((_R,), F32), zpad])[None]
    nq = MLA_HEADS * LANES
    grid = (b, s // tm)
    out_shape = (jax.ShapeDtypeStruct((b, MLA_HEADS, s, LANES), BF16),
                 jax.ShapeDtypeStruct((b, MLA_HEADS, s, LANES), BF16),
                 jax.ShapeDtypeStruct((b, MLA_HEADS, s, LANES), BF16))
    return pl.pallas_call(
        _mla_proj_kernel, out_shape=out_shape, grid=grid,
        in_specs=[pl.BlockSpec((1, tm, D_MODEL), lambda bi, i: (bi, i, 0)),
                  pl.BlockSpec((1, tm, 1), lambda bi, i: (bi, i, 0)),
                  _const_spec((D_MODEL, _MLA_W + LANES)),
                  _const_spec((1, MLA_Q_RANK)), _const_spec((1, MLA_KV_RANK)),
                  _const_spec((MLA_Q_RANK, nq)), _const_spec((MLA_KV_RANK, nq)),
                  _const_spec((MLA_KV_RANK, nq)),
                  _const_spec((1, LANES)), _const_spec((1, LANES))],
        out_specs=(pl.BlockSpec((1, MLA_HEADS, tm, LANES), lambda bi, i: (bi, 0, i, 0)),
                   pl.BlockSpec((1, MLA_HEADS, tm, LANES), lambda bi, i: (bi, 0, i, 0)),
                   pl.BlockSpec((1, MLA_HEADS, tm, LANES), lambda bi, i: (bi, 0, i, 0))),
        compiler_params=_params("parallel", "parallel"), name="mla_proj",
    )(x, pos3, win, q_norm[None], kv_norm[None], wuq, wuk, wuv, freq, sgn)


def _flash_kernel(q_ref, k_ref, v_ref, o_ref, m_sc, acc_sc, s_sc, *, t):
    i = pl.program_id(2)
    row = lax.broadcasted_iota(I32, (t, t), 0)
    col = lax.broadcasted_iota(I32, (t, t), 1)
    m_sc[...] = jnp.full(m_sc.shape, NEG_BIG, F32)
    acc_sc[...] = jnp.zeros(acc_sc.shape, F32)

    def scores(j, hh):
        kc = k_ref[0, hh, pl.ds(pl.multiple_of(j * t, t), t), :]
        return _dot_nt(q_ref[0, hh], kc)

    def step(j, slot, diag):
        for hh in range(2):
            sc = s_sc[slot, hh]
            if diag:
                sc = jnp.where(col <= row, sc, NEG_BIG)
            else:
                s_sc[1 - slot, hh] = scores(j + 1, hh)
            m_prev = m_sc[hh]
            m_new = jnp.maximum(m_prev, jnp.max(sc, axis=-1, keepdims=True))
            a = jnp.exp2(m_prev - m_new)
            p = jnp.concatenate([jnp.exp2(sc[:, c * LANES:(c + 1) * LANES] - m_new)
                                 for c in range(t // LANES)], axis=1).astype(BF16)
            vc = v_ref[0, hh, pl.ds(pl.multiple_of(j * t, t), t), :]
            acc_sc[hh] = a * acc_sc[hh] + _dot(p, vc)
            m_sc[hh] = m_new

    for hh in range(2):
        s_sc[0, hh] = scores(0, hh)

    def pair(j2, c):
        step(2 * j2, 0, False)
        step(2 * j2 + 1, 1, False)
        return c

    lax.fori_loop(0, i // 2, pair, 0)

    @pl.when(i % 2 == 1)
    def _():
        step(i - 1, 0, False)
        step(i, 1, True)

    @pl.when(i % 2 == 0)
    def _():
        step(i, 0, True)

    lane = lax.broadcasted_iota(I32, (t, LANES), 1)
    a0 = acc_sc[0]
    a1 = acc_sc[1]
    o0 = a0 / a0[:, HALF:HALF + 1]
    o1 = a1 / a1[:, HALF:HALF + 1]
    o_ref[0] = jnp.where(lane < HALF, o0, pltpu.roll(o1, HALF, 1)).astype(BF16)


def _flash_mla(q, k, v, t):
    b, h, s, _ = q.shape
    return pl.pallas_call(
        functools.partial(_flash_kernel, t=t),
        out_shape=jax.ShapeDtypeStruct((b, s, h * MLA_V), BF16),
        grid=(b, h // 2, s // t),
        in_specs=[pl.BlockSpec((1, 2, t, LANES), lambda bi, p, i: (bi, p, i, 0)),
                  pl.BlockSpec((1, 2, s, LANES), lambda bi, p, i: (bi, p, 0, 0)),
                  pl.BlockSpec((1, 2, s, LANES), lambda bi, p, i: (bi, p, 0, 0))],
        out_specs=pl.BlockSpec((1, t, LANES), lambda bi, p, i: (bi, i, p)),
        scratch_shapes=[pltpu.VMEM((2, t, LANES), F32), pltpu.VMEM((2, t, LANES), F32),
                        pltpu.VMEM((2, 2, t, t), F32)],
        compiler_params=_params("parallel", "parallel", "arbitrary"), name="mla_flash",
    )(q, k, v)


_DQ0 = 0
_DK0 = DSA_HEADS * LANES
_DV0 = _DK0 + DSA_KV_HEADS * LANES
_DI0 = _DV0 + DSA_KV_HEADS * LANES
_DW0 = _DI0 + IDX_HEADS * LANES
_DN = _DW0 + LANES


def _dsa_proj_kernel(x_ref, w_ref, qhot_ref, krow_ref, wsc_ref, qa_ref, ka_ref, v2_ref, qi_ref, kw_ref, wf_ref):
    xb = x_ref[0].astype(BF16)
    h = _dot(xb, w_ref[...])
    qscale = DSA_HEAD_DIM ** -0.5 * LOG2E
    for hd in range(DSA_HEADS):
        sl = slice(_DQ0 + hd * LANES, _DQ0 + (hd + 1) * LANES)
        qa_ref[0, hd] = (h[:, sl] * qscale + qhot_ref[hd:hd + 1, :]).astype(BF16)
    for g in range(DSA_KV_HEADS):
        ka_ref[0, g] = (h[:, _DK0 + g * LANES:_DK0 + (g + 1) * LANES] + krow_ref[g:g + 1, :]).astype(BF16)
        v2_ref[0, g] = h[:, _DV0 + g * LANES:_DV0 + (g + 1) * LANES].astype(BF16)
    for hd in range(IDX_HEADS):
        qi_ref[0, hd] = (h[:, _DI0 + hd * LANES:_DI0 + (hd + 1) * LANES] * (IDX_DIM ** -0.5)).astype(BF16)
    kw = h[:, _DW0:]
    kw_ref[0] = kw.astype(BF16)
    wf_ref[0] = kw * wsc_ref[...]


def _dsa_proj(x, w_in, rel_bias, tm):
    b, s, _ = x.shape
    o1 = DSA_HEADS * DSA_HEAD_DIM
    o2 = o1 + DSA_KV_HEADS * DSA_HEAD_DIM
    o3 = o2 + DSA_KV_HEADS * DSA_HEAD_DIM
    o4 = o3 + IDX_HEADS * IDX_DIM
    o5 = o4 + IDX_DIM
    idx = []
    for hd in range(DSA_HEADS):
        idx += [hd * DSA_HEAD_DIM + j for j in range(DSA_HEAD_DIM)] + [-1] * HALF
    for g in range(DSA_KV_HEADS):
        idx += [o1 + g * DSA_HEAD_DIM + j for j in range(DSA_HEAD_DIM)] + [-1] * HALF
    for g in range(DSA_KV_HEADS):
        idx += [o2 + g * DSA_HEAD_DIM + j for j in range(DSA_HEAD_DIM)] * 2
    for hd in range(IDX_HEADS):
        idx += [o3 + hd * IDX_DIM + j for j in range(IDX_DIM)] + [-1] * HALF
    idx += [o4 + j for j in range(IDX_DIM)] + [o5 + j for j in range(IDX_HEADS)] + [-1] * (HALF - IDX_HEADS)
    assert len(idx) == _DN
    w = _take_cols(w_in, idx).astype(BF16)
    cfar = rel_bias[REL_BUCKETS - 1] * LOG2E
    chi = cfar.astype(BF16).astype(F32)
    clo = (cfar - chi).astype(BF16).astype(F32)
    qhot = np.zeros((DSA_HEADS, LANES), np.float32)
    for hd in range(DSA_HEADS):
        qhot[hd, HALF + hd % DSA_GROUP] = 1.0
        qhot[hd, HALF + DSA_GROUP + hd % DSA_GROUP] = 1.0
    zpad = jnp.zeros((DSA_KV_HEADS, LANES - HALF - 2 * DSA_GROUP), F32)
    krow = jnp.concatenate([jnp.zeros((DSA_KV_HEADS, HALF), F32), chi.reshape(DSA_KV_HEADS, DSA_GROUP),
                            clo.reshape(DSA_KV_HEADS, DSA_GROUP), zpad], axis=1)
    wsc = np.zeros((1, LANES), np.float32)
    wsc[0, HALF:HALF + IDX_HEADS] = IDX_HEADS ** -0.5
    hspec = lambda n: pl.BlockSpec((1, n, tm, LANES), lambda bi, i: (bi, 0, i, 0))
    tspec = pl.BlockSpec((1, tm, LANES), lambda bi, i: (bi, i, 0))
    out_shape = (jax.ShapeDtypeStruct((b, DSA_HEADS, s, LANES), BF16),
                 jax.ShapeDtypeStruct((b, DSA_KV_HEADS, s, LANES), BF16),
                 jax.ShapeDtypeStruct((b, DSA_KV_HEADS, s, LANES), BF16),
                 jax.ShapeDtypeStruct((b, IDX_HEADS, s, LANES), BF16),
                 jax.ShapeDtypeStruct((b, s, LANES), BF16),
                 jax.ShapeDtypeStruct((b, s, LANES), F32))
    return pl.pallas_call(
        _dsa_proj_kernel, out_shape=out_shape, grid=(b, s // tm),
        in_specs=[pl.BlockSpec((1, tm, D_MODEL), lambda bi, i: (bi, i, 0)),
                  _const_spec((D_MODEL, _DN)), _const_spec((DSA_HEADS, LANES)),
                  _const_spec((DSA_KV_HEADS, LANES)), _const_spec((1, LANES))],
        out_specs=(hspec(DSA_HEADS), hspec(DSA_KV_HEADS), hspec(DSA_KV_HEADS), hspec(IDX_HEADS), tspec, tspec),
        compiler_params=_params("parallel", "parallel"), name="dsa_proj",
    )(x, w, jnp.asarray(qhot), krow, jnp.asarray(wsc))


def _t5_tiles_kernel(rb_ref, o_ref):
    d = pl.program_id(0)
    hd = pl.program_id(1)
    row = lax.broadcasted_iota(I32, (LANES, LANES), 0)
    col = lax.broadcasted_iota(I32, (LANES, LANES), 1)
    n = jnp.maximum(d * LANES + row - col, 0)
    max_exact = REL_BUCKETS // 2
    nf = jnp.maximum(n, 1).astype(F32)
    large = max_exact + (jnp.log(nf / max_exact) / math.log(REL_MAX_DIST / max_exact)
                         * (REL_BUCKETS - max_exact)).astype(I32)
    large = jnp.minimum(large, REL_BUCKETS - 1)
    bucket = jnp.where(n < max_exact, n, large)
    far = rb_ref[REL_BUCKETS - 1, hd]
    acc = jnp.zeros((LANES, LANES), F32)
    for bk in range(REL_BUCKETS - 1):
        acc = jnp.where(bucket == bk, rb_ref[bk, hd] - far, acc)
    o_ref[0, 0] = acc * LOG2E


def _t5_tiles(rel_bias):
    return pl.pallas_call(
        _t5_tiles_kernel,
        out_shape=jax.ShapeDtypeStruct((2, DSA_HEADS, LANES, LANES), F32),
        grid=(2, DSA_HEADS),
        in_specs=[pl.BlockSpec(memory_space=pltpu.SMEM)],
        out_specs=pl.BlockSpec((1, 1, LANES, LANES), lambda d, h: (d, h, 0, 0)),
        compiler_params=_params("parallel", "parallel"), name="t5_tiles",
    )(rel_bias)


_SENT_KEY = int(np.int32(np.uint32(0xFF800000) ^ np.uint32(0x7FFFFFFF)))
_INT_MIN = -(2 ** 31)


def _dsa_attn_kernel(qa_ref, qi_ref, wf_ref, ka_ref, v2_ref, kw_ref, tb_ref, o_ref,
                     key_sc, m_sc, l_sc, acc_sc, cut_sc, *, t, topk, s_len):
    i = pl.program_id(1)
    nsub = t // LANES
    row = lax.broadcasted_iota(I32, (t, t), 0)
    col = lax.broadcasted_iota(I32, (t, t), 1)

    qi = qi_ref[0].reshape(IDX_HEADS * t, LANES)
    wv = wf_ref[0]
    wcols = [wv[:, HALF + hd:HALF + hd + 1] for hd in range(IDX_HEADS)]

    def score_keys(j):
        kc = kw_ref[0, pl.ds(pl.multiple_of(j * t, t), t), :]
        d = _dot_nt(qi, kc)
        sc = wcols[0] * jnp.maximum(d[0:t], 0.0)
        for hd in range(1, IDX_HEADS):
            sc = sc + wcols[hd] * jnp.maximum(d[hd * t:(hd + 1) * t], 0.0)
        bits = pltpu.bitcast(sc + 0.0, I32)
        return bits ^ ((bits >> 31) & 0x7FFFFFFF)

    def p1(j, c):
        key_sc[j] = score_keys(j)
        return c

    lax.fori_loop(0, i, p1, 0)
    key_sc[i] = jnp.where(col <= row, score_keys(i), _SENT_KEY)

    def lane_fold(c):
        part = c[:, 0:LANES]
        for g in range(1, nsub):
            part = part + c[:, g * LANES:(g + 1) * LANES]
        return part

    def count(pred):
        def body(j, acc):
            return acc + lane_fold(jnp.where(pred(key_sc[j], j), 1.0, 0.0))
        acc = lax.fori_loop(0, i + 1, body, jnp.zeros((t, LANES), F32))
        return jnp.sum(acc, axis=-1, keepdims=True)

    kf = float(topk)
    cnt0 = count(lambda kc, j: kc >= 0)
    prefix = jnp.where(cnt0 >= kf, 0, _INT_MIN).astype(I32)

    def bit_body(bi, prefix):
        cand = prefix + lax.shift_left(jnp.int32(1), 30 - bi)
        cnt = count(lambda kc, j: kc >= cand)
        return jnp.where(cnt >= kf, cand, prefix)

    thr = lax.fori_loop(0, 31, bit_body, prefix)
    cnt_gt = count(lambda kc, j: kc > thr)
    cnt_ge = count(lambda kc, j: kc >= thr)
    need = kf - cnt_gt
    excess = jnp.where((cnt_ge > kf) & (thr > _SENT_KEY), 1.0, 0.0)
    cut_sc[...] = jnp.full((t, LANES), s_len, I32)

    @pl.when(jnp.max(excess) > 0.0)
    def _():
        nbits = max(1, (s_len - 1).bit_length())

        def tie_body(bi, c):
            cand = c + lax.shift_left(jnp.int32(1), nbits - 1 - bi)
            cnt = count(lambda kc, j: (kc == thr) & (col + j * t < cand))
            return jnp.where(cnt < need, cand, c)

        c = lax.fori_loop(0, nbits, tie_body, jnp.zeros((t, 1), I32))
        cut_sc[...] = jnp.broadcast_to(c, (t, LANES))

    cut = cut_sc[:, 0:1]

    def to_mask(j, causal):
        kc = key_sc[j]
        sel = (kc > thr) | ((kc == thr) & (col + j * t <= cut))
        if causal:
            sel = sel & (col <= row)
        key_sc[j] = pltpu.bitcast(jnp.where(sel, 0.0, NEG_BIG), I32)

    def p2(j, c):
        to_mask(j, False)
        return c

    lax.fori_loop(0, i, p2, 0)
    to_mask(i, True)

    m_sc[...] = jnp.full(m_sc.shape, NEG_BIG, F32)
    l_sc[...] = jnp.zeros(l_sc.shape, F32)
    acc_sc[...] = jnp.zeros(acc_sc.shape, F32)
    gt = DSA_GROUP * t

    def attend(j, delta0):
        start = pl.multiple_of(j * t, t)
        madd = pltpu.bitcast(key_sc[j], F32)
        for g in range(DSA_KV_HEADS):
            q = qa_ref[0, g * DSA_GROUP:(g + 1) * DSA_GROUP].reshape(gt, LANES)
            kc = ka_ref[0, g, pl.ds(start, t), :]
            vc = v2_ref[0, g, pl.ds(start, t), :]
            sc = _dot_nt(q, kc)
            rows = []
            for r in range(DSA_GROUP):
                sr = sc[r * t:(r + 1) * t] + madd
                if delta0 is not None:
                    hd = g * DSA_GROUP + r
                    blocks = []
                    for rs in range(nsub):
                        rowb = []
                        for cs in range(nsub):
                            blk = sr[rs * LANES:(rs + 1) * LANES, cs * LANES:(cs + 1) * LANES]
                            dl = delta0 + rs - cs
                            if dl in (0, 1):
                                blk = blk + tb_ref[dl, hd]
                            rowb.append(blk)
                        blocks.append(jnp.concatenate(rowb, axis=1) if nsub > 1 else rowb[0])
                    sr = jnp.concatenate(blocks, axis=0) if nsub > 1 else blocks[0]
                rows.append(sr)
            sc = jnp.concatenate(rows, axis=0)
            m_prev = m_sc[g]
            m_cur = jnp.max(sc, axis=-1, keepdims=True)
            m_new = jnp.maximum(m_prev, m_cur)
            a = jnp.exp2(m_prev - m_new)
            p = jnp.exp2(sc - m_new[:, 0:1])
            l_sc[g] = a * l_sc[g] + jnp.sum(p, axis=-1, keepdims=True)
            acc_sc[g] = a * acc_sc[g] + _dot(p.astype(BF16), vc)
            m_sc[g] = m_new

    def p3(j, c):
        attend(j, None)
        return c

    lax.fori_loop(0, jnp.maximum(i - 1, 0), p3, 0)

    @pl.when(i >= 1)
    def _():
        attend(i - 1, nsub)

    attend(i, 0)

    lane = lax.broadcasted_iota(I32, (t, LANES), 1)
    for g in range(DSA_KV_HEADS):
        o = acc_sc[g] / l_sc[g]
        for pr in range(DSA_GROUP // 2):
            lo = o[(2 * pr) * t:(2 * pr + 1) * t]
            hi = o[(2 * pr + 1) * t:(2 * pr + 2) * t]
            cblk = g * (DSA_GROUP // 2) + pr
            o_ref[0, :, cblk * LANES:(cblk + 1) * LANES] = jnp.where(lane < HALF, lo, hi).astype(BF16)


def _dsa_attn(qa, ka, v2, qi, kw, wf, tb, t, topk):
    b, _, s, _ = qa.shape
    nch = s // t
    kern = functools.partial(_dsa_attn_kernel, t=t, topk=topk, s_len=s)
    return pl.pallas_call(
        kern, out_shape=jax.ShapeDtypeStruct((b, s, DSA_HEADS * DSA_HEAD_DIM), BF16),
        grid=(b, nch),
        in_specs=[pl.BlockSpec((1, DSA_HEADS, t, LANES), lambda bi, i: (bi, 0, i, 0)),
                  pl.BlockSpec((1, IDX_HEADS, t, LANES), lambda bi, i: (bi, 0, i, 0)),
                  pl.BlockSpec((1, t, LANES), lambda bi, i: (bi, i, 0)),
                  pl.BlockSpec((1, DSA_KV_HEADS, s, LANES), lambda bi, i: (bi, 0, 0, 0)),
                  pl.BlockSpec((1, DSA_KV_HEADS, s, LANES), lambda bi, i: (bi, 0, 0, 0)),
                  pl.BlockSpec((1, s, LANES), lambda bi, i: (bi, 0, 0)),
                  _const_spec((2, DSA_HEADS, LANES, LANES))],
        out_specs=pl.BlockSpec((1, t, DSA_HEADS * DSA_HEAD_DIM), lambda bi, i: (bi, i, 0)),
        scratch_shapes=[pltpu.VMEM((nch, t, t), I32),
                        pltpu.VMEM((DSA_KV_HEADS, DSA_GROUP * t, LANES), F32),
                        pltpu.VMEM((DSA_KV_HEADS, DSA_GROUP * t, LANES), F32),
                        pltpu.VMEM((DSA_KV_HEADS, DSA_GROUP * t, LANES), F32),
                        pltpu.VMEM((t, LANES), I32)],
        compiler_params=_params("parallel", "arbitrary"), name="dsa_attn",
    )(qa, qi, wf, ka, v2, kw, tb)


def _mm_kernel(a_ref, w_ref, o_ref):
    o_ref[...] = _dot(a_ref[...].astype(BF16), w_ref[...]).astype(BF16)


def _mm(a, w, tm):
    m, k = a.shape
    n = w.shape[1]
    return pl.pallas_call(
        _mm_kernel, out_shape=jax.ShapeDtypeStruct((m, n), BF16), grid=(m // tm,),
        in_specs=[pl.BlockSpec((tm, k), lambda i: (i, 0)), _const_spec((k, n))],
        out_specs=pl.BlockSpec((tm, n), lambda i: (i, 0)),
        compiler_params=_params("parallel"), name="mem_kv",
    )(a, w)


def _post_kernel(a_ref, x_ref, wo_ref, wq_ref, kv_ref, wxo_ref, g_ref, b_ref, o_ref, oc_sc):
    y = _dot(a_ref[0], wo_ref[...])
    x1 = _layer_norm(ALPHA * x_ref[0] + y, g_ref[0:1, :], b_ref[0:1, :])
    q = (_dot(x1.astype(BF16), wq_ref[...]) * (XA_HEAD_DIM ** -0.5 * LOG2E)).astype(BF16)
    for hd in range(XA_HEADS):
        sl = slice(hd * XA_HEAD_DIM, (hd + 1) * XA_HEAD_DIM)
        sc = _dot_nt(q[:, sl], kv_ref[0, :, sl])
        m = jnp.max(sc, axis=-1, keepdims=True)
        p = jnp.exp2(sc - m)
        l = jnp.sum(p, axis=-1, keepdims=True)
        vsl = slice(D_MODEL + hd * XA_HEAD_DIM, D_MODEL + (hd + 1) * XA_HEAD_DIM)
        oc_sc[:, sl] = (_dot(p.astype(BF16), kv_ref[0, :, vsl]) / l).astype(BF16)
    y2 = _dot(oc_sc[...], wxo_ref[...])
    o_ref[0] = _layer_norm(ALPHA * x1 + y2, g_ref[1:2, :], b_ref[1:2, :])


def _post(a, x, w_o, w_q, kv, w_xo, g, bta, tm):
    b, s, _ = x.shape
    mlen = kv.shape[1]
    tok = lambda bi, i: (bi, i, 0)
    return pl.pallas_call(
        _post_kernel, out_shape=jax.ShapeDtypeStruct((b, s, D_MODEL), F32), grid=(b, s // tm),
        in_specs=[pl.BlockSpec((1, tm, D_MODEL), tok), pl.BlockSpec((1, tm, D_MODEL), tok),
                  _const_spec((D_MODEL, D_MODEL)), _const_spec((D_MODEL, D_MODEL)),
                  pl.BlockSpec((1, mlen, 2 * D_MODEL), lambda bi, i: (bi, 0, 0)),
                  _const_spec((D_MODEL, D_MODEL)), _const_spec((3, D_MODEL)), _const_spec((3, D_MODEL))],
        out_specs=pl.BlockSpec((1, tm, D_MODEL), tok),
        scratch_shapes=[pltpu.VMEM((tm, D_MODEL), BF16)],
        compiler_params=_params("parallel", "parallel"), name="post_attn",
    )(a, x, w_o, w_q, kv, w_xo, g, bta)


FFN_CHUNK = 256


def _ffn_kernel(x_ref, wg_ref, wu_ref, wd_ref, g_ref, b_ref, o_ref, acc_sc):
    x = x_ref[...]
    xb = x.astype(BF16)
    acc_sc[...] = jnp.zeros(acc_sc.shape, F32)

    def body(c, carry):
        gte = _dot(xb, wg_ref[c])
        up = _dot(xb, wu_ref[c])
        act = (gte * jax.nn.sigmoid(gte) * up).astype(BF16)
        acc_sc[...] += _dot(act, wd_ref[c])
        return carry

    lax.fori_loop(0, wg_ref.shape[0], body, 0)
    o_ref[...] = _layer_norm(ALPHA * x + acc_sc[...], g_ref[2:3, :], b_ref[2:3, :])


def _ffn(x2d, w_in, w_down, g, bta, tm):
    n = x2d.shape[0]
    nc = FFN_HIDDEN // FFN_CHUNK
    wg = w_in[:, :FFN_HIDDEN].reshape(D_MODEL, nc, FFN_CHUNK).transpose(1, 0, 2).astype(BF16)
    wu = w_in[:, FFN_HIDDEN:].reshape(D_MODEL, nc, FFN_CHUNK).transpose(1, 0, 2).astype(BF16)
    wd = w_down.reshape(nc, FFN_CHUNK, D_MODEL).astype(BF16)
    return pl.pallas_call(
        _ffn_kernel, out_shape=jax.ShapeDtypeStruct((n, D_MODEL), F32), grid=(n // tm,),
        in_specs=[pl.BlockSpec((tm, D_MODEL), lambda i: (i, 0)),
                  _const_spec((nc, D_MODEL, FFN_CHUNK)), _const_spec((nc, D_MODEL, FFN_CHUNK)),
                  _const_spec((nc, FFN_CHUNK, D_MODEL)), _const_spec((3, D_MODEL)), _const_spec((3, D_MODEL))],
        out_specs=pl.BlockSpec((tm, D_MODEL), lambda i: (i, 0)),
        scratch_shapes=[pltpu.VMEM((tm, D_MODEL), F32)],
        compiler_params=_params("parallel"), name="ffn",
    )(x2d, wg, wu, wd, g, bta)


def _tile(n, pref):
    t = min(n, pref)
    assert n % t == 0, (n, pref)
    return t


def kernel(x, mem, positions, rel_bias, mla_w_in, mla_q_norm, mla_w_uq, mla_kv_norm, mla_w_ukv, mla_w_o,
           dsa_w_in, dsa_w_o, xa_w_q, xa_w_kv, xa_w_o, ffn_w_in, ffn_w_down, ln_g, ln_b):
    b, s, d = x.shape
    assert d == D_MODEL and s % LANES == 0
    mlen = mem.shape[1]
    topk = min(TOPK_MAX, s // 4)
    t_attn = _tile(s, 256)
    tm_proj = _tile(s, 256)
    tm_post = _tile(s, 512)
    tm_ffn = _tile(b * s, 512)
    pos3 = positions.reshape(b, s, 1)
    tb = _t5_tiles(rel_bias)
    for i in range(DEPTH):
        j = i // 2
        if i % 2 == 0:
            q, k, v = _mla_proj(x, pos3, mla_w_in[j], mla_q_norm[j], mla_w_uq[j], mla_kv_norm[j],
                                mla_w_ukv[j], tm_proj)
            a = _flash_mla(q, k, v, _tile(s, 512))
            w_o = mla_w_o[j]
        else:
            qa, ka, v2, qi, kw, wf = _dsa_proj(x, dsa_w_in[j], rel_bias, tm_proj)
            a = _dsa_attn(qa, ka, v2, qi, kw, wf, tb, t_attn, topk)
            w_o = dsa_w_o[j]
        kv = _mm(mem.reshape(b * mlen, d), xa_w_kv[i].astype(BF16), _tile(b * mlen, 512)).reshape(b, mlen, 2 * d)
        x = _post(a, x, w_o.astype(BF16), xa_w_q[i].astype(BF16), kv, xa_w_o[i].astype(BF16),
                  ln_g[i], ln_b[i], tm_post)
        x = _ffn(x.reshape(b * s, d), ffn_w_in[i], ffn_w_down[i], ln_g[i], ln_b[i], tm_ffn).reshape(b, s, d)
    return x
```

```python
import functools
import math

import numpy as np
import jax
import jax.numpy as jnp
from jax import lax
from jax.experimental import pallas as pl
from jax.experimental.pallas import tpu as pltpu

D_MODEL = 1024
DEPTH = 2
MLA_HEADS = 16
MLA_Q_RANK = 512
MLA_KV_RANK = 256
MLA_NOPE = 64
MLA_ROPE = 32
MLA_V = 64
DSA_HEADS = 16
DSA_KV_HEADS = 4
DSA_GROUP = DSA_HEADS // DSA_KV_HEADS
DSA_HEAD_DIM = 64
IDX_HEADS = 8
IDX_DIM = 64
TOPK_MAX = 256
XA_HEADS = 4
XA_HEAD_DIM = D_MODEL // XA_HEADS
FFN_HIDDEN = -(-8 * D_MODEL // (3 * 256)) * 256
REL_BUCKETS = 32
REL_MAX_DIST = 128
ROPE_BASE = 10000.0
LN_EPS = 1e-5
RMS_EPS = 1e-6
ALPHA = (2 * DEPTH) ** 0.25

LANES = 128
HALF = LANES // 2
VMEM_LIMIT = 56 * 1024 * 1024

LOG2E = math.log2(math.e)
NEG_BIG = -1e30
F32 = jnp.float32
BF16 = jnp.bfloat16
I32 = jnp.int32
_NT = (((1,), (1,)), ((), ()))


def _dot(a, b):
    return jnp.dot(a, b, preferred_element_type=F32)


def _dot_nt(a, b):
    return lax.dot_general(a, b, _NT, preferred_element_type=F32)


def _layer_norm(y, g, b):
    mu = jnp.mean(y, axis=-1, keepdims=True)
    yc = y - mu
    var = jnp.mean(yc * yc, axis=-1, keepdims=True)
    return yc * lax.rsqrt(var + LN_EPS) * g + b


def _rms_norm(y, g):
    return y * lax.rsqrt(jnp.mean(y * y, axis=-1, keepdims=True) + RMS_EPS) * g


def _params(*sem):
    return pltpu.CompilerParams(dimension_semantics=sem, vmem_limit_bytes=VMEM_LIMIT)


def _const_spec(shape):
    nd = len(shape)
    return pl.BlockSpec(shape, lambda *_: (0,) * nd)


def _take_cols(w, idx):
    idx = np.asarray(idx)
    wp = jnp.concatenate([w, jnp.zeros((w.shape[0], 1), w.dtype)], axis=1)
    return wp[:, np.where(idx < 0, w.shape[1], idx)]


_R = MLA_ROPE // 2
_MLA_W = MLA_Q_RANK + MLA_KV_RANK


def _mla_proj_kernel(x_ref, pos_ref, win_ref, qg_ref, kvg_ref, wuq_ref, wuk_ref, wuv_ref,
                     freq_ref, sgn_ref, q_ref, k_ref, v_ref):
    xb = x_ref[0].astype(BF16)
    h = _dot(xb, win_ref[...])
    cq = _rms_norm(h[:, :MLA_Q_RANK], qg_ref[...]).astype(BF16)
    ckv = _rms_norm(h[:, MLA_Q_RANK:_MLA_W], kvg_ref[...]).astype(BF16)
    kr = h[:, _MLA_W:]
    ang = pos_ref[0].astype(F32) * freq_ref[...]
    cos = jnp.cos(ang)
    sin = jnp.sin(ang) * sgn_ref[...]

    def rope(t):
        return t * cos + pltpu.roll(t, HALF, 1) * sin

    kr = rope(kr)
    q_all = _dot(cq, wuq_ref[...])
    k_all = _dot(ckv, wuk_ref[...])
    v_all = _dot(ckv, wuv_ref[...])
    qscale = (MLA_NOPE + MLA_ROPE) ** -0.5 * LOG2E
    ones_hi = (lax.broadcasted_iota(I32, (1, LANES), 1) >= HALF).astype(F32)
    for hd in range(MLA_HEADS):
        sl = slice(hd * LANES, (hd + 1) * LANES)
        q_ref[0, hd] = (rope(q_all[:, sl]) * qscale).astype(BF16)
        k_ref[0, hd] = (k_all[:, sl] + kr).astype(BF16)
        v_ref[0, hd] = (v_all[:, sl] + ones_hi).astype(BF16)


def _mla_proj(x, pos3, w_in, q_norm, w_uq, kv_norm, w_ukv, tm):
    b, s, _ = x.shape
    hq = MLA_NOPE + MLA_ROPE
    in_idx = list(range(_MLA_W)) + [_MLA_W + j for j in range(_R)] + [-1] * (HALF - _R) \
        + [_MLA_W + _R + j for j in range(_R)] + [-1] * (HALF - _R)
    q_idx, k_idx, v_idx = [], [], []
    n_lo = HALF - _R
    n_hi = MLA_NOPE - n_lo
    tail = [-1] * (HALF - _R - n_hi)
    for hd in range(MLA_HEADS):
        qb = hd * hq
        q_idx += [qb + MLA_NOPE + j for j in range(_R)] + [qb + j for j in range(n_lo)] \
            + [qb + MLA_NOPE + _R + j for j in range(_R)] + [qb + n_lo + j for j in range(n_hi)] + tail
        kb = hd * (MLA_NOPE + MLA_V)
        k_idx += [-1] * _R + [kb + j for j in range(n_lo)] + [-1] * _R + [kb + n_lo + j for j in range(n_hi)] + tail
        v_idx += [kb + MLA_NOPE + j for j in range(MLA_V)] + [-1] * (LANES - MLA_V)
    win = _take_cols(w_in, in_idx).astype(BF16)
    wuq = _take_cols(w_uq, q_idx).astype(BF16)
    wuk = _take_cols(w_ukv, k_idx).astype(BF16)
    wuv = _take_cols(w_ukv, v_idx).astype(BF16)
    inv_freq = ROPE_BASE ** (-jnp.arange(0, MLA_ROPE, 2, dtype=F32) / MLA_ROPE)
    zpad = jnp.zeros((HALF - _R,), F32)
    freq = jnp.concatenate([inv_freq, zpad, inv_freq, zpad])[None]
    sgn = jnp.concatenate([-jnp.ones((_R,), F32), zpad, jnp.ones((_R,), F32), zpad])[None]
    nq = MLA_HEADS * LANES
    grid = (b, s // tm)
    out_shape = (jax.ShapeDtypeStruct((b, MLA_HEADS, s, LANES), BF16),
                 jax.ShapeDtypeStruct((b, MLA_HEADS, s, LANES), BF16),
                 jax.ShapeDtypeStruct((b, MLA_HEADS, s, LANES), BF16))
    return pl.pallas_call(
        _mla_proj_kernel, out_shape=out_shape, grid=grid,
        in_specs=[pl.BlockSpec((1, tm, D_MODEL), lambda bi, i: (bi, i, 0)),
                  pl.BlockSpec((1, tm, 1), lambda bi, i: (bi, i, 0)),
                  _const_spec((D_MODEL, _MLA_W + LANES)),
                  _const_spec((1, MLA_Q_RANK)), _const_spec((1, MLA_KV_RANK)),
                  _const_spec((MLA_Q_RANK, nq)), _const_spec((MLA_KV_RANK, nq)),
                  _const_spec((MLA_KV_RANK, nq)),
                  _const_spec((1, LANES)), _const_spec((1, LANES))],
        out_specs=(pl.BlockSpec((1, MLA_HEADS, tm, LANES), lambda bi, i: (bi, 0, i, 0)),
                   pl.BlockSpec((1, MLA_HEADS, tm, LANES), lambda bi, i: (bi, 0, i, 0)),
                   pl.BlockSpec((1, MLA_HEADS, tm, LANES), lambda bi, i: (bi, 0, i, 0))),
        compiler_params=_params("parallel", "parallel"), name="mla_proj",
    )(x, pos3, win, q_norm[None], kv_norm[None], wuq, wuk, wuv, freq, sgn)


def _flash_kernel(q_ref, k_ref, v_ref, o_ref, m_sc, acc_sc, s_sc, *, t):
    i = pl.program_id(2)
    row = lax.broadcasted_iota(I32, (t, t), 0)
    col = lax.broadcasted_iota(I32, (t, t), 1)
    m_sc[...] = jnp.full(m_sc.shape, NEG_BIG, F32)
    acc_sc[...] = jnp.zeros(acc_sc.shape, F32)

    def scores(j, hh):
        kc = k_ref[0, hh, pl.ds(pl.multiple_of(j * t, t), t), :]
        return _dot_nt(q_ref[0, hh], kc)

    def step(j, slot, diag):
        for hh in range(2):
            sc = s_sc[slot, hh]
            if diag:
                sc = jnp.where(col <= row, sc, NEG_BIG)
            else:
                s_sc[1 - slot, hh] = scores(j + 1, hh)
            m_prev = m_sc[hh]
            m_new = jnp.maximum(m_prev, jnp.max(sc, axis=-1, keepdims=True))
            a = jnp.exp2(m_prev - m_new)
            p = jnp.concatenate([jnp.exp2(sc[:, c * LANES:(c + 1) * LANES] - m_new)
                                 for c in range(t // LANES)], axis=1).astype(BF16)
            vc = v_ref[0, hh, pl.ds(pl.multiple_of(j * t, t), t), :]
            acc_sc[hh] = a * acc_sc[hh] + _dot(p, vc)
            m_sc[hh] = m_new

    for hh in range(2):
        s_sc[0, hh] = scores(0, hh)

    def pair(j2, c):
        step(2 * j2, 0, False)
        step(2 * j2 + 1, 1, False)
        return c

    lax.fori_loop(0, i // 2, pair, 0)

    @pl.when(i % 2 == 1)
    def _():
        step(i - 1, 0, False)
        step(i, 1, True)

    @pl.when(i % 2 == 0)
    def _():
        step(i, 0, True)

    lane = lax.broadcasted_iota(I32, (t, LANES), 1)
    a0 = acc_sc[0]
    a1 = acc_sc[1]
    o0 = a0 / a0[:, HALF:HALF + 1]
    o1 = a1 / a1[:, HALF:HALF + 1]
    o_ref[0] = jnp.where(lane < HALF, o0, pltpu.roll(o1, HALF, 1)).astype(BF16)


def _flash_mla(q, k, v, t):
    b, h, s, _ = q.shape
    return pl.pallas_call(
        functools.partial(_flash_kernel, t=t),
        out_shape=jax.ShapeDtypeStruct((b, s, h * MLA_V), BF16),
        grid=(b, h // 2, s // t),
        in_specs=[pl.BlockSpec((1, 2, t, LANES), lambda bi, p, i: (bi, p, i, 0)),
                  pl.BlockSpec((1, 2, s, LANES), lambda bi, p, i: (bi, p, 0, 0)),
                  pl.BlockSpec((1, 2, s, LANES), lambda bi, p, i: (bi, p, 0, 0))],
        out_specs=pl.BlockSpec((1, t, LANES), lambda bi, p, i: (bi, i, p)),
        scratch_shapes=[pltpu.VMEM((2, t, LANES), F32), pltpu.VMEM((2, t, LANES), F32),
                        pltpu.VMEM((2, 2, t, t), F32)],
        compiler_params=_params("parallel", "parallel", "arbitrary"), name="mla_flash",
    )(q, k, v)


_DQ0 = 0
_DK0 = DSA_HEADS * LANES
_DV0 = _DK0 + DSA_KV_HEADS * LANES
_DI0 = _DV0 + DSA_KV_HEADS * LANES
_DW0 = _DI0 + IDX_HEADS * LANES
_DN = _DW0 + LANES


def _dsa_proj_kernel(x_ref, w_ref, qhot_ref, krow_ref, wsc_ref, qa_ref, ka_ref, v2_ref, qi_ref, kw_ref, wf_ref):
    xb = x_ref[0].astype(BF16)
    h = _dot(xb, w_ref[...])
    qscale = DSA_HEAD_DIM ** -0.5 * LOG2E
    for hd in range(DSA_HEADS):
        sl = slice(_DQ0 + hd * LANES, _DQ0 + (hd + 1) * LANES)
        qa_ref[0, hd] = (h[:, sl] * qscale + qhot_ref[hd:hd + 1, :]).astype(BF16)
    ones_hi = (lax.broadcasted_iota(I32, (1, LANES), 1) >= HALF).astype(F32)
    for g in range(DSA_KV_HEADS):
        ka_ref[0, g] = (h[:, _DK0 + g * LANES:_DK0 + (g + 1) * LANES] + krow_ref[g:g + 1, :]).astype(BF16)
        v2_ref[0, g] = (h[:, _DV0 + g * LANES:_DV0 + (g + 1) * LANES] + ones_hi).astype(BF16)
    for hd in range(IDX_HEADS):
        qi_ref[0, hd] = (h[:, _DI0 + hd * LANES:_DI0 + (hd + 1) * LANES] * (IDX_DIM ** -0.5)).astype(BF16)
    kw = h[:, _DW0:]
    kw_ref[0] = kw.astype(BF16)
    wf_ref[0] = kw * wsc_ref[...]


def _dsa_proj(x, w_in, rel_bias, tm):
    b, s, _ = x.shape
    o1 = DSA_HEADS * DSA_HEAD_DIM
    o2 = o1 + DSA_KV_HEADS * DSA_HEAD_DIM
    o3 = o2 + DSA_KV_HEADS * DSA_HEAD_DIM
    o4 = o3 + IDX_HEADS * IDX_DIM
    o5 = o4 + IDX_DIM
    idx = []
    for hd in range(DSA_HEADS):
        idx += [hd * DSA_HEAD_DIM + j for j in range(DSA_HEAD_DIM)] + [-1] * HALF
    for g in range(DSA_KV_HEADS):
        idx += [o1 + g * DSA_HEAD_DIM + j for j in range(DSA_HEAD_DIM)] + [-1] * HALF
    for g in range(DSA_KV_HEADS):
        idx += [o2 + g * DSA_HEAD_DIM + j for j in range(DSA_HEAD_DIM)] + [-1] * HALF
    for hd in range(IDX_HEADS):
        idx += [o3 + hd * IDX_DIM + j for j in range(IDX_DIM)] + [-1] * HALF
    idx += [o4 + j for j in range(IDX_DIM)] + [o5 + j for j in range(IDX_HEADS)] + [-1] * (HALF - IDX_HEADS)
    assert len(idx) == _DN
    w = _take_cols(w_in, idx).astype(BF16)
    cfar = rel_bias[REL_BUCKETS - 1] * LOG2E
    chi = cfar.astype(BF16).astype(F32)
    clo = (cfar - chi).astype(BF16).astype(F32)
    qhot = np.zeros((DSA_HEADS, LANES), np.float32)
    for hd in range(DSA_HEADS):
        qhot[hd, HALF + hd % DSA_GROUP] = 1.0
        qhot[hd, HALF + DSA_GROUP + hd % DSA_GROUP] = 1.0
    zpad = jnp.zeros((DSA_KV_HEADS, LANES - HALF - 2 * DSA_GROUP), F32)
    krow = jnp.concatenate([jnp.zeros((DSA_KV_HEADS, HALF), F32), chi.reshape(DSA_KV_HEADS, DSA_GROUP),
                            clo.reshape(DSA_KV_HEADS, DSA_GROUP), zpad], axis=1)
    wsc = np.zeros((1, LANES), np.float32)
    wsc[0, HALF:HALF + IDX_HEADS] = IDX_HEADS ** -0.5
    hspec = lambda n: pl.BlockSpec((1, n, tm, LANES), lambda bi, i: (bi, 0, i, 0))
    tspec = pl.BlockSpec((1, tm, LANES), lambda bi, i: (bi, i, 0))
    out_shape = (jax.ShapeDtypeStruct((b, DSA_HEADS, s, LANES), BF16),
                 jax.ShapeDtypeStruct((b, DSA_KV_HEADS, s, LANES), BF16),
                 jax.ShapeDtypeStruct((b, DSA_KV_HEADS, s, LANES), BF16),
                 jax.ShapeDtypeStruct((b, IDX_HEADS, s, LANES), BF16),
                 jax.ShapeDtypeStruct((b, s, LANES), BF16),
                 jax.ShapeDtypeStruct((b, s, LANES), F32))
    return pl.pallas_call(
        _dsa_proj_kernel, out_shape=out_shape, grid=(b, s // tm),
        in_specs=[pl.BlockSpec((1, tm, D_MODEL), lambda bi, i: (bi, i, 0)),
                  _const_spec((D_MODEL, _DN)), _const_spec((DSA_HEADS, LANES)),
                  _const_spec((DSA_KV_HEADS, LANES)), _const_spec((1, LANES))],
        out_specs=(hspec(DSA_HEADS), hspec(DSA_KV_HEADS), hspec(DSA_KV_HEADS), hspec(IDX_HEADS), tspec, tspec),
        compiler_params=_params("parallel", "parallel"), name="dsa_proj",
    )(x, w, jnp.asarray(qhot), krow, jnp.asarray(wsc))


def _t5_tiles_kernel(rb_ref, o_ref):
    d = pl.program_id(0)
    hd = pl.program_id(1)
    row = lax.broadcasted_iota(I32, (LANES, LANES), 0)
    col = lax.broadcasted_iota(I32, (LANES, LANES), 1)
    n = jnp.maximum(d * LANES + row - col, 0)
    max_exact = REL_BUCKETS // 2
    nf = jnp.maximum(n, 1).astype(F32)
    large = max_exact + (jnp.log(nf / max_exact) / math.log(REL_MAX_DIST / max_exact)
                         * (REL_BUCKETS - max_exact)).astype(I32)
    large = jnp.minimum(large, REL_BUCKETS - 1)
    bucket = jnp.where(n < max_exact, n, large)
    far = rb_ref[REL_BUCKETS - 1, hd]
    acc = jnp.zeros((LANES, LANES), F32)
    for bk in range(REL_BUCKETS - 1):
        acc = jnp.where(bucket == bk, rb_ref[bk, hd] - far, acc)
    o_ref[0, 0] = acc * LOG2E


def _t5_tiles(rel_bias):
    return pl.pallas_call(
        _t5_tiles_kernel,
        out_shape=jax.ShapeDtypeStruct((2, DSA_HEADS, LANES, LANES), F32),
        grid=(2, DSA_HEADS),
        in_specs=[pl.BlockSpec(memory_space=pltpu.SMEM)],
        out_specs=pl.BlockSpec((1, 1, LANES, LANES), lambda d, h: (d, h, 0, 0)),
        compiler_params=_params("parallel", "parallel"), name="t5_tiles",
    )(rel_bias)


_SENT_KEY = int(np.int32(np.uint32(0xFF800000) ^ np.uint32(0x7FFFFFFF)))
_INT_MIN = -(2 ** 31)


def _dsa_attn_kernel(qa_ref, qi_ref, wf_ref, ka_ref, v_ref, kw_ref, tb_ref, o_ref,
                     keyt_sc, madd_sc, s_sc, m_sc, acc_sc, cut_sc, *, t, topk, s_len):
    i = pl.program_id(1)
    nsub = t // LANES
    kidx = lax.broadcasted_iota(I32, (t, t), 0)
    qidx = lax.broadcasted_iota(I32, (t, t), 1)

    qi = qi_ref[0].reshape(IDX_HEADS * t, LANES)
    wv = wf_ref[0]
    wcols = [wv[:, HALF + hd:HALF + hd + 1] for hd in range(IDX_HEADS)]

    def score_keys_t(j):
        kc = kw_ref[0, pl.ds(pl.multiple_of(j * t, t), t), :]
        d = _dot_nt(qi, kc)
        sc = wcols[0] * jnp.maximum(d[0:t], 0.0)
        for hd in range(1, IDX_HEADS):
            sc = sc + wcols[hd] * jnp.maximum(d[hd * t:(hd + 1) * t], 0.0)
        bits = pltpu.bitcast(sc.T + 0.0, I32)
        return bits ^ ((bits >> 31) & 0x7FFFFFFF)

    def p1(j, c):
        keyt_sc[j] = score_keys_t(j)
        return c

    lax.fori_loop(0, i, p1, 0)
    keyt_sc[i] = jnp.where(kidx <= qidx, score_keys_t(i), _SENT_KEY)

    def count(pred):
        def body(j, acc):
            c = jnp.where(pred(keyt_sc[j], j), 1.0, 0.0)
            return acc + jnp.sum(c.reshape(t // 8, 8, t), axis=0)
        acc = lax.fori_loop(0, i + 1, body, jnp.zeros((8, t), F32))
        return jnp.sum(acc, axis=0, keepdims=True)

    kf = float(topk)
    cnt0 = count(lambda kc, j: kc >= 0)
    prefix = jnp.where(cnt0 >= kf, 0, _INT_MIN).astype(I32)

    def bit_body(bi, prefix):
        cand = prefix + lax.shift_left(jnp.int32(1), 30 - bi)
        cnt = count(lambda kc, j: kc >= cand)
        return jnp.where(cnt >= kf, cand, prefix)

    thr = lax.fori_loop(0, 31, bit_body, prefix)
    cnt_gt = count(lambda kc, j: kc > thr)
    cnt_ge = count(lambda kc, j: kc >= thr)
    need = kf - cnt_gt
    excess = jnp.where((cnt_ge > kf) & (thr > _SENT_KEY), 1.0, 0.0)
    cut_sc[...] = jnp.full(cut_sc.shape, s_len, I32)

    @pl.when(jnp.max(excess) > 0.0)
    def _():
        nbits = max(1, (s_len - 1).bit_length())

        def tie_body(bi, c):
            cand = c + lax.shift_left(jnp.int32(1), nbits - 1 - bi)
            cnt = count(lambda kc, j: (kc == thr) & (kidx + j * t < cand))
            return jnp.where(cnt < need, cand, c)

        c = lax.fori_loop(0, nbits, tie_body, jnp.zeros((1, t), I32))
        cut_sc[...] = jnp.broadcast_to(c, cut_sc.shape)

    cut = cut_sc[0:1, :]

    def to_mask(j, causal):
        kc = keyt_sc[j]
        sel = (kc > thr) | ((kc == thr) & (kidx + j * t <= cut))
        if causal:
            sel = sel & (kidx <= qidx)
        madd_sc[j] = jnp.where(sel, 0.0, NEG_BIG).T

    def p2(j, c):
        to_mask(j, False)
        return c

    lax.fori_loop(0, i, p2, 0)
    to_mask(i, True)

    m_sc[...] = jnp.full(m_sc.shape, NEG_BIG, F32)
    acc_sc[...] = jnp.zeros(acc_sc.shape, F32)
    gt = DSA_GROUP * t
    near_delta = {"prev": nsub, "diag": 0}

    def attend(j0, parts):
        w = len(parts)
        start = pl.multiple_of(j0 * t, t)

        def qk(g):
            q = qa_ref[0, g * DSA_GROUP:(g + 1) * DSA_GROUP].reshape(gt, LANES)
            return _dot_nt(q, ka_ref[0, g, pl.ds(start, w * t), :])

        s_sc[0, :, 0:w * t] = qk(0)
        for g in range(DSA_KV_HEADS):
            if g + 1 < DSA_KV_HEADS:
                s_sc[(g + 1) % 2, :, 0:w * t] = qk(g + 1)
            slot = g % 2
            p_rows, a_rows = [], []
            for r in range(DSA_GROUP):
                hd = g * DSA_GROUP + r
                for rs in range(nsub):
                    r0 = r * t + rs * LANES
                    blocks = []
                    for cb in range(w * nsub):
                        part, cs = parts[cb // nsub], cb % nsub
                        blk = s_sc[slot, r0:r0 + LANES, cb * LANES:(cb + 1) * LANES] \
                            + madd_sc[j0 + cb // nsub, rs * LANES:(rs + 1) * LANES, cs * LANES:(cs + 1) * LANES]
                        if part in near_delta and near_delta[part] + rs - cs in (0, 1):
                            blk = blk + tb_ref[near_delta[part] + rs - cs, hd]
                        blocks.append(blk)
                    m_prev = m_sc[g, r0:r0 + LANES]
                    m_blk = blocks[0]
                    for blk in blocks[1:]:
                        m_blk = jnp.maximum(m_blk, blk)
                    m_new = jnp.maximum(m_prev, jnp.max(m_blk, axis=-1, keepdims=True))
                    a_rows.append(jnp.exp2(m_prev - m_new))
                    p_rows.append(jnp.concatenate([jnp.exp2(blk - m_new).astype(BF16) for blk in blocks], axis=1))
                    m_sc[g, r0:r0 + LANES] = m_new
            p = jnp.concatenate(p_rows, axis=0)
            a = jnp.concatenate(a_rows, axis=0)
            vc = v_ref[0, g, pl.ds(start, w * t), :]
            acc_sc[g] = a * acc_sc[g] + _dot(p, vc)

    @pl.when(i == 0)
    def _():
        attend(0, ("diag",))

    @pl.when(i >= 1)
    def _():
        nfar = i - 1

        def far_pair(u, c):
            attend(2 * u, ("far", "far"))
            return c

        lax.fori_loop(0, nfar // 2, far_pair, 0)

        @pl.when(nfar % 2 == 1)
        def _():
            attend(i - 2, ("far",))

        attend(i - 1, ("prev", "diag"))

    lane = lax.broadcasted_iota(I32, (t, LANES), 1)
    for g in range(DSA_KV_HEADS):
        acc = acc_sc[g]
        o = acc / acc[:, HALF:HALF + 1]
        for pr in range(DSA_GROUP // 2):
            lo = o[(2 * pr) * t:(2 * pr + 1) * t]
            hi = o[(2 * pr + 1) * t:(2 * pr + 2) * t]
            cblk = g * (DSA_GROUP // 2) + pr
            o_ref[0, :, cblk * LANES:(cblk + 1) * LANES] = \
                jnp.where(lane < HALF, lo, pltpu.roll(hi, HALF, 1)).astype(BF16)


def _dsa_attn(qa, ka, v, qi, kw, wf, tb, t, topk):
    b, _, s, _ = qa.shape
    nch = s // t
    gt = DSA_GROUP * t
    kern = functools.partial(_dsa_attn_kernel, t=t, topk=topk, s_len=s)
    return pl.pallas_call(
        kern, out_shape=jax.ShapeDtypeStruct((b, s, DSA_HEADS * DSA_HEAD_DIM), BF16),
        grid=(b, nch),
        in_specs=[pl.BlockSpec((1, DSA_HEADS, t, LANES), lambda bi, i: (bi, 0, i, 0)),
                  pl.BlockSpec((1, IDX_HEADS, t, LANES), lambda bi, i: (bi, 0, i, 0)),
                  pl.BlockSpec((1, t, LANES), lambda bi, i: (bi, i, 0)),
                  pl.BlockSpec((1, DSA_KV_HEADS, s, LANES), lambda bi, i: (bi, 0, 0, 0)),
                  pl.BlockSpec((1, DSA_KV_HEADS, s, LANES), lambda bi, i: (bi, 0, 0, 0)),
                  pl.BlockSpec((1, s, LANES), lambda bi, i: (bi, 0, 0)),
                  _const_spec((2, DSA_HEADS, LANES, LANES))],
        out_specs=pl.BlockSpec((1, t, DSA_HEADS * DSA_HEAD_DIM), lambda bi, i: (bi, i, 0)),
        scratch_shapes=[pltpu.VMEM((nch, t, t), I32),
                        pltpu.VMEM((nch, t, t), F32),
                        pltpu.VMEM((2, gt, 2 * t), F32),
                        pltpu.VMEM((DSA_KV_HEADS, gt, LANES), F32),
                        pltpu.VMEM((DSA_KV_HEADS, gt, LANES), F32),
                        pltpu.VMEM((8, t), I32)],
        compiler_params=_params("parallel", "arbitrary"), name="dsa_attn",
    )(qa, qi, wf, ka, v, kw, tb)


def _mm_kernel(a_ref, w_ref, o_ref):
    o_ref[...] = _dot(a_ref[...].astype(BF16), w_ref[...]).astype(BF16)


def _mm(a, w, tm):
    m, k = a.shape
    n = w.shape[1]
    return pl.pallas_call(
        _mm_kernel, out_shape=jax.ShapeDtypeStruct((m, n), BF16), grid=(m // tm,),
        in_specs=[pl.BlockSpec((tm, k), lambda i: (i, 0)), _const_spec((k, n))],
        out_specs=pl.BlockSpec((tm, n), lambda i: (i, 0)),
        compiler_params=_params("parallel"), name="mem_kv",
    )(a, w)


def _post_kernel(a_ref, x_ref, wo_ref, wq_ref, kv_ref, wxo_ref, g_ref, b_ref, o_ref, oc_sc):
    y = _dot(a_ref[0], wo_ref[...])
    x1 = _layer_norm(ALPHA * x_ref[0] + y, g_ref[0:1, :], b_ref[0:1, :])
    q = (_dot(x1.astype(BF16), wq_ref[...]) * (XA_HEAD_DIM ** -0.5 * LOG2E)).astype(BF16)
    for hd in range(XA_HEADS):
        sl = slice(hd * XA_HEAD_DIM, (hd + 1) * XA_HEAD_DIM)
        sc = _dot_nt(q[:, sl], kv_ref[0, :, sl])
        m = jnp.max(sc, axis=-1, keepdims=True)
        p = jnp.exp2(sc - m)
        l = jnp.sum(p, axis=-1, keepdims=True)
        vsl = slice(D_MODEL + hd * XA_HEAD_DIM, D_MODEL + (hd + 1) * XA_HEAD_DIM)
        oc_sc[:, sl] = (_dot(p.astype(BF16), kv_ref[0, :, vsl]) / l).astype(BF16)
    y2 = _dot(oc_sc[...], wxo_ref[...])
    o_ref[0] = _layer_norm(ALPHA * x1 + y2, g_ref[1:2, :], b_ref[1:2, :])


def _post(a, x, w_o, w_q, kv, w_xo, g, bta, tm):
    b, s, _ = x.shape
    mlen = kv.shape[1]
    tok = lambda bi, i: (bi, i, 0)
    return pl.pallas_call(
        _post_kernel, out_shape=jax.ShapeDtypeStruct((b, s, D_MODEL), F32), grid=(b, s // tm),
        in_specs=[pl.BlockSpec((1, tm, D_MODEL), tok), pl.BlockSpec((1, tm, D_MODEL), tok),
                  _const_spec((D_MODEL, D_MODEL)), _const_spec((D_MODEL, D_MODEL)),
                  pl.BlockSpec((1, mlen, 2 * D_MODEL), lambda bi, i: (bi, 0, 0)),
                  _const_spec((D_MODEL, D_MODEL)), _const_spec((3, D_MODEL)), _const_spec((3, D_MODEL))],
        out_specs=pl.BlockSpec((1, tm, D_MODEL), tok),
        scratch_shapes=[pltpu.VMEM((tm, D_MODEL), BF16)],
        compiler_params=_params("parallel", "parallel"), name="post_attn",
    )(a, x, w_o, w_q, kv, w_xo, g, bta)


FFN_CHUNK = 256


def _ffn_kernel(x_ref, wg_ref, wu_ref, wd_ref, g_ref, b_ref, o_ref, acc_sc):
    x = x_ref[...]
    xb = x.astype(BF16)
    acc_sc[...] = jnp.zeros(acc_sc.shape, F32)

    def body(c, carry):
        gte = _dot(xb, wg_ref[c])
        up = _dot(xb, wu_ref[c])
        act = (gte * jax.nn.sigmoid(gte) * up).astype(BF16)
        acc_sc[...] += _dot(act, wd_ref[c])
        return carry

    lax.fori_loop(0, wg_ref.shape[0], body, 0)
    o_ref[...] = _layer_norm(ALPHA * x + acc_sc[...], g_ref[2:3, :], b_ref[2:3, :])


def _ffn(x2d, w_in, w_down, g, bta, tm):
    n = x2d.shape[0]
    nc = FFN_HIDDEN // FFN_CHUNK
    wg = w_in[:, :FFN_HIDDEN].reshape(D_MODEL, nc, FFN_CHUNK).transpose(1, 0, 2).astype(BF16)
    wu = w_in[:, FFN_HIDDEN:].reshape(D_MODEL, nc, FFN_CHUNK).transpose(1, 0, 2).astype(BF16)
    wd = w_down.reshape(nc, FFN_CHUNK, D_MODEL).astype(BF16)
    return pl.pallas_call(
        _ffn_kernel, out_shape=jax.ShapeDtypeStruct((n, D_MODEL), F32), grid=(n // tm,),
        in_specs=[pl.BlockSpec((tm, D_MODEL), lambda i: (i, 0)),
                  _const_spec((nc, D_MODEL, FFN_CHUNK)), _const_spec((nc, D_MODEL, FFN_CHUNK)),
                  _const_spec((nc, FFN_CHUNK, D_MODEL)), _const_spec((3, D_MODEL)), _const_spec((3, D_MODEL))],
        out_specs=pl.BlockSpec((tm, D_MODEL), lambda i: (i, 0)),
        scratch_shapes=[pltpu.VMEM((tm, D_MODEL), F32)],
        compiler_params=_params("parallel"), name="ffn",
    )(x2d, wg, wu, wd, g, bta)


def _tile(n, pref):
    t = min(n, pref)
    assert n % t == 0, (n, pref)
    return t


def kernel(x, mem, positions, rel_bias, mla_w_in, mla_q_norm, mla_w_uq, mla_kv_norm, mla_w_ukv, mla_w_o,
           dsa_w_in, dsa_w_o, xa_w_q, xa_w_kv, xa_w_o, ffn_w_in, ffn_w_down, ln_g, ln_b):
    b, s, d = x.shape
    assert d == D_MODEL and s % LANES == 0
    mlen = mem.shape[1]
    topk = min(TOPK_MAX, s // 4)
    t_attn = _tile(s, 256)
    tm_proj = _tile(s, 256)
    tm_post = _tile(s, 512)
    tm_ffn = _tile(b * s, 512)
    pos3 = positions.reshape(b, s, 1)
    tb = _t5_tiles(rel_bias)
    for i in range(DEPTH):
        j = i // 2
        if i % 2 == 0:
            q, k, v = _mla_proj(x, pos3, mla_w_in[j], mla_q_norm[j], mla_w_uq[j], mla_kv_norm[j],
                                mla_w_ukv[j], tm_proj)
            a = _flash_mla(q, k, v, _tile(s, 512))
            w_o = mla_w_o[j]
        else:
            qa, ka, v2, qi, kw, wf = _dsa_proj(x, dsa_w_in[j], rel_bias, tm_proj)
            a = _dsa_attn(qa, ka, v2, qi, kw, wf, tb, t_attn, topk)
            w_o = dsa_w_o[j]
        kv = _mm(mem.reshape(b * mlen, d), xa_w_kv[i].astype(BF16), _tile(b * mlen, 512)).reshape(b, mlen, 2 * d)
        x = _post(a, x, w_o.astype(BF16), xa_w_q[i].astype(BF16), kv, xa_w_o[i].astype(BF16),
                  ln_g[i], ln_b[i], tm_post)
        x = _ffn(x.reshape(b * s, d), ffn_w_in[i], ffn_w_down[i], ln_g[i], ln_b[i], tm_ffn).reshape(b, s, d)
    return x
```

```python
import functools
import math

import numpy as np
import jax
import jax.numpy as jnp
from jax import lax
from jax.experimental import pallas as pl
from jax.experimental.pallas import tpu as pltpu

D_MODEL = 1024
DEPTH = 2
MLA_HEADS = 16
MLA_Q_RANK = 512
MLA_KV_RANK = 256
MLA_NOPE = 64
MLA_ROPE = 32
MLA_V = 64
DSA_HEADS = 16
DSA_KV_HEADS = 4
DSA_GROUP = DSA_HEADS // DSA_KV_HEADS
DSA_HEAD_DIM = 64
IDX_HEADS = 8
IDX_DIM = 64
TOPK_MAX = 256
XA_HEADS = 4
XA_HEAD_DIM = D_MODEL // XA_HEADS
FFN_HIDDEN = -(-8 * D_MODEL // (3 * 256)) * 256
REL_BUCKETS = 32
REL_MAX_DIST = 128
ROPE_BASE = 10000.0
LN_EPS = 1e-5
RMS_EPS = 1e-6
ALPHA = (2 * DEPTH) ** 0.25

LANES = 128
HALF = LANES // 2
VMEM_LIMIT = 56 * 1024 * 1024

LOG2E = math.log2(math.e)
NEG_BIG = -1e30
F32 = jnp.float32
BF16 = jnp.bfloat16
I32 = jnp.int32
_NT = (((1,), (1,)), ((), ()))


def _dot(a, b):
    return jnp.dot(a, b, preferred_element_type=F32)


def _dot_nt(a, b):
    return lax.dot_general(a, b, _NT, preferred_element_type=F32)


def _layer_norm(y, g, b):
    mu = jnp.mean(y, axis=-1, keepdims=True)
    yc = y - mu
    var = jnp.mean(yc * yc, axis=-1, keepdims=True)
    return yc * lax.rsqrt(var + LN_EPS) * g + b


def _rms_norm(y, g):
    return y * lax.rsqrt(jnp.mean(y * y, axis=-1, keepdims=True) + RMS_EPS) * g


def _params(*sem):
    return pltpu.CompilerParams(dimension_semantics=sem, vmem_limit_bytes=VMEM_LIMIT)


def _const_spec(shape):
    nd = len(shape)
    return pl.BlockSpec(shape, lambda *_: (0,) * nd)


def _take_cols(w, idx):
    idx = np.asarray(idx)
    wp = jnp.concatenate([w, jnp.zeros((w.shape[0], 1), w.dtype)], axis=1)
    return wp[:, np.where(idx < 0, w.shape[1], idx)]


_R = MLA_ROPE // 2
_MLA_W = MLA_Q_RANK + MLA_KV_RANK


def _mla_proj_kernel(x_ref, pos_ref, win_ref, qg_ref, kvg_ref, wuq_ref, wuk_ref, wuv_ref,
                     freq_ref, sgn_ref, q_ref, k_ref, v_ref):
    xb = x_ref[0].astype(BF16)
    h = _dot(xb, win_ref[...])
    cq = _rms_norm(h[:, :MLA_Q_RANK], qg_ref[...]).astype(BF16)
    ckv = _rms_norm(h[:, MLA_Q_RANK:_MLA_W], kvg_ref[...]).astype(BF16)
    kr = h[:, _MLA_W:]
    ang = pos_ref[0].astype(F32) * freq_ref[...]
    cos = jnp.cos(ang)
    sin = jnp.sin(ang) * sgn_ref[...]

    def rope(t):
        return t * cos + pltpu.roll(t, HALF, 1) * sin

    kr = rope(kr)
    q_all = _dot(cq, wuq_ref[...])
    k_all = _dot(ckv, wuk_ref[...])
    v_all = _dot(ckv, wuv_ref[...])
    qscale = (MLA_NOPE + MLA_ROPE) ** -0.5 * LOG2E
    ones_hi = (lax.broadcasted_iota(I32, (1, LANES), 1) >= HALF).astype(F32)
    for hd in range(MLA_HEADS):
        sl = slice(hd * LANES, (hd + 1) * LANES)
        q_ref[0, hd] = (rope(q_all[:, sl]) * qscale).astype(BF16)
        k_ref[0, hd] = (k_all[:, sl] + kr).astype(BF16)
        v_ref[0, hd] = (v_all[:, sl] + ones_hi).astype(BF16)


def _mla_proj(x, pos3, w_in, q_norm, w_uq, kv_norm, w_ukv, tm):
    b, s, _ = x.shape
    hq = MLA_NOPE + MLA_ROPE
    in_idx = list(range(_MLA_W)) + [_MLA_W + j for j in range(_R)] + [-1] * (HALF - _R) \
        + [_MLA_W + _R + j for j in range(_R)] + [-1] * (HALF - _R)
    q_idx, k_idx, v_idx = [], [], []
    n_lo = HALF - _R
    n_hi = MLA_NOPE - n_lo
    tail = [-1] * (HALF - _R - n_hi)
    for hd in range(MLA_HEADS):
        qb = hd * hq
        q_idx += [qb + MLA_NOPE + j for j in range(_R)] + [qb + j for j in range(n_lo)] \
            + [qb + MLA_NOPE + _R + j for j in range(_R)] + [qb + n_lo + j for j in range(n_hi)] + tail
        kb = hd * (MLA_NOPE + MLA_V)
        k_idx += [-1] * _R + [kb + j for j in range(n_lo)] + [-1] * _R + [kb + n_lo + j for j in range(n_hi)] + tail
        v_idx += [kb + MLA_NOPE + j for j in range(MLA_V)] + [-1] * (LANES - MLA_V)
    win = _take_cols(w_in, in_idx).astype(BF16)
    wuq = _take_cols(w_uq, q_idx).astype(BF16)
    wuk = _take_cols(w_ukv, k_idx).astype(BF16)
    wuv = _take_cols(w_ukv, v_idx).astype(BF16)
    inv_freq = ROPE_BASE ** (-jnp.arange(0, MLA_ROPE, 2, dtype=F32) / MLA_ROPE)
    zpad = jnp.zeros((HALF - _R,), F32)
    freq = jnp.concatenate([inv_freq, zpad, inv_freq, zpad])[None]
    sgn = jnp.concatenate([-jnp.ones((_R,), F32), zpad, jnp.ones((_R,), F32), zpad])[None]
    nq = MLA_HEADS * LANES
    grid = (b, s // tm)
    out_shape = (jax.ShapeDtypeStruct((b, MLA_HEADS, s, LANES), BF16),
                 jax.ShapeDtypeStruct((b, MLA_HEADS, s, LANES), BF16),
                 jax.ShapeDtypeStruct((b, MLA_HEADS, s, LANES), BF16))
    return pl.pallas_call(
        _mla_proj_kernel, out_shape=out_shape, grid=grid,
        in_specs=[pl.BlockSpec((1, tm, D_MODEL), lambda bi, i: (bi, i, 0)),
                  pl.BlockSpec((1, tm, 1), lambda bi, i: (bi, i, 0)),
                  _const_spec((D_MODEL, _MLA_W + LANES)),
                  _const_spec((1, MLA_Q_RANK)), _const_spec((1, MLA_KV_RANK)),
                  _const_spec((MLA_Q_RANK, nq)), _const_spec((MLA_KV_RANK, nq)),
                  _const_spec((MLA_KV_RANK, nq)),
                  _const_spec((1, LANES)), _const_spec((1, LANES))],
        out_specs=(pl.BlockSpec((1, MLA_HEADS, tm, LANES), lambda bi, i: (bi, 0, i, 0)),
                   pl.BlockSpec((1, MLA_HEADS, tm, LANES), lambda bi, i: (bi, 0, i, 0)),
                   pl.BlockSpec((1, MLA_HEADS, tm, LANES), lambda bi, i: (bi, 0, i, 0))),
        compiler_params=_params("parallel", "parallel"), name="mla_proj",
    )(x, pos3, win, q_norm[None], kv_norm[None], wuq, wuk, wuv, freq, sgn)


def _flash_kernel(q_ref, k_ref, v_ref, o_ref, m_sc, acc_sc, s_sc, *, t, nh):
    i = pl.program_id(2)
    row = lax.broadcasted_iota(I32, (t, t), 0)
    col = lax.broadcasted_iota(I32, (t, t), 1)
    m_sc[...] = jnp.full(m_sc.shape, NEG_BIG, F32)
    acc_sc[...] = jnp.zeros(acc_sc.shape, F32)

    def scores(j, hh):
        kc = k_ref[0, hh, pl.ds(pl.multiple_of(j * t, t), t), :]
        return _dot_nt(q_ref[0, hh], kc)

    def step(j, slot, diag):
        for hh in range(nh):
            sc = s_sc[slot, hh]
            if diag:
                sc = jnp.where(col <= row, sc, NEG_BIG)
            else:
                s_sc[1 - slot, hh] = scores(j + 1, hh)
            m_prev = m_sc[hh]
            m_new = jnp.maximum(m_prev, jnp.max(sc, axis=-1, keepdims=True))
            a = jnp.exp2(m_prev - m_new)
            p = jnp.concatenate([jnp.exp2(sc[:, c * LANES:(c + 1) * LANES] - m_new)
                                 for c in range(t // LANES)], axis=1).astype(BF16)
            vc = v_ref[0, hh, pl.ds(pl.multiple_of(j * t, t), t), :]
            acc_sc[hh] = a * acc_sc[hh] + _dot(p, vc)
            m_sc[hh] = m_new

    for hh in range(nh):
        s_sc[0, hh] = scores(0, hh)

    def pair(j2, c):
        step(2 * j2, 0, False)
        step(2 * j2 + 1, 1, False)
        return c

    lax.fori_loop(0, i // 2, pair, 0)

    @pl.when(i % 2 == 1)
    def _():
        step(i - 1, 0, False)
        step(i, 1, True)

    @pl.when(i % 2 == 0)
    def _():
        step(i, 0, True)

    lane = lax.broadcasted_iota(I32, (t, LANES), 1)
    for pr in range(nh // 2):
        a0 = acc_sc[2 * pr]
        a1 = acc_sc[2 * pr + 1]
        o0 = a0 / a0[:, HALF:HALF + 1]
        o1 = a1 / a1[:, HALF:HALF + 1]
        o_ref[0, :, pr * LANES:(pr + 1) * LANES] = \
            jnp.where(lane < HALF, o0, pltpu.roll(o1, HALF, 1)).astype(BF16)


MLA_HEADS_PER_STEP = 4


def _flash_mla(q, k, v, t):
    b, h, s, _ = q.shape
    nh = MLA_HEADS_PER_STEP
    return pl.pallas_call(
        functools.partial(_flash_kernel, t=t, nh=nh),
        out_shape=jax.ShapeDtypeStruct((b, s, h * MLA_V), BF16),
        grid=(b, h // nh, s // t),
        in_specs=[pl.BlockSpec((1, nh, t, LANES), lambda bi, p, i: (bi, p, i, 0)),
                  pl.BlockSpec((1, nh, s, LANES), lambda bi, p, i: (bi, p, 0, 0)),
                  pl.BlockSpec((1, nh, s, LANES), lambda bi, p, i: (bi, p, 0, 0))],
        out_specs=pl.BlockSpec((1, t, nh * MLA_V), lambda bi, p, i: (bi, i, p)),
        scratch_shapes=[pltpu.VMEM((nh, t, LANES), F32), pltpu.VMEM((nh, t, LANES), F32),
                        pltpu.VMEM((2, nh, t, t), F32)],
        compiler_params=_params("parallel", "parallel", "arbitrary"), name="mla_flash",
    )(q, k, v)


_DQ0 = 0
_DK0 = DSA_HEADS * LANES
_DV0 = _DK0 + DSA_KV_HEADS * LANES
_DI0 = _DV0 + DSA_KV_HEADS * LANES
_DW0 = _DI0 + IDX_HEADS * LANES
_DN = _DW0 + LANES


def _dsa_proj_kernel(x_ref, w_ref, qhot_ref, krow_ref, wsc_ref, qa_ref, ka_ref, v2_ref, qi_ref, kw_ref, wf_ref):
    xb = x_ref[0].astype(BF16)
    h = _dot(xb, w_ref[...])
    qscale = DSA_HEAD_DIM ** -0.5 * LOG2E
    for hd in range(DSA_HEADS):
        sl = slice(_DQ0 + hd * LANES, _DQ0 + (hd + 1) * LANES)
        qa_ref[0, hd] = (h[:, sl] * qscale + qhot_ref[hd:hd + 1, :]).astype(BF16)
    ones_hi = (lax.broadcasted_iota(I32, (1, LANES), 1) >= HALF).astype(F32)
    for g in range(DSA_KV_HEADS):
        ka_ref[0, g] = (h[:, _DK0 + g * LANES:_DK0 + (g + 1) * LANES] + krow_ref[g:g + 1, :]).astype(BF16)
        v2_ref[0, g] = (h[:, _DV0 + g * LANES:_DV0 + (g + 1) * LANES] + ones_hi).astype(BF16)
    for hd in range(IDX_HEADS):
        qi_ref[0, hd] = (h[:, _DI0 + hd * LANES:_DI0 + (hd + 1) * LANES] * (IDX_DIM ** -0.5)).astype(BF16)
    kw = h[:, _DW0:]
    kw_ref[0] = kw.astype(BF16)
    wf_ref[0] = kw * wsc_ref[...]


def _dsa_proj(x, w_in, rel_bias, tm):
    b, s, _ = x.shape
    o1 = DSA_HEADS * DSA_HEAD_DIM
    o2 = o1 + DSA_KV_HEADS * DSA_HEAD_DIM
    o3 = o2 + DSA_KV_HEADS * DSA_HEAD_DIM
    o4 = o3 + IDX_HEADS * IDX_DIM
    o5 = o4 + IDX_DIM
    idx = []
    for hd in range(DSA_HEADS):
        idx += [hd * DSA_HEAD_DIM + j for j in range(DSA_HEAD_DIM)] + [-1] * HALF
    for g in range(DSA_KV_HEADS):
        idx += [o1 + g * DSA_HEAD_DIM + j for j in range(DSA_HEAD_DIM)] + [-1] * HALF
    for g in range(DSA_KV_HEADS):
        idx += [o2 + g * DSA_HEAD_DIM + j for j in range(DSA_HEAD_DIM)] + [-1] * HALF
    for hd in range(IDX_HEADS):
        idx += [o3 + hd * IDX_DIM + j for j in range(IDX_DIM)] + [-1] * HALF
    idx += [o4 + j for j in range(IDX_DIM)] + [o5 + j for j in range(IDX_HEADS)] + [-1] * (HALF - IDX_HEADS)
    assert len(idx) == _DN
    w = _take_cols(w_in, idx).astype(BF16)
    cfar = rel_bias[REL_BUCKETS - 1] * LOG2E
    chi = cfar.astype(BF16).astype(F32)
    clo = (cfar - chi).astype(BF16).astype(F32)
    qhot = np.zeros((DSA_HEADS, LANES), np.float32)
    for hd in range(DSA_HEADS):
        qhot[hd, HALF + hd % DSA_GROUP] = 1.0
        qhot[hd, HALF + DSA_GROUP + hd % DSA_GROUP] = 1.0
    zpad = jnp.zeros((DSA_KV_HEADS, LANES - HALF - 2 * DSA_GROUP), F32)
    krow = jnp.concatenate([jnp.zeros((DSA_KV_HEADS, HALF), F32), chi.reshape(DSA_KV_HEADS, DSA_GROUP),
                            clo.reshape(DSA_KV_HEADS, DSA_GROUP), zpad], axis=1)
    wsc = np.zeros((1, LANES), np.float32)
    wsc[0, HALF:HALF + IDX_HEADS] = IDX_HEADS ** -0.5
    hspec = lambda n: pl.BlockSpec((1, n, tm, LANES), lambda bi, i: (bi, 0, i, 0))
    tspec = pl.BlockSpec((1, tm, LANES), lambda bi, i: (bi, i, 0))
    out_shape = (jax.ShapeDtypeStruct((b, DSA_HEADS, s, LANES), BF16),
                 jax.ShapeDtypeStruct((b, DSA_KV_HEADS, s, LANES), BF16),
                 jax.ShapeDtypeStruct((b, DSA_KV_HEADS, s, LANES), BF16),
                 jax.ShapeDtypeStruct((b, IDX_HEADS, s, LANES), BF16),
                 jax.ShapeDtypeStruct((b, s, LANES), BF16),
                 jax.ShapeDtypeStruct((b, s, LANES), F32))
    return pl.pallas_call(
        _dsa_proj_kernel, out_shape=out_shape, grid=(b, s // tm),
        in_specs=[pl.BlockSpec((1, tm, D_MODEL), lambda bi, i: (bi, i, 0)),
                  _const_spec((D_MODEL, _DN)), _const_spec((DSA_HEADS, LANES)),
                  _const_spec((DSA_KV_HEADS, LANES)), _const_spec((1, LANES))],
        out_specs=(hspec(DSA_HEADS), hspec(DSA_KV_HEADS), hspec(DSA_KV_HEADS), hspec(IDX_HEADS), tspec, tspec),
        compiler_params=_params("parallel", "parallel"), name="dsa_proj",
    )(x, w, jnp.asarray(qhot), krow, jnp.asarray(wsc))


def _t5_tiles_kernel(rb_ref, o_ref):
    d = pl.program_id(0)
    hd = pl.program_id(1)
    row = lax.broadcasted_iota(I32, (LANES, LANES), 0)
    col = lax.broadcasted_iota(I32, (LANES, LANES), 1)
    n = jnp.maximum(d * LANES + row - col, 0)
    max_exact = REL_BUCKETS // 2
    nf = jnp.maximum(n, 1).astype(F32)
    large = max_exact + (jnp.log(nf / max_exact) / math.log(REL_MAX_DIST / max_exact)
                         * (REL_BUCKETS - max_exact)).astype(I32)
    large = jnp.minimum(large, REL_BUCKETS - 1)
    bucket = jnp.where(n < max_exact, n, large)
    far = rb_ref[REL_BUCKETS - 1, hd]
    acc = jnp.zeros((LANES, LANES), F32)
    for bk in range(REL_BUCKETS - 1):
        acc = jnp.where(bucket == bk, rb_ref[bk, hd] - far, acc)
    o_ref[0, 0] = acc * LOG2E


def _t5_tiles(rel_bias):
    return pl.pallas_call(
        _t5_tiles_kernel,
        out_shape=jax.ShapeDtypeStruct((2, DSA_HEADS, LANES, LANES), F32),
        grid=(2, DSA_HEADS),
        in_specs=[pl.BlockSpec(memory_space=pltpu.SMEM)],
        out_specs=pl.BlockSpec((1, 1, LANES, LANES), lambda d, h: (d, h, 0, 0)),
        compiler_params=_params("parallel", "parallel"), name="t5_tiles",
    )(rel_bias)


_SENT_KEY = int(np.int32(np.uint32(0xFF800000) ^ np.uint32(0x7FFFFFFF)))
_INT_MIN = -(2 ** 31)


def _dsa_attn_kernel(qa_ref, qi_ref, wf_ref, ka_ref, v_ref, kw_ref, tb_ref, o_ref,
                     keyt_sc, madd_sc, s_sc, m_sc, acc_sc, cut_sc, *, t, topk, s_len):
    i = pl.program_id(1)
    nsub = t // LANES
    kidx = lax.broadcasted_iota(I32, (t, t), 0)
    qidx = lax.broadcasted_iota(I32, (t, t), 1)

    qi = qi_ref[0].reshape(IDX_HEADS * t, LANES)
    wt = wf_ref[0].T
    wrows = [wt[HALF + hd:HALF + hd + 1, :] for hd in range(IDX_HEADS)]

    def score_keys_t(j0, w):
        kc = kw_ref[0, pl.ds(pl.multiple_of(j0 * t, t), w * t), :]
        d = _dot_nt(kc, qi)
        sc = wrows[0] * jnp.maximum(d[:, 0:t], 0.0)
        for hd in range(1, IDX_HEADS):
            sc = sc + wrows[hd] * jnp.maximum(d[:, hd * t:(hd + 1) * t], 0.0)
        bits = pltpu.bitcast(sc + 0.0, I32)
        return bits ^ ((bits >> 31) & 0x7FFFFFFF)

    def p1(u, c):
        keys = score_keys_t(2 * u, 2)
        keyt_sc[2 * u] = keys[0:t]
        keyt_sc[2 * u + 1] = keys[t:2 * t]
        return c

    lax.fori_loop(0, i // 2, p1, 0)

    @pl.when(i % 2 == 1)
    def _():
        keys = score_keys_t(i - 1, 2)
        keyt_sc[i - 1] = keys[0:t]
        keyt_sc[i] = jnp.where(kidx <= qidx, keys[t:2 * t], _SENT_KEY)

    @pl.when(i % 2 == 0)
    def _():
        keyt_sc[i] = jnp.where(kidx <= qidx, score_keys_t(i, 1), _SENT_KEY)

    cacc = 32

    def count(pred):
        def body(j, acc):
            c = jnp.where(pred(keyt_sc[j], j), 1.0, 0.0)
            return acc + jnp.sum(c.reshape(t // cacc, cacc, t), axis=0)
        acc = lax.fori_loop(0, i + 1, body, jnp.zeros((cacc, t), F32))
        return jnp.sum(acc, axis=0, keepdims=True)

    kf = float(topk)
    cnt0 = count(lambda kc, j: kc >= 0)
    prefix = jnp.where(cnt0 >= kf, 0, _INT_MIN).astype(I32)

    def bit_body(bi, prefix):
        cand = prefix + lax.shift_left(jnp.int32(1), 30 - bi)
        cnt = count(lambda kc, j: kc >= cand)
        return jnp.where(cnt >= kf, cand, prefix)

    thr = lax.fori_loop(0, 31, bit_body, prefix)
    cnt_gt = count(lambda kc, j: kc > thr)
    cnt_ge = count(lambda kc, j: kc >= thr)
    need = kf - cnt_gt
    excess = jnp.where((cnt_ge > kf) & (thr > _SENT_KEY), 1.0, 0.0)
    cut_sc[...] = jnp.full(cut_sc.shape, s_len, I32)

    @pl.when(jnp.max(excess) > 0.0)
    def _():
        nbits = max(1, (s_len - 1).bit_length())

        def tie_body(bi, c):
            cand = c + lax.shift_left(jnp.int32(1), nbits - 1 - bi)
            cnt = count(lambda kc, j: (kc == thr) & (kidx + j * t < cand))
            return jnp.where(cnt < need, cand, c)

        c = lax.fori_loop(0, nbits, tie_body, jnp.zeros((1, t), I32))
        cut_sc[...] = jnp.broadcast_to(c, cut_sc.shape)

    cut = cut_sc[0:1, :]

    def to_mask(j, causal):
        kc = keyt_sc[j]
        sel = (kc > thr) | ((kc == thr) & (kidx + j * t <= cut))
        if causal:
            sel = sel & (kidx <= qidx)
        madd_sc[j] = jnp.where(sel, 0.0, NEG_BIG).T

    def p2(j, c):
        to_mask(j, False)
        return c

    lax.fori_loop(0, i, p2, 0)
    to_mask(i, True)

    m_sc[...] = jnp.full(m_sc.shape, NEG_BIG, F32)
    acc_sc[...] = jnp.zeros(acc_sc.shape, F32)
    gt = DSA_GROUP * t
    near_delta = {"prev": nsub, "diag": 0}

    def attend(j0, parts):
        w = len(parts)
        start = pl.multiple_of(j0 * t, t)

        def qk(g):
            q = qa_ref[0, g * DSA_GROUP:(g + 1) * DSA_GROUP].reshape(gt, LANES)
            return _dot_nt(q, ka_ref[0, g, pl.ds(start, w * t), :])

        s_sc[0, :, 0:w * t] = qk(0)
        for g in range(DSA_KV_HEADS):
            if g + 1 < DSA_KV_HEADS:
                s_sc[(g + 1) % 2, :, 0:w * t] = qk(g + 1)
            slot = g % 2
            p_rows, a_rows = [], []
            for r in range(DSA_GROUP):
                hd = g * DSA_GROUP + r
                for rs in range(nsub):
                    r0 = r * t + rs * LANES
                    blocks = []
                    for cb in range(w * nsub):
                        part, cs = parts[cb // nsub], cb % nsub
                        blk = s_sc[slot, r0:r0 + LANES, cb * LANES:(cb + 1) * LANES] \
                            + madd_sc[j0 + cb // nsub, rs * LANES:(rs + 1) * LANES, cs * LANES:(cs + 1) * LANES]
                        if part in near_delta and near_delta[part] + rs - cs in (0, 1):
                            blk = blk + tb_ref[near_delta[part] + rs - cs, hd]
                        blocks.append(blk)
                    m_prev = m_sc[g, r0:r0 + LANES]
                    m_blk = blocks[0]
                    for blk in blocks[1:]:
                        m_blk = jnp.maximum(m_blk, blk)
                    m_new = jnp.maximum(m_prev, jnp.max(m_blk, axis=-1, keepdims=True))
                    a_rows.append(jnp.exp2(m_prev - m_new))
                    p_rows.append(jnp.concatenate([jnp.exp2(blk - m_new).astype(BF16) for blk in blocks], axis=1))
                    m_sc[g, r0:r0 + LANES] = m_new
            p = jnp.concatenate(p_rows, axis=0)
            a = jnp.concatenate(a_rows, axis=0)
            vc = v_ref[0, g, pl.ds(start, w * t), :]
            acc_sc[g] = a * acc_sc[g] + _dot(p, vc)

    @pl.when(i == 0)
    def _():
        attend(0, ("diag",))

    @pl.when(i >= 1)
    def _():
        nfar = i - 1

        def far_pair(u, c):
            attend(2 * u, ("far", "far"))
            return c

        lax.fori_loop(0, nfar // 2, far_pair, 0)

        @pl.when(nfar % 2 == 1)
        def _():
            attend(i - 2, ("far",))

        attend(i - 1, ("prev", "diag"))

    lane = lax.broadcasted_iota(I32, (t, LANES), 1)
    for g in range(DSA_KV_HEADS):
        acc = acc_sc[g]
        o = acc / acc[:, HALF:HALF + 1]
        for pr in range(DSA_GROUP // 2):
            lo = o[(2 * pr) * t:(2 * pr + 1) * t]
            hi = o[(2 * pr + 1) * t:(2 * pr + 2) * t]
            cblk = g * (DSA_GROUP // 2) + pr
            o_ref[0, :, cblk * LANES:(cblk + 1) * LANES] = \
                jnp.where(lane < HALF, lo, pltpu.roll(hi, HALF, 1)).astype(BF16)


def _dsa_attn(qa, ka, v, qi, kw, wf, tb, t, topk):
    b, _, s, _ = qa.shape
    nch = s // t
    gt = DSA_GROUP * t
    kern = functools.partial(_dsa_attn_kernel, t=t, topk=topk, s_len=s)
    return pl.pallas_call(
        kern, out_shape=jax.ShapeDtypeStruct((b, s, DSA_HEADS * DSA_HEAD_DIM), BF16),
        grid=(b, nch),
        in_specs=[pl.BlockSpec((1, DSA_HEADS, t, LANES), lambda bi, i: (bi, 0, i, 0)),
                  pl.BlockSpec((1, IDX_HEADS, t, LANES), lambda bi, i: (bi, 0, i, 0)),
                  pl.BlockSpec((1, t, LANES), lambda bi, i: (bi, i, 0)),
                  pl.BlockSpec((1, DSA_KV_HEADS, s, LANES), lambda bi, i: (bi, 0, 0, 0)),
                  pl.BlockSpec((1, DSA_KV_HEADS, s, LANES), lambda bi, i: (bi, 0, 0, 0)),
                  pl.BlockSpec((1, s, LANES), lambda bi, i: (bi, 0, 0)),
                  _const_spec((2, DSA_HEADS, LANES, LANES))],
        out_specs=pl.BlockSpec((1, t, DSA_HEADS * DSA_HEAD_DIM), lambda bi, i: (bi, i, 0)),
        scratch_shapes=[pltpu.VMEM((nch, t, t), I32),
                        pltpu.VMEM((nch, t, t), F32),
                        pltpu.VMEM((2, gt, 2 * t), F32),
                        pltpu.VMEM((DSA_KV_HEADS, gt, LANES), F32),
                        pltpu.VMEM((DSA_KV_HEADS, gt, LANES), F32),
                        pltpu.VMEM((8, t), I32)],
        compiler_params=_params("parallel", "arbitrary"), name="dsa_attn",
    )(qa, qi, wf, ka, v, kw, tb)


def _mm_kernel(a_ref, w_ref, o_ref):
    o_ref[...] = _dot(a_ref[...].astype(BF16), w_ref[...]).astype(BF16)


def _mm(a, w, tm):
    m, k = a.shape
    n = w.shape[1]
    return pl.pallas_call(
        _mm_kernel, out_shape=jax.ShapeDtypeStruct((m, n), BF16), grid=(m // tm,),
        in_specs=[pl.BlockSpec((tm, k), lambda i: (i, 0)), _const_spec((k, n))],
        out_specs=pl.BlockSpec((tm, n), lambda i: (i, 0)),
        compiler_params=_params("parallel"), name="mem_kv",
    )(a, w)


def _post_kernel(a_ref, x_ref, wo_ref, wq_ref, kv_ref, wxo_ref, g_ref, b_ref, o_ref, oc_sc):
    y = _dot(a_ref[0], wo_ref[...])
    x1 = _layer_norm(ALPHA * x_ref[0] + y, g_ref[0:1, :], b_ref[0:1, :])
    q = (_dot(x1.astype(BF16), wq_ref[...]) * (XA_HEAD_DIM ** -0.5 * LOG2E)).astype(BF16)
    for hd in range(XA_HEADS):
        sl = slice(hd * XA_HEAD_DIM, (hd + 1) * XA_HEAD_DIM)
        sc = _dot_nt(q[:, sl], kv_ref[0, :, sl])
        m = jnp.max(sc, axis=-1, keepdims=True)
        p = jnp.exp2(sc - m)
        l = jnp.sum(p, axis=-1, keepdims=True)
        vsl = slice(D_MODEL + hd * XA_HEAD_DIM, D_MODEL + (hd + 1) * XA_HEAD_DIM)
        oc_sc[:, sl] = (_dot(p.astype(BF16), kv_ref[0, :, vsl]) / l).astype(BF16)
    y2 = _dot(oc_sc[...], wxo_ref[...])
    o_ref[0] = _layer_norm(ALPHA * x1 + y2, g_ref[1:2, :], b_ref[1:2, :])


def _post(a, x, w_o, w_q, kv, w_xo, g, bta, tm):
    b, s, _ = x.shape
    mlen = kv.shape[1]
    tok = lambda bi, i: (bi, i, 0)
    return pl.pallas_call(
        _post_kernel, out_shape=jax.ShapeDtypeStruct((b, s, D_MODEL), F32), grid=(b, s // tm),
        in_specs=[pl.BlockSpec((1, tm, D_MODEL), tok), pl.BlockSpec((1, tm, D_MODEL), tok),
                  _const_spec((D_MODEL, D_MODEL)), _const_spec((D_MODEL, D_MODEL)),
                  pl.BlockSpec((1, mlen, 2 * D_MODEL), lambda bi, i: (bi, 0, 0)),
                  _const_spec((D_MODEL, D_MODEL)), _const_spec((3, D_MODEL)), _const_spec((3, D_MODEL))],
        out_specs=pl.BlockSpec((1, tm, D_MODEL), tok),
        scratch_shapes=[pltpu.VMEM((tm, D_MODEL), BF16)],
        compiler_params=_params("parallel", "parallel"), name="post_attn",
    )(a, x, w_o, w_q, kv, w_xo, g, bta)


FFN_CHUNK = 256


def _ffn_kernel(x_ref, wi_ref, wd_ref, g_ref, b_ref, o_ref, act_sc):
    x = x_ref[...]
    xb = x.astype(BF16)
    for c in range(FFN_HIDDEN // FFN_CHUNK):
        sl = slice(c * FFN_CHUNK, (c + 1) * FFN_CHUNK)
        gte = _dot(xb, wi_ref[:, sl])
        up = _dot(xb, wi_ref[:, FFN_HIDDEN + c * FFN_CHUNK:FFN_HIDDEN + (c + 1) * FFN_CHUNK])
        act_sc[:, sl] = (gte * jax.nn.sigmoid(gte) * up).astype(BF16)
    y = _dot(act_sc[...], wd_ref[...])
    o_ref[...] = _layer_norm(ALPHA * x + y, g_ref[2:3, :], b_ref[2:3, :])


def _ffn(x2d, w_in, w_down, g, bta, tm):
    n = x2d.shape[0]
    return pl.pallas_call(
        _ffn_kernel, out_shape=jax.ShapeDtypeStruct((n, D_MODEL), F32), grid=(n // tm,),
        in_specs=[pl.BlockSpec((tm, D_MODEL), lambda i: (i, 0)),
                  _const_spec((D_MODEL, 2 * FFN_HIDDEN)), _const_spec((FFN_HIDDEN, D_MODEL)),
                  _const_spec((3, D_MODEL)), _const_spec((3, D_MODEL))],
        out_specs=pl.BlockSpec((tm, D_MODEL), lambda i: (i, 0)),
        scratch_shapes=[pltpu.VMEM((tm, FFN_HIDDEN), BF16)],
        compiler_params=_params("parallel"), name="ffn",
    )(x2d, w_in.astype(BF16), w_down.astype(BF16), g, bta)


def _tile(n, pref):
    t = min(n, pref)
    assert n % t == 0, (n, pref)
    return t


def kernel(x, mem, positions, rel_bias, mla_w_in, mla_q_norm, mla_w_uq, mla_kv_norm, mla_w_ukv, mla_w_o,
           dsa_w_in, dsa_w_o, xa_w_q, xa_w_kv, xa_w_o, ffn_w_in, ffn_w_down, ln_g, ln_b):
    b, s, d = x.shape
    assert d == D_MODEL and s % LANES == 0
    mlen = mem.shape[1]
    topk = min(TOPK_MAX, s // 4)
    t_attn = _tile(s, 256)
    tm_proj = _tile(s, 256)
    tm_post = _tile(s, 512)
    tm_ffn = _tile(b * s, 512)
    pos3 = positions.reshape(b, s, 1)
    tb = _t5_tiles(rel_bias)
    for i in range(DEPTH):
        j = i // 2
        if i % 2 == 0:
            q, k, v = _mla_proj(x, pos3, mla_w_in[j], mla_q_norm[j], mla_w_uq[j], mla_kv_norm[j],
                                mla_w_ukv[j], tm_proj)
            a = _flash_mla(q, k, v, _tile(s, 512))
            w_o = mla_w_o[j]
        else:
            qa, ka, v2, qi, kw, wf = _dsa_proj(x, dsa_w_in[j], rel_bias, tm_proj)
            a = _dsa_attn(qa, ka, v2, qi, kw, wf, tb, t_attn, topk)
            w_o = dsa_w_o[j]
        kv = _mm(mem.reshape(b * mlen, d), xa_w_kv[i].astype(BF16), _tile(b * mlen, 512)).reshape(b, mlen, 2 * d)
        x = _post(a, x, w_o.astype(BF16), xa_w_q[i].astype(BF16), kv, xa_w_o[i].astype(BF16),
                  ln_g[i], ln_b[i], tm_post)
        x = _ffn(x.reshape(b * s, d), ffn_w_in[i], ffn_w_down[i], ln_g[i], ln_b[i], tm_ffn).reshape(b, s, d)
    return x
```

```python
import functools
import math

import numpy as np
import jax
import jax.numpy as jnp
from jax import lax
from jax.experimental import pallas as pl
from jax.experimental.pallas import tpu as pltpu

D_MODEL = 1024
DEPTH = 2
MLA_HEADS = 16
MLA_Q_RANK = 512
MLA_KV_RANK = 256
MLA_NOPE = 64
MLA_ROPE = 32
MLA_V = 64
DSA_HEADS = 16
DSA_KV_HEADS = 4
DSA_GROUP = DSA_HEADS // DSA_KV_HEADS
DSA_HEAD_DIM = 64
IDX_HEADS = 8
IDX_DIM = 64
TOPK_MAX = 256
XA_HEADS = 4
XA_HEAD_DIM = D_MODEL // XA_HEADS
FFN_HIDDEN = -(-8 * D_MODEL // (3 * 256)) * 256
REL_BUCKETS = 32
REL_MAX_DIST = 128
ROPE_BASE = 10000.0
LN_EPS = 1e-5
RMS_EPS = 1e-6
ALPHA = (2 * DEPTH) ** 0.25

LANES = 128
HALF = LANES // 2
VMEM_LIMIT = 56 * 1024 * 1024

LOG2E = math.log2(math.e)
NEG_BIG = -1e30
F32 = jnp.float32
BF16 = jnp.bfloat16
I32 = jnp.int32
_NT = (((1,), (1,)), ((), ()))


def _dot(a, b):
    return jnp.dot(a, b, preferred_element_type=F32)


def _dot_nt(a, b):
    return lax.dot_general(a, b, _NT, preferred_element_type=F32)


def _layer_norm(y, g, b):
    mu = jnp.mean(y, axis=-1, keepdims=True)
    yc = y - mu
    var = jnp.mean(yc * yc, axis=-1, keepdims=True)
    return yc * lax.rsqrt(var + LN_EPS) * g + b


def _rms_norm(y, g):
    return y * lax.rsqrt(jnp.mean(y * y, axis=-1, keepdims=True) + RMS_EPS) * g


def _params(*sem):
    return pltpu.CompilerParams(dimension_semantics=sem, vmem_limit_bytes=VMEM_LIMIT)


def _const_spec(shape):
    nd = len(shape)
    return pl.BlockSpec(shape, lambda *_: (0,) * nd)


def _take_cols(w, idx):
    idx = np.asarray(idx)
    wp = jnp.concatenate([w, jnp.zeros((w.shape[0], 1), w.dtype)], axis=1)
    return wp[:, np.where(idx < 0, w.shape[1], idx)]


_R = MLA_ROPE // 2
_MLA_W = MLA_Q_RANK + MLA_KV_RANK


def _mla_proj_kernel(x_ref, pos_ref, win_ref, qg_ref, kvg_ref, wuq_ref, wuk_ref, wuv_ref,
                     freq_ref, sgn_ref, q_ref, k_ref, v_ref):
    xb = x_ref[0].astype(BF16)
    h = _dot(xb, win_ref[...])
    cq = _rms_norm(h[:, :MLA_Q_RANK], qg_ref[...]).astype(BF16)
    ckv = _rms_norm(h[:, MLA_Q_RANK:_MLA_W], kvg_ref[...]).astype(BF16)
    kr = h[:, _MLA_W:]
    ang = pos_ref[0].astype(F32) * freq_ref[...]
    cos = jnp.cos(ang)
    sin = jnp.sin(ang) * sgn_ref[...]

    def rope(t):
        return t * cos + pltpu.roll(t, HALF, 1) * sin

    kr = rope(kr)
    q_all = _dot(cq, wuq_ref[...])
    k_all = _dot(ckv, wuk_ref[...])
    v_all = _dot(ckv, wuv_ref[...])
    qscale = (MLA_NOPE + MLA_ROPE) ** -0.5 * LOG2E
    ones_hi = (lax.broadcasted_iota(I32, (1, LANES), 1) >= HALF).astype(F32)
    for hd in range(MLA_HEADS):
        sl = slice(hd * LANES, (hd + 1) * LANES)
        q_ref[0, hd] = (rope(q_all[:, sl]) * qscale).astype(BF16)
        k_ref[0, hd] = (k_all[:, sl] + kr).astype(BF16)
        v_ref[0, hd] = (v_all[:, sl] + ones_hi).astype(BF16)


def _mla_proj(x, pos3, w_in, q_norm, w_uq, kv_norm, w_ukv, tm):
    b, s, _ = x.shape
    hq = MLA_NOPE + MLA_ROPE
    in_idx = list(range(_MLA_W)) + [_MLA_W + j for j in range(_R)] + [-1] * (HALF - _R) \
        + [_MLA_W + _R + j for j in range(_R)] + [-1] * (HALF - _R)
    q_idx, k_idx, v_idx = [], [], []
    n_lo = HALF - _R
    n_hi = MLA_NOPE - n_lo
    tail = [-1] * (HALF - _R - n_hi)
    for hd in range(MLA_HEADS):
        qb = hd * hq
        q_idx += [qb + MLA_NOPE + j for j in range(_R)] + [qb + j for j in range(n_lo)] \
            + [qb + MLA_NOPE + _R + j for j in range(_R)] + [qb + n_lo + j for j in range(n_hi)] + tail
        kb = hd * (MLA_NOPE + MLA_V)
        k_idx += [-1] * _R + [kb + j for j in range(n_lo)] + [-1] * _R + [kb + n_lo + j for j in range(n_hi)] + tail
        v_idx += [kb + MLA_NOPE + j for j in range(MLA_V)] + [-1] * (LANES - MLA_V)
    win = _take_cols(w_in, in_idx).astype(BF16)
    wuq = _take_cols(w_uq, q_idx).astype(BF16)
    wuk = _take_cols(w_ukv, k_idx).astype(BF16)
    wuv = _take_cols(w_ukv, v_idx).astype(BF16)
    inv_freq = ROPE_BASE ** (-jnp.arange(0, MLA_ROPE, 2, dtype=F32) / MLA_ROPE)
    zpad = jnp.zeros((HALF - _R,), F32)
    freq = jnp.concatenate([inv_freq, zpad, inv_freq, zpad])[None]
    sgn = jnp.concatenate([-jnp.ones((_R,), F32), zpad, jnp.ones((_R,), F32), zpad])[None]
    nq = MLA_HEADS * LANES
    grid = (b, s // tm)
    out_shape = (jax.ShapeDtypeStruct((b, MLA_HEADS, s, LANES), BF16),
                 jax.ShapeDtypeStruct((b, MLA_HEADS, s, LANES), BF16),
                 jax.ShapeDtypeStruct((b, MLA_HEADS, s, LANES), BF16))
    return pl.pallas_call(
        _mla_proj_kernel, out_shape=out_shape, grid=grid,
        in_specs=[pl.BlockSpec((1, tm, D_MODEL), lambda bi, i: (bi, i, 0)),
                  pl.BlockSpec((1, tm, 1), lambda bi, i: (bi, i, 0)),
                  _const_spec((D_MODEL, _MLA_W + LANES)),
                  _const_spec((1, MLA_Q_RANK)), _const_spec((1, MLA_KV_RANK)),
                  _const_spec((MLA_Q_RANK, nq)), _const_spec((MLA_KV_RANK, nq)),
                  _const_spec((MLA_KV_RANK, nq)),
                  _const_spec((1, LANES)), _const_spec((1, LANES))],
        out_specs=(pl.BlockSpec((1, MLA_HEADS, tm, LANES), lambda bi, i: (bi, 0, i, 0)),
                   pl.BlockSpec((1, MLA_HEADS, tm, LANES), lambda bi, i: (bi, 0, i, 0)),
                   pl.BlockSpec((1, MLA_HEADS, tm, LANES), lambda bi, i: (bi, 0, i, 0))),
        compiler_params=_params("parallel", "parallel"), name="mla_proj",
    )(x, pos3, win, q_norm[None], kv_norm[None], wuq, wuk, wuv, freq, sgn)


def _flash_kernel(q_ref, k_ref, v_ref, o_ref, m_sc, acc_sc, s_sc, *, t, nh):
    i = pl.program_id(2)
    row = lax.broadcasted_iota(I32, (t, t), 0)
    col = lax.broadcasted_iota(I32, (t, t), 1)
    m_sc[...] = jnp.full(m_sc.shape, NEG_BIG, F32)
    acc_sc[...] = jnp.zeros(acc_sc.shape, F32)

    def scores(j, hh):
        kc = k_ref[0, hh, pl.ds(pl.multiple_of(j * t, t), t), :]
        return _dot_nt(q_ref[0, hh], kc)

    def step(j, slot, diag):
        for hh in range(nh):
            sc = s_sc[slot, hh]
            if diag:
                sc = jnp.where(col <= row, sc, NEG_BIG)
            else:
                s_sc[1 - slot, hh] = scores(j + 1, hh)
            m_prev = m_sc[hh]
            m_new = jnp.maximum(m_prev, jnp.max(sc, axis=-1, keepdims=True))
            a = jnp.exp2(m_prev - m_new)
            p = jnp.concatenate([jnp.exp2(sc[:, c * LANES:(c + 1) * LANES] - m_new)
                                 for c in range(t // LANES)], axis=1).astype(BF16)
            vc = v_ref[0, hh, pl.ds(pl.multiple_of(j * t, t), t), :]
            acc_sc[hh] = a * acc_sc[hh] + _dot(p, vc)
            m_sc[hh] = m_new

    for hh in range(nh):
        s_sc[0, hh] = scores(0, hh)

    def pair(j2, c):
        step(2 * j2, 0, False)
        step(2 * j2 + 1, 1, False)
        return c

    lax.fori_loop(0, i // 2, pair, 0)

    @pl.when(i % 2 == 1)
    def _():
        step(i - 1, 0, False)
        step(i, 1, True)

    @pl.when(i % 2 == 0)
    def _():
        step(i, 0, True)

    lane = lax.broadcasted_iota(I32, (t, LANES), 1)
    for pr in range(nh // 2):
        a0 = acc_sc[2 * pr]
        a1 = acc_sc[2 * pr + 1]
        o0 = a0 / a0[:, HALF:HALF + 1]
        o1 = a1 / a1[:, HALF:HALF + 1]
        o_ref[0, :, pr * LANES:(pr + 1) * LANES] = \
            jnp.where(lane < HALF, o0, pltpu.roll(o1, HALF, 1)).astype(BF16)


MLA_HEADS_PER_STEP = 4


def _flash_mla(q, k, v, t):
    b, h, s, _ = q.shape
    nh = MLA_HEADS_PER_STEP
    return pl.pallas_call(
        functools.partial(_flash_kernel, t=t, nh=nh),
        out_shape=jax.ShapeDtypeStruct((b, s, h * MLA_V), BF16),
        grid=(b, h // nh, s // t),
        in_specs=[pl.BlockSpec((1, nh, t, LANES), lambda bi, p, i: (bi, p, i, 0)),
                  pl.BlockSpec((1, nh, s, LANES), lambda bi, p, i: (bi, p, 0, 0)),
                  pl.BlockSpec((1, nh, s, LANES), lambda bi, p, i: (bi, p, 0, 0))],
        out_specs=pl.BlockSpec((1, t, nh * MLA_V), lambda bi, p, i: (bi, i, p)),
        scratch_shapes=[pltpu.VMEM((nh, t, LANES), F32), pltpu.VMEM((nh, t, LANES), F32),
                        pltpu.VMEM((2, nh, t, t), F32)],
        compiler_params=_params("parallel", "parallel", "arbitrary"), name="mla_flash",
    )(q, k, v)


_DQ0 = 0
_DK0 = DSA_HEADS * LANES
_DV0 = _DK0 + DSA_KV_HEADS * LANES
_DI0 = _DV0 + DSA_KV_HEADS * LANES
_DW0 = _DI0 + IDX_HEADS * LANES
_DN = _DW0 + LANES


def _dsa_proj_kernel(x_ref, w_ref, qhot_ref, krow_ref, wsc_ref, qa_ref, ka_ref, v2_ref, qi_ref, kw_ref, wf_ref):
    xb = x_ref[0].astype(BF16)
    h = _dot(xb, w_ref[...])
    qscale = DSA_HEAD_DIM ** -0.5 * LOG2E
    for hd in range(DSA_HEADS):
        sl = slice(_DQ0 + hd * LANES, _DQ0 + (hd + 1) * LANES)
        qa_ref[0, hd] = (h[:, sl] * qscale + qhot_ref[hd:hd + 1, :]).astype(BF16)
    ones_hi = (lax.broadcasted_iota(I32, (1, LANES), 1) >= HALF).astype(F32)
    for g in range(DSA_KV_HEADS):
        ka_ref[0, g] = (h[:, _DK0 + g * LANES:_DK0 + (g + 1) * LANES] + krow_ref[g:g + 1, :]).astype(BF16)
        v2_ref[0, g] = (h[:, _DV0 + g * LANES:_DV0 + (g + 1) * LANES] + ones_hi).astype(BF16)
    for hd in range(IDX_HEADS):
        qi_ref[0, hd] = (h[:, _DI0 + hd * LANES:_DI0 + (hd + 1) * LANES] * (IDX_DIM ** -0.5)).astype(BF16)
    kw = h[:, _DW0:]
    kw_ref[0] = kw.astype(BF16)
    wf_ref[0] = kw * wsc_ref[...]


def _dsa_proj(x, w_in, rel_bias, tm):
    b, s, _ = x.shape
    o1 = DSA_HEADS * DSA_HEAD_DIM
    o2 = o1 + DSA_KV_HEADS * DSA_HEAD_DIM
    o3 = o2 + DSA_KV_HEADS * DSA_HEAD_DIM
    o4 = o3 + IDX_HEADS * IDX_DIM
    o5 = o4 + IDX_DIM
    idx = []
    for hd in range(DSA_HEADS):
        idx += [hd * DSA_HEAD_DIM + j for j in range(DSA_HEAD_DIM)] + [-1] * HALF
    for g in range(DSA_KV_HEADS):
        idx += [o1 + g * DSA_HEAD_DIM + j for j in range(DSA_HEAD_DIM)] + [-1] * HALF
    for g in range(DSA_KV_HEADS):
        idx += [o2 + g * DSA_HEAD_DIM + j for j in range(DSA_HEAD_DIM)] + [-1] * HALF
    for hd in range(IDX_HEADS):
        idx += [o3 + hd * IDX_DIM + j for j in range(IDX_DIM)] + [-1] * HALF
    idx += [o4 + j for j in range(IDX_DIM)] + [o5 + j for j in range(IDX_HEADS)] + [-1] * (HALF - IDX_HEADS)
    assert len(idx) == _DN
    w = _take_cols(w_in, idx).astype(BF16)
    cfar = rel_bias[REL_BUCKETS - 1] * LOG2E
    chi = cfar.astype(BF16).astype(F32)
    clo = (cfar - chi).astype(BF16).astype(F32)
    qhot = np.zeros((DSA_HEADS, LANES), np.float32)
    for hd in range(DSA_HEADS):
        qhot[hd, HALF + hd % DSA_GROUP] = 1.0
        qhot[hd, HALF + DSA_GROUP + hd % DSA_GROUP] = 1.0
    zpad = jnp.zeros((DSA_KV_HEADS, LANES - HALF - 2 * DSA_GROUP), F32)
    krow = jnp.concatenate([jnp.zeros((DSA_KV_HEADS, HALF), F32), chi.reshape(DSA_KV_HEADS, DSA_GROUP),
                            clo.reshape(DSA_KV_HEADS, DSA_GROUP), zpad], axis=1)
    wsc = np.zeros((1, LANES), np.float32)
    wsc[0, HALF:HALF + IDX_HEADS] = IDX_HEADS ** -0.5
    hspec = lambda n: pl.BlockSpec((1, n, tm, LANES), lambda bi, i: (bi, 0, i, 0))
    tspec = pl.BlockSpec((1, tm, LANES), lambda bi, i: (bi, i, 0))
    out_shape = (jax.ShapeDtypeStruct((b, DSA_HEADS, s, LANES), BF16),
                 jax.ShapeDtypeStruct((b, DSA_KV_HEADS, s, LANES), BF16),
                 jax.ShapeDtypeStruct((b, DSA_KV_HEADS, s, LANES), BF16),
                 jax.ShapeDtypeStruct((b, IDX_HEADS, s, LANES), BF16),
                 jax.ShapeDtypeStruct((b, s, LANES), BF16),
                 jax.ShapeDtypeStruct((b, s, LANES), F32))
    return pl.pallas_call(
        _dsa_proj_kernel, out_shape=out_shape, grid=(b, s // tm),
        in_specs=[pl.BlockSpec((1, tm, D_MODEL), lambda bi, i: (bi, i, 0)),
                  _const_spec((D_MODEL, _DN)), _const_spec((DSA_HEADS, LANES)),
                  _const_spec((DSA_KV_HEADS, LANES)), _const_spec((1, LANES))],
        out_specs=(hspec(DSA_HEADS), hspec(DSA_KV_HEADS), hspec(DSA_KV_HEADS), hspec(IDX_HEADS), tspec, tspec),
        compiler_params=_params("parallel", "parallel"), name="dsa_proj",
    )(x, w, jnp.asarray(qhot), krow, jnp.asarray(wsc))


def _t5_tiles_kernel(rb_ref, o_ref):
    d = pl.program_id(0)
    hd = pl.program_id(1)
    row = lax.broadcasted_iota(I32, (LANES, LANES), 0)
    col = lax.broadcasted_iota(I32, (LANES, LANES), 1)
    n = jnp.maximum(d * LANES + row - col, 0)
    max_exact = REL_BUCKETS // 2
    nf = jnp.maximum(n, 1).astype(F32)
    large = max_exact + (jnp.log(nf / max_exact) / math.log(REL_MAX_DIST / max_exact)
                         * (REL_BUCKETS - max_exact)).astype(I32)
    large = jnp.minimum(large, REL_BUCKETS - 1)
    bucket = jnp.where(n < max_exact, n, large)
    far = rb_ref[REL_BUCKETS - 1, hd]
    acc = jnp.zeros((LANES, LANES), F32)
    for bk in range(REL_BUCKETS - 1):
        acc = jnp.where(bucket == bk, rb_ref[bk, hd] - far, acc)
    o_ref[0, 0] = acc * LOG2E


def _t5_tiles(rel_bias):
    return pl.pallas_call(
        _t5_tiles_kernel,
        out_shape=jax.ShapeDtypeStruct((2, DSA_HEADS, LANES, LANES), F32),
        grid=(2, DSA_HEADS),
        in_specs=[pl.BlockSpec(memory_space=pltpu.SMEM)],
        out_specs=pl.BlockSpec((1, 1, LANES, LANES), lambda d, h: (d, h, 0, 0)),
        compiler_params=_params("parallel", "parallel"), name="t5_tiles",
    )(rel_bias)


_SENT_KEY = int(np.int32(np.uint32(0xFF800000) ^ np.uint32(0x7FFFFFFF)))
_INT_MIN = -(2 ** 31)


def _dsa_attn_kernel(qa_ref, qi_ref, wf_ref, ka_ref, v_ref, kw_ref, tb_ref, o_ref,
                     keyt_sc, madd_sc, s_sc, m_sc, acc_sc, cut_sc, *, t, topk, s_len):
    i = pl.program_id(1)
    nsub = t // LANES
    kidx = lax.broadcasted_iota(I32, (t, t), 0)
    qidx = lax.broadcasted_iota(I32, (t, t), 1)

    qi = qi_ref[0].reshape(IDX_HEADS * t, LANES)
    wt = wf_ref[0].T
    wrows = [wt[HALF + hd:HALF + hd + 1, :] for hd in range(IDX_HEADS)]

    def score_keys_t(j0, w):
        kc = kw_ref[0, pl.ds(pl.multiple_of(j0 * t, t), w * t), :]
        d = _dot_nt(kc, qi)
        sc = wrows[0] * jnp.maximum(d[:, 0:t], 0.0)
        for hd in range(1, IDX_HEADS):
            sc = sc + wrows[hd] * jnp.maximum(d[:, hd * t:(hd + 1) * t], 0.0)
        bits = pltpu.bitcast(sc + 0.0, I32)
        return bits ^ ((bits >> 31) & 0x7FFFFFFF)

    def p1(u, c):
        keys = score_keys_t(2 * u, 2)
        keyt_sc[2 * u] = keys[0:t]
        keyt_sc[2 * u + 1] = keys[t:2 * t]
        return c

    lax.fori_loop(0, i // 2, p1, 0)

    @pl.when(i % 2 == 1)
    def _():
        keys = score_keys_t(i - 1, 2)
        keyt_sc[i - 1] = keys[0:t]
        keyt_sc[i] = jnp.where(kidx <= qidx, keys[t:2 * t], _SENT_KEY)

    @pl.when(i % 2 == 0)
    def _():
        keyt_sc[i] = jnp.where(kidx <= qidx, score_keys_t(i, 1), _SENT_KEY)

    cacc = 32

    def count(pred):
        def body(j, acc):
            c = jnp.where(pred(keyt_sc[j], j), 1.0, 0.0)
            return acc + jnp.sum(c.reshape(t // cacc, cacc, t), axis=0)
        acc = lax.fori_loop(0, i + 1, body, jnp.zeros((cacc, t), F32))
        return jnp.sum(acc, axis=0, keepdims=True)

    kf = float(topk)
    cnt0 = count(lambda kc, j: kc >= 0)
    nonneg = cnt0 >= kf
    prefix0 = jnp.where(nonneg, 0, _INT_MIN).astype(I32)
    cge0 = jnp.where(nonneg, cnt0, ((i + 1) * t).astype(F32))

    def bit_body(bi, carry):
        prefix, cge = carry
        cand = prefix + lax.shift_left(jnp.int32(1), 30 - bi)
        cnt = count(lambda kc, j: kc >= cand)
        take = cnt >= kf
        return jnp.where(take, cand, prefix), jnp.where(take, cnt, cge)

    thr, cnt_ge = lax.fori_loop(0, 31, bit_body, (prefix0, cge0))
    excess = jnp.where((cnt_ge > kf) & (thr > _SENT_KEY), 1.0, 0.0)
    cut_sc[...] = jnp.full(cut_sc.shape, s_len, I32)

    @pl.when(jnp.max(excess) > 0.0)
    def _():
        nbits = max(1, (s_len - 1).bit_length())
        need = kf - count(lambda kc, j: kc > thr)

        def tie_body(bi, c):
            cand = c + lax.shift_left(jnp.int32(1), nbits - 1 - bi)
            cnt = count(lambda kc, j: (kc == thr) & (kidx + j * t < cand))
            return jnp.where(cnt < need, cand, c)

        c = lax.fori_loop(0, nbits, tie_body, jnp.zeros((1, t), I32))
        cut_sc[...] = jnp.broadcast_to(c, cut_sc.shape)

    cut = cut_sc[0:1, :]

    def to_mask(j, causal):
        kc = keyt_sc[j]
        sel = (kc > thr) | ((kc == thr) & (kidx + j * t <= cut))
        if causal:
            sel = sel & (kidx <= qidx)
        return jnp.where(sel, 0.0, NEG_BIG).T

    m_sc[...] = jnp.full(m_sc.shape, NEG_BIG, F32)
    acc_sc[...] = jnp.zeros(acc_sc.shape, F32)
    gt = DSA_GROUP * t
    near_delta = {"prev": nsub, "diag": 0}

    def attend(j0, parts):
        w = len(parts)
        start = pl.multiple_of(j0 * t, t)

        def qk(g):
            q = qa_ref[0, g * DSA_GROUP:(g + 1) * DSA_GROUP].reshape(gt, LANES)
            return _dot_nt(q, ka_ref[0, g, pl.ds(start, w * t), :])

        s_sc[0, :, 0:w * t] = qk(0)
        for pi, part in enumerate(parts):
            madd_sc[pi] = to_mask(j0 + pi, part == "diag")
        for g in range(DSA_KV_HEADS):
            if g + 1 < DSA_KV_HEADS:
                s_sc[(g + 1) % 2, :, 0:w * t] = qk(g + 1)
            slot = g % 2
            p_rows, a_rows = [], []
            for r in range(DSA_GROUP):
                hd = g * DSA_GROUP + r
                for rs in range(nsub):
                    r0 = r * t + rs * LANES
                    blocks = []
                    for cb in range(w * nsub):
                        part, cs = parts[cb // nsub], cb % nsub
                        blk = s_sc[slot, r0:r0 + LANES, cb * LANES:(cb + 1) * LANES] \
                            + madd_sc[cb // nsub, rs * LANES:(rs + 1) * LANES, cs * LANES:(cs + 1) * LANES]
                        if part in near_delta and near_delta[part] + rs - cs in (0, 1):
                            blk = blk + tb_ref[near_delta[part] + rs - cs, hd]
                        blocks.append(blk)
                    m_prev = m_sc[g, r0:r0 + LANES]
                    m_blk = blocks[0]
                    for blk in blocks[1:]:
                        m_blk = jnp.maximum(m_blk, blk)
                    m_new = jnp.maximum(m_prev, jnp.max(m_blk, axis=-1, keepdims=True))
                    a_rows.append(jnp.exp2(m_prev - m_new))
                    p_rows.append(jnp.concatenate([jnp.exp2(blk - m_new).astype(BF16) for blk in blocks], axis=1))
                    m_sc[g, r0:r0 + LANES] = m_new
            p = jnp.concatenate(p_rows, axis=0)
            a = jnp.concatenate(a_rows, axis=0)
            vc = v_ref[0, g, pl.ds(start, w * t), :]
            acc_sc[g] = a * acc_sc[g] + _dot(p, vc)

    @pl.when(i == 0)
    def _():
        attend(0, ("diag",))

    @pl.when(i >= 1)
    def _():
        nfar = i - 1

        def far_pair(u, c):
            attend(2 * u, ("far", "far"))
            return c

        lax.fori_loop(0, nfar // 2, far_pair, 0)

        @pl.when(nfar % 2 == 1)
        def _():
            attend(i - 2, ("far",))

        attend(i - 1, ("prev", "diag"))

    lane = lax.broadcasted_iota(I32, (t, LANES), 1)
    for g in range(DSA_KV_HEADS):
        acc = acc_sc[g]
        o = acc / acc[:, HALF:HALF + 1]
        for pr in range(DSA_GROUP // 2):
            lo = o[(2 * pr) * t:(2 * pr + 1) * t]
            hi = o[(2 * pr + 1) * t:(2 * pr + 2) * t]
            cblk = g * (DSA_GROUP // 2) + pr
            o_ref[0, :, cblk * LANES:(cblk + 1) * LANES] = \
                jnp.where(lane < HALF, lo, pltpu.roll(hi, HALF, 1)).astype(BF16)


def _dsa_attn(qa, ka, v, qi, kw, wf, tb, t, topk):
    b, _, s, _ = qa.shape
    nch = s // t
    gt = DSA_GROUP * t
    kern = functools.partial(_dsa_attn_kernel, t=t, topk=topk, s_len=s)
    return pl.pallas_call(
        kern, out_shape=jax.ShapeDtypeStruct((b, s, DSA_HEADS * DSA_HEAD_DIM), BF16),
        grid=(b, nch),
        in_specs=[pl.BlockSpec((1, DSA_HEADS, t, LANES), lambda bi, i: (bi, 0, i, 0)),
                  pl.BlockSpec((1, IDX_HEADS, t, LANES), lambda bi, i: (bi, 0, i, 0)),
                  pl.BlockSpec((1, t, LANES), lambda bi, i: (bi, i, 0)),
                  pl.BlockSpec((1, DSA_KV_HEADS, s, LANES), lambda bi, i: (bi, 0, 0, 0)),
                  pl.BlockSpec((1, DSA_KV_HEADS, s, LANES), lambda bi, i: (bi, 0, 0, 0)),
                  pl.BlockSpec((1, s, LANES), lambda bi, i: (bi, 0, 0)),
                  _const_spec((2, DSA_HEADS, LANES, LANES))],
        out_specs=pl.BlockSpec((1, t, DSA_HEADS * DSA_HEAD_DIM), lambda bi, i: (bi, i, 0)),
        scratch_shapes=[pltpu.VMEM((nch, t, t), I32),
                        pltpu.VMEM((2, t, t), F32),
                        pltpu.VMEM((2, gt, 2 * t), F32),
                        pltpu.VMEM((DSA_KV_HEADS, gt, LANES), F32),
                        pltpu.VMEM((DSA_KV_HEADS, gt, LANES), F32),
                        pltpu.VMEM((8, t), I32)],
        compiler_params=_params("parallel", "arbitrary"), name="dsa_attn",
    )(qa, qi, wf, ka, v, kw, tb)


def _mm_kernel(a_ref, w_ref, o_ref):
    o_ref[...] = _dot(a_ref[...].astype(BF16), w_ref[...]).astype(BF16)


def _mm(a, w, tm):
    m, k = a.shape
    n = w.shape[1]
    return pl.pallas_call(
        _mm_kernel, out_shape=jax.ShapeDtypeStruct((m, n), BF16), grid=(m // tm,),
        in_specs=[pl.BlockSpec((tm, k), lambda i: (i, 0)), _const_spec((k, n))],
        out_specs=pl.BlockSpec((tm, n), lambda i: (i, 0)),
        compiler_params=_params("parallel"), name="mem_kv",
    )(a, w)


def _post_kernel(a_ref, x_ref, wo_ref, wq_ref, kv_ref, wxo_ref, g_ref, b_ref, o_ref, oc_sc):
    y = _dot(a_ref[0], wo_ref[...])
    x1 = _layer_norm(ALPHA * x_ref[0] + y, g_ref[0:1, :], b_ref[0:1, :])
    q = (_dot(x1.astype(BF16), wq_ref[...]) * (XA_HEAD_DIM ** -0.5 * LOG2E)).astype(BF16)
    for hd in range(XA_HEADS):
        sl = slice(hd * XA_HEAD_DIM, (hd + 1) * XA_HEAD_DIM)
        sc = _dot_nt(q[:, sl], kv_ref[0, :, sl])
        m = jnp.max(sc, axis=-1, keepdims=True)
        p = jnp.exp2(sc - m)
        l = jnp.sum(p, axis=-1, keepdims=True)
        vsl = slice(D_MODEL + hd * XA_HEAD_DIM, D_MODEL + (hd + 1) * XA_HEAD_DIM)
        oc_sc[:, sl] = (_dot(p.astype(BF16), kv_ref[0, :, vsl]) / l).astype(BF16)
    y2 = _dot(oc_sc[...], wxo_ref[...])
    o_ref[0] = _layer_norm(ALPHA * x1 + y2, g_ref[1:2, :], b_ref[1:2, :])


def _post(a, x, w_o, w_q, kv, w_xo, g, bta, tm):
    b, s, _ = x.shape
    mlen = kv.shape[1]
    tok = lambda bi, i: (bi, i, 0)
    return pl.pallas_call(
        _post_kernel, out_shape=jax.ShapeDtypeStruct((b, s, D_MODEL), F32), grid=(b, s // tm),
        in_specs=[pl.BlockSpec((1, tm, D_MODEL), tok), pl.BlockSpec((1, tm, D_MODEL), tok),
                  _const_spec((D_MODEL, D_MODEL)), _const_spec((D_MODEL, D_MODEL)),
                  pl.BlockSpec((1, mlen, 2 * D_MODEL), lambda bi, i: (bi, 0, 0)),
                  _const_spec((D_MODEL, D_MODEL)), _const_spec((3, D_MODEL)), _const_spec((3, D_MODEL))],
        out_specs=pl.BlockSpec((1, tm, D_MODEL), tok),
        scratch_shapes=[pltpu.VMEM((tm, D_MODEL), BF16)],
        compiler_params=_params("parallel", "parallel"), name="post_attn",
    )(a, x, w_o, w_q, kv, w_xo, g, bta)


FFN_CHUNK = 256


def _ffn_kernel(x_ref, wi_ref, wd_ref, g_ref, b_ref, o_ref, act_sc):
    x = x_ref[...]
    xb = x.astype(BF16)
    for c in range(FFN_HIDDEN // FFN_CHUNK):
        sl = slice(c * FFN_CHUNK, (c + 1) * FFN_CHUNK)
        gte = _dot(xb, wi_ref[:, sl])
        up = _dot(xb, wi_ref[:, FFN_HIDDEN + c * FFN_CHUNK:FFN_HIDDEN + (c + 1) * FFN_CHUNK])
        act_sc[:, sl] = (gte * jax.nn.sigmoid(gte) * up).astype(BF16)
    y = _dot(act_sc[...], wd_ref[...])
    o_ref[...] = _layer_norm(ALPHA * x + y, g_ref[2:3, :], b_ref[2:3, :])


def _ffn(x2d, w_in, w_down, g, bta, tm):
    n = x2d.shape[0]
    return pl.pallas_call(
        _ffn_kernel, out_shape=jax.ShapeDtypeStruct((n, D_MODEL), F32), grid=(n // tm,),
        in_specs=[pl.BlockSpec((tm, D_MODEL), lambda i: (i, 0)),
                  _const_spec((D_MODEL, 2 * FFN_HIDDEN)), _const_spec((FFN_HIDDEN, D_MODEL)),
                  _const_spec((3, D_MODEL)), _const_spec((3, D_MODEL))],
        out_specs=pl.BlockSpec((tm, D_MODEL), lambda i: (i, 0)),
        scratch_shapes=[pltpu.VMEM((tm, FFN_HIDDEN), BF16)],
        compiler_params=_params("parallel"), name="ffn",
    )(x2d, w_in.astype(BF16), w_down.astype(BF16), g, bta)


def _tile(n, pref):
    t = min(n, pref)
    assert n % t == 0, (n, pref)
    return t


def kernel(x, mem, positions, rel_bias, mla_w_in, mla_q_norm, mla_w_uq, mla_kv_norm, mla_w_ukv, mla_w_o,
           dsa_w_in, dsa_w_o, xa_w_q, xa_w_kv, xa_w_o, ffn_w_in, ffn_w_down, ln_g, ln_b):
    b, s, d = x.shape
    assert d == D_MODEL and s % LANES == 0
    mlen = mem.shape[1]
    topk = min(TOPK_MAX, s // 4)
    t_attn = _tile(s, 256)
    tm_proj = _tile(s, 256)
    tm_post = _tile(s, 512)
    tm_ffn = _tile(b * s, 512)
    pos3 = positions.reshape(b, s, 1)
    tb = _t5_tiles(rel_bias)
    for i in range(DEPTH):
        j = i // 2
        if i % 2 == 0:
            q, k, v = _mla_proj(x, pos3, mla_w_in[j], mla_q_norm[j], mla_w_uq[j], mla_kv_norm[j],
                                mla_w_ukv[j], tm_proj)
            a = _flash_mla(q, k, v, _tile(s, 512))
            w_o = mla_w_o[j]
        else:
            qa, ka, v2, qi, kw, wf = _dsa_proj(x, dsa_w_in[j], rel_bias, tm_proj)
            a = _dsa_attn(qa, ka, v2, qi, kw, wf, tb, t_attn, topk)
            w_o = dsa_w_o[j]
        kv = _mm(mem.reshape(b * mlen, d), xa_w_kv[i].astype(BF16), _tile(b * mlen, 512)).reshape(b, mlen, 2 * d)
        x = _post(a, x, w_o.astype(BF16), xa_w_q[i].astype(BF16), kv, xa_w_o[i].astype(BF16),
                  ln_g[i], ln_b[i], tm_post)
        x = _ffn(x.reshape(b * s, d), ffn_w_in[i], ffn_w_down[i], ln_g[i], ln_b[i], tm_ffn).reshape(b, s, d)
    return x
```

```python
import functools
import math

import numpy as np
import jax
import jax.numpy as jnp
from jax import lax
from jax.experimental import pallas as pl
from jax.experimental.pallas import tpu as pltpu

D_MODEL = 1024
DEPTH = 2
MLA_HEADS = 16
MLA_Q_RANK = 512
MLA_KV_RANK = 256
MLA_NOPE = 64
MLA_ROPE = 32
MLA_V = 64
DSA_HEADS = 16
DSA_KV_HEADS = 4
DSA_GROUP = DSA_HEADS // DSA_KV_HEADS
DSA_HEAD_DIM = 64
IDX_HEADS = 8
IDX_DIM = 64
TOPK_MAX = 256
XA_HEADS = 4
XA_HEAD_DIM = D_MODEL // XA_HEADS
FFN_HIDDEN = -(-8 * D_MODEL // (3 * 256)) * 256
REL_BUCKETS = 32
REL_MAX_DIST = 128
ROPE_BASE = 10000.0
LN_EPS = 1e-5
RMS_EPS = 1e-6
ALPHA = (2 * DEPTH) ** 0.25

LANES = 128
HALF = LANES // 2
VMEM_LIMIT = 56 * 1024 * 1024

LOG2E = math.log2(math.e)
NEG_BIG = -1e30
F32 = jnp.float32
BF16 = jnp.bfloat16
I32 = jnp.int32
_NT = (((1,), (1,)), ((), ()))


def _dot(a, b):
    return jnp.dot(a, b, preferred_element_type=F32)


def _dot_nt(a, b):
    return lax.dot_general(a, b, _NT, preferred_element_type=F32)


def _layer_norm(y, g, b):
    mu = jnp.mean(y, axis=-1, keepdims=True)
    yc = y - mu
    var = jnp.mean(yc * yc, axis=-1, keepdims=True)
    return yc * lax.rsqrt(var + LN_EPS) * g + b


def _rms_norm(y, g):
    return y * lax.rsqrt(jnp.mean(y * y, axis=-1, keepdims=True) + RMS_EPS) * g


def _params(*sem):
    return pltpu.CompilerParams(dimension_semantics=sem, vmem_limit_bytes=VMEM_LIMIT)


def _const_spec(shape):
    nd = len(shape)
    return pl.BlockSpec(shape, lambda *_: (0,) * nd)


def _take_cols(w, idx):
    idx = np.asarray(idx)
    wp = jnp.concatenate([w, jnp.zeros((w.shape[0], 1), w.dtype)], axis=1)
    return wp[:, np.where(idx < 0, w.shape[1], idx)]


_R = MLA_ROPE // 2
_MLA_W = MLA_Q_RANK + MLA_KV_RANK


def _mla_proj_kernel(x_ref, pos_ref, win_ref, qg_ref, kvg_ref, wuq_ref, wuk_ref, wuv_ref,
                     freq_ref, sgn_ref, q_ref, k_ref, v_ref):
    xb = x_ref[0].astype(BF16)
    h = _dot(xb, win_ref[...])
    cq = _rms_norm(h[:, :MLA_Q_RANK], qg_ref[...]).astype(BF16)
    ckv = _rms_norm(h[:, MLA_Q_RANK:_MLA_W], kvg_ref[...]).astype(BF16)
    kr = h[:, _MLA_W:]
    ang = pos_ref[0].astype(F32) * freq_ref[...]
    cos = jnp.cos(ang)
    sin = jnp.sin(ang) * sgn_ref[...]

    def rope(t):
        return t * cos + pltpu.roll(t, HALF, 1) * sin

    kr = rope(kr)
    q_all = _dot(cq, wuq_ref[...])
    k_all = _dot(ckv, wuk_ref[...])
    v_all = _dot(ckv, wuv_ref[...])
    qscale = (MLA_NOPE + MLA_ROPE) ** -0.5 * LOG2E
    ones_hi = (lax.broadcasted_iota(I32, (1, LANES), 1) >= HALF).astype(F32)
    for hd in range(MLA_HEADS):
        sl = slice(hd * LANES, (hd + 1) * LANES)
        q_ref[0, hd] = (rope(q_all[:, sl]) * qscale).astype(BF16)
        k_ref[0, hd] = (k_all[:, sl] + kr).astype(BF16)
        v_ref[0, hd] = (v_all[:, sl] + ones_hi).astype(BF16)


def _mla_proj(x, pos3, w_in, q_norm, w_uq, kv_norm, w_ukv, tm):
    b, s, _ = x.shape
    hq = MLA_NOPE + MLA_ROPE
    in_idx = list(range(_MLA_W)) + [_MLA_W + j for j in range(_R)] + [-1] * (HALF - _R) \
        + [_MLA_W + _R + j for j in range(_R)] + [-1] * (HALF - _R)
    q_idx, k_idx, v_idx = [], [], []
    n_lo = HALF - _R
    n_hi = MLA_NOPE - n_lo
    tail = [-1] * (HALF - _R - n_hi)
    for hd in range(MLA_HEADS):
        qb = hd * hq
        q_idx += [qb + MLA_NOPE + j for j in range(_R)] + [qb + j for j in range(n_lo)] \
            + [qb + MLA_NOPE + _R + j for j in range(_R)] + [qb + n_lo + j for j in range(n_hi)] + tail
        kb = hd * (MLA_NOPE + MLA_V)
        k_idx += [-1] * _R + [kb + j for j in range(n_lo)] + [-1] * _R + [kb + n_lo + j for j in range(n_hi)] + tail
        v_idx += [kb + MLA_NOPE + j for j in range(MLA_V)] + [-1] * (LANES - MLA_V)
    win = _take_cols(w_in, in_idx).astype(BF16)
    wuq = _take_cols(w_uq, q_idx).astype(BF16)
    wuk = _take_cols(w_ukv, k_idx).astype(BF16)
    wuv = _take_cols(w_ukv, v_idx).astype(BF16)
    inv_freq = ROPE_BASE ** (-jnp.arange(0, MLA_ROPE, 2, dtype=F32) / MLA_ROPE)
    zpad = jnp.zeros((HALF - _R,), F32)
    freq = jnp.concatenate([inv_freq, zpad, inv_freq, zpad])[None]
    sgn = jnp.concatenate([-jnp.ones((_R,), F32), zpad, jnp.ones((_R,), F32), zpad])[None]
    nq = MLA_HEADS * LANES
    grid = (b, s // tm)
    out_shape = (jax.ShapeDtypeStruct((b, MLA_HEADS, s, LANES), BF16),
                 jax.ShapeDtypeStruct((b, MLA_HEADS, s, LANES), BF16),
                 jax.ShapeDtypeStruct((b, MLA_HEADS, s, LANES), BF16))
    return pl.pallas_call(
        _mla_proj_kernel, out_shape=out_shape, grid=grid,
        in_specs=[pl.BlockSpec((1, tm, D_MODEL), lambda bi, i: (bi, i, 0)),
                  pl.BlockSpec((1, tm, 1), lambda bi, i: (bi, i, 0)),
                  _const_spec((D_MODEL, _MLA_W + LANES)),
                  _const_spec((1, MLA_Q_RANK)), _const_spec((1, MLA_KV_RANK)),
                  _const_spec((MLA_Q_RANK, nq)), _const_spec((MLA_KV_RANK, nq)),
                  _const_spec((MLA_KV_RANK, nq)),
                  _const_spec((1, LANES)), _const_spec((1, LANES))],
        out_specs=(pl.BlockSpec((1, MLA_HEADS, tm, LANES), lambda bi, i: (bi, 0, i, 0)),
                   pl.BlockSpec((1, MLA_HEADS, tm, LANES), lambda bi, i: (bi, 0, i, 0)),
                   pl.BlockSpec((1, MLA_HEADS, tm, LANES), lambda bi, i: (bi, 0, i, 0))),
        compiler_params=_params("parallel", "parallel"), name="mla_proj",
    )(x, pos3, win, q_norm[None], kv_norm[None], wuq, wuk, wuv, freq, sgn)


def _flash_kernel(q_ref, k_ref, v_ref, o_ref, m_sc, acc_sc, s_sc, *, t, nh):
    i = pl.program_id(2)
    row = lax.broadcasted_iota(I32, (t, t), 0)
    col = lax.broadcasted_iota(I32, (t, t), 1)
    m_sc[...] = jnp.full(m_sc.shape, NEG_BIG, F32)
    acc_sc[...] = jnp.zeros(acc_sc.shape, F32)

    def scores(j, hh):
        kc = k_ref[0, hh, pl.ds(pl.multiple_of(j * t, t), t), :]
        return _dot_nt(q_ref[0, hh], kc)

    def step(j, slot, diag):
        for hh in range(nh):
            sc = s_sc[slot, hh]
            if diag:
                sc = jnp.where(col <= row, sc, NEG_BIG)
            else:
                s_sc[1 - slot, hh] = scores(j + 1, hh)
            m_prev = m_sc[hh]
            m_new = jnp.maximum(m_prev, jnp.max(sc, axis=-1, keepdims=True))
            a = jnp.exp2(m_prev - m_new)
            p = jnp.concatenate([jnp.exp2(sc[:, c * LANES:(c + 1) * LANES] - m_new)
                                 for c in range(t // LANES)], axis=1).astype(BF16)
            vc = v_ref[0, hh, pl.ds(pl.multiple_of(j * t, t), t), :]
            acc_sc[hh] = a * acc_sc[hh] + _dot(p, vc)
            m_sc[hh] = m_new

    for hh in range(nh):
        s_sc[0, hh] = scores(0, hh)

    def pair(j2, c):
        step(2 * j2, 0, False)
        step(2 * j2 + 1, 1, False)
        return c

    lax.fori_loop(0, i // 2, pair, 0)

    @pl.when(i % 2 == 1)
    def _():
        step(i - 1, 0, False)
        step(i, 1, True)

    @pl.when(i % 2 == 0)
    def _():
        step(i, 0, True)

    lane = lax.broadcasted_iota(I32, (t, LANES), 1)
    for pr in range(nh // 2):
        a0 = acc_sc[2 * pr]
        a1 = acc_sc[2 * pr + 1]
        o0 = a0 / a0[:, HALF:HALF + 1]
        o1 = a1 / a1[:, HALF:HALF + 1]
        o_ref[0, :, pr * LANES:(pr + 1) * LANES] = \
            jnp.where(lane < HALF, o0, pltpu.roll(o1, HALF, 1)).astype(BF16)


MLA_HEADS_PER_STEP = 4


def _flash_mla(q, k, v, t):
    b, h, s, _ = q.shape
    nh = MLA_HEADS_PER_STEP
    return pl.pallas_call(
        functools.partial(_flash_kernel, t=t, nh=nh),
        out_shape=jax.ShapeDtypeStruct((b, s, h * MLA_V), BF16),
        grid=(b, h // nh, s // t),
        in_specs=[pl.BlockSpec((1, nh, t, LANES), lambda bi, p, i: (bi, p, i, 0)),
                  pl.BlockSpec((1, nh, s, LANES), lambda bi, p, i: (bi, p, 0, 0)),
                  pl.BlockSpec((1, nh, s, LANES), lambda bi, p, i: (bi, p, 0, 0))],
        out_specs=pl.BlockSpec((1, t, nh * MLA_V), lambda bi, p, i: (bi, i, p)),
        scratch_shapes=[pltpu.VMEM((nh, t, LANES), F32), pltpu.VMEM((nh, t, LANES), F32),
                        pltpu.VMEM((2, nh, t, t), F32)],
        compiler_params=_params("parallel", "parallel", "arbitrary"), name="mla_flash",
    )(q, k, v)


_DQ0 = 0
_DK0 = DSA_HEADS * LANES
_DV0 = _DK0 + DSA_KV_HEADS * LANES
_DI0 = _DV0 + DSA_KV_HEADS * LANES
_DW0 = _DI0 + IDX_HEADS * LANES
_DN = _DW0 + LANES


def _dsa_proj_kernel(x_ref, w_ref, qhot_ref, krow_ref, wsc_ref, qa_ref, ka_ref, v2_ref, qi_ref, kw_ref, wf_ref):
    xb = x_ref[0].astype(BF16)
    h = _dot(xb, w_ref[...])
    qscale = DSA_HEAD_DIM ** -0.5 * LOG2E
    for hd in range(DSA_HEADS):
        sl = slice(_DQ0 + hd * LANES, _DQ0 + (hd + 1) * LANES)
        qa_ref[0, hd] = (h[:, sl] * qscale + qhot_ref[hd:hd + 1, :]).astype(BF16)
    ones_hi = (lax.broadcasted_iota(I32, (1, LANES), 1) >= HALF).astype(F32)
    for g in range(DSA_KV_HEADS):
        ka_ref[0, g] = (h[:, _DK0 + g * LANES:_DK0 + (g + 1) * LANES] + krow_ref[g:g + 1, :]).astype(BF16)
        v2_ref[0, g] = (h[:, _DV0 + g * LANES:_DV0 + (g + 1) * LANES] + ones_hi).astype(BF16)
    for hd in range(IDX_HEADS):
        qi_ref[0, hd] = (h[:, _DI0 + hd * LANES:_DI0 + (hd + 1) * LANES] * (IDX_DIM ** -0.5)).astype(BF16)
    kw = h[:, _DW0:]
    kw_ref[0] = kw.astype(BF16)
    wf_ref[0] = kw * wsc_ref[...]


def _dsa_proj(x, w_in, rel_bias, tm):
    b, s, _ = x.shape
    o1 = DSA_HEADS * DSA_HEAD_DIM
    o2 = o1 + DSA_KV_HEADS * DSA_HEAD_DIM
    o3 = o2 + DSA_KV_HEADS * DSA_HEAD_DIM
    o4 = o3 + IDX_HEADS * IDX_DIM
    o5 = o4 + IDX_DIM
    idx = []
    for hd in range(DSA_HEADS):
        idx += [hd * DSA_HEAD_DIM + j for j in range(DSA_HEAD_DIM)] + [-1] * HALF
    for g in range(DSA_KV_HEADS):
        idx += [o1 + g * DSA_HEAD_DIM + j for j in range(DSA_HEAD_DIM)] + [-1] * HALF
    for g in range(DSA_KV_HEADS):
        idx += [o2 + g * DSA_HEAD_DIM + j for j in range(DSA_HEAD_DIM)] + [-1] * HALF
    for hd in range(IDX_HEADS):
        idx += [o3 + hd * IDX_DIM + j for j in range(IDX_DIM)] + [-1] * HALF
    idx += [o4 + j for j in range(IDX_DIM)] + [o5 + j for j in range(IDX_HEADS)] + [-1] * (HALF - IDX_HEADS)
    assert len(idx) == _DN
    w = _take_cols(w_in, idx).astype(BF16)
    cfar = rel_bias[REL_BUCKETS - 1] * LOG2E
    chi = cfar.astype(BF16).astype(F32)
    clo = (cfar - chi).astype(BF16).astype(F32)
    qhot = np.zeros((DSA_HEADS, LANES), np.float32)
    for hd in range(DSA_HEADS):
        qhot[hd, HALF + hd % DSA_GROUP] = 1.0
        qhot[hd, HALF + DSA_GROUP + hd % DSA_GROUP] = 1.0
    zpad = jnp.zeros((DSA_KV_HEADS, LANES - HALF - 2 * DSA_GROUP), F32)
    krow = jnp.concatenate([jnp.zeros((DSA_KV_HEADS, HALF), F32), chi.reshape(DSA_KV_HEADS, DSA_GROUP),
                            clo.reshape(DSA_KV_HEADS, DSA_GROUP), zpad], axis=1)
    wsc = np.zeros((1, LANES), np.float32)
    wsc[0, HALF:HALF + IDX_HEADS] = IDX_HEADS ** -0.5
    hspec = lambda n: pl.BlockSpec((1, n, tm, LANES), lambda bi, i: (bi, 0, i, 0))
    tspec = pl.BlockSpec((1, tm, LANES), lambda bi, i: (bi, i, 0))
    out_shape = (jax.ShapeDtypeStruct((b, DSA_HEADS, s, LANES), BF16),
                 jax.ShapeDtypeStruct((b, DSA_KV_HEADS, s, LANES), BF16),
                 jax.ShapeDtypeStruct((b, DSA_KV_HEADS, s, LANES), BF16),
                 jax.ShapeDtypeStruct((b, IDX_HEADS, s, LANES), BF16),
                 jax.ShapeDtypeStruct((b, s, LANES), BF16),
                 jax.ShapeDtypeStruct((b, s, LANES), F32))
    return pl.pallas_call(
        _dsa_proj_kernel, out_shape=out_shape, grid=(b, s // tm),
        in_specs=[pl.BlockSpec((1, tm, D_MODEL), lambda bi, i: (bi, i, 0)),
                  _const_spec((D_MODEL, _DN)), _const_spec((DSA_HEADS, LANES)),
                  _const_spec((DSA_KV_HEADS, LANES)), _const_spec((1, LANES))],
        out_specs=(hspec(DSA_HEADS), hspec(DSA_KV_HEADS), hspec(DSA_KV_HEADS), hspec(IDX_HEADS), tspec, tspec),
        compiler_params=_params("parallel", "parallel"), name="dsa_proj",
    )(x, w, jnp.asarray(qhot), krow, jnp.asarray(wsc))


def _t5_tiles_kernel(rb_ref, o_ref):
    d = pl.program_id(0)
    hd = pl.program_id(1)
    row = lax.broadcasted_iota(I32, (LANES, LANES), 0)
    col = lax.broadcasted_iota(I32, (LANES, LANES), 1)
    n = jnp.maximum(d * LANES + row - col, 0)
    max_exact = REL_BUCKETS // 2
    nf = jnp.maximum(n, 1).astype(F32)
    large = max_exact + (jnp.log(nf / max_exact) / math.log(REL_MAX_DIST / max_exact)
                         * (REL_BUCKETS - max_exact)).astype(I32)
    large = jnp.minimum(large, REL_BUCKETS - 1)
    bucket = jnp.where(n < max_exact, n, large)
    far = rb_ref[REL_BUCKETS - 1, hd]
    acc = jnp.zeros((LANES, LANES), F32)
    for bk in range(REL_BUCKETS - 1):
        acc = jnp.where(bucket == bk, rb_ref[bk, hd] - far, acc)
    o_ref[0, 0] = acc * LOG2E


def _t5_tiles(rel_bias):
    return pl.pallas_call(
        _t5_tiles_kernel,
        out_shape=jax.ShapeDtypeStruct((2, DSA_HEADS, LANES, LANES), F32),
        grid=(2, DSA_HEADS),
        in_specs=[pl.BlockSpec(memory_space=pltpu.SMEM)],
        out_specs=pl.BlockSpec((1, 1, LANES, LANES), lambda d, h: (d, h, 0, 0)),
        compiler_params=_params("parallel", "parallel"), name="t5_tiles",
    )(rel_bias)


_SENT_KEY = int(np.int32(np.uint32(0xFF800000) ^ np.uint32(0x7FFFFFFF)))
_INT_MIN = -(2 ** 31)


def _dsa_attn_kernel(qa_ref, qi_ref, wf_ref, ka_ref, v_ref, kw_ref, tb_ref, o_ref,
                     keyt_sc, madd_sc, s_sc, m_sc, acc_sc, cut_sc, thr_sc, cge_sc, *, t, topk, s_len):
    i = pl.program_id(1)
    nsub = t // LANES
    kidx = lax.broadcasted_iota(I32, (t, t), 0)
    qidx = lax.broadcasted_iota(I32, (t, t), 1)

    qi = qi_ref[0].reshape(IDX_HEADS * t, LANES)
    wt = wf_ref[0].T
    wrows = [wt[HALF + hd:HALF + hd + 1, :] for hd in range(IDX_HEADS)]

    def score_keys_t(j0, w):
        kc = kw_ref[0, pl.ds(pl.multiple_of(j0 * t, t), w * t), :]
        d = _dot_nt(kc, qi)
        sc = wrows[0] * jnp.maximum(d[:, 0:t], 0.0)
        for hd in range(1, IDX_HEADS):
            sc = sc + wrows[hd] * jnp.maximum(d[:, hd * t:(hd + 1) * t], 0.0)
        bits = pltpu.bitcast(sc + 0.0, I32)
        return bits ^ ((bits >> 31) & 0x7FFFFFFF)

    def p1(u, c):
        keys = score_keys_t(2 * u, 2)
        keyt_sc[2 * u] = keys[0:t]
        keyt_sc[2 * u + 1] = keys[t:2 * t]
        return c

    lax.fori_loop(0, i // 2, p1, 0)

    @pl.when(i % 2 == 1)
    def _():
        keys = score_keys_t(i - 1, 2)
        keyt_sc[i - 1] = keys[0:t]
        keyt_sc[i] = jnp.where(kidx <= qidx, keys[t:2 * t], _SENT_KEY)

    @pl.when(i % 2 == 0)
    def _():
        keyt_sc[i] = jnp.where(kidx <= qidx, score_keys_t(i, 1), _SENT_KEY)

    cacc = 32

    def count(pred):
        def body(j, acc):
            c = jnp.where(pred(keyt_sc[j], j), 1.0, 0.0)
            return acc + jnp.sum(c.reshape(t // cacc, cacc, t), axis=0)
        acc = lax.fori_loop(0, i + 1, body, jnp.zeros((cacc, t), F32))
        return jnp.sum(acc, axis=0, keepdims=True)

    kf = float(topk)

    def bisect(nchunks):
        def count_ge(cand):
            acc = jnp.zeros((cacc, t), F32)
            for j in range(nchunks):
                c = jnp.where(keyt_sc[j] >= cand, 1.0, 0.0)
                acc = acc + jnp.sum(c.reshape(t // cacc, cacc, t), axis=0)
            return jnp.sum(acc, axis=0, keepdims=True)

        cnt0 = count_ge(jnp.zeros((1, t), I32))
        nonneg = cnt0 >= kf
        prefix0 = jnp.where(nonneg, 0, _INT_MIN).astype(I32)
        cge0 = jnp.where(nonneg, cnt0, float(nchunks * t))

        def bit_body(bi, carry):
            prefix, cge = carry
            cand = prefix + lax.shift_left(jnp.int32(1), 30 - bi)
            cnt = count_ge(cand)
            take = cnt >= kf
            return jnp.where(take, cand, prefix), jnp.where(take, cnt, cge)

        thr, cge = lax.fori_loop(0, 31, bit_body, (prefix0, cge0))
        thr_sc[...] = jnp.broadcast_to(thr, thr_sc.shape)
        cge_sc[...] = jnp.broadcast_to(cge, cge_sc.shape)

    for ii in range(s_len // t):
        pl.when(i == ii)(functools.partial(bisect, ii + 1))
    thr = thr_sc[0:1, :]
    cnt_ge = cge_sc[0:1, :]
    excess = jnp.where((cnt_ge > kf) & (thr > _SENT_KEY), 1.0, 0.0)
    cut_sc[...] = jnp.full(cut_sc.shape, s_len, I32)

    @pl.when(jnp.max(excess) > 0.0)
    def _():
        nbits = max(1, (s_len - 1).bit_length())
        need = kf - count(lambda kc, j: kc > thr)

        def tie_body(bi, c):
            cand = c + lax.shift_left(jnp.int32(1), nbits - 1 - bi)
            cnt = count(lambda kc, j: (kc == thr) & (kidx + j * t < cand))
            return jnp.where(cnt < need, cand, c)

        c = lax.fori_loop(0, nbits, tie_body, jnp.zeros((1, t), I32))
        cut_sc[...] = jnp.broadcast_to(c, cut_sc.shape)

    cut = cut_sc[0:1, :]

    def to_mask(j, causal):
        kc = keyt_sc[j]
        sel = (kc > thr) | ((kc == thr) & (kidx + j * t <= cut))
        if causal:
            sel = sel & (kidx <= qidx)
        return jnp.where(sel, 0.0, NEG_BIG).T

    m_sc[...] = jnp.full(m_sc.shape, NEG_BIG, F32)
    acc_sc[...] = jnp.zeros(acc_sc.shape, F32)
    gt = DSA_GROUP * t
    near_delta = {"prev": nsub, "diag": 0}

    def attend(j0, parts):
        w = len(parts)
        start = pl.multiple_of(j0 * t, t)

        def qk(g):
            q = qa_ref[0, g * DSA_GROUP:(g + 1) * DSA_GROUP].reshape(gt, LANES)
            return _dot_nt(q, ka_ref[0, g, pl.ds(start, w * t), :])

        s_sc[0, :, 0:w * t] = qk(0)
        for pi, part in enumerate(parts):
            madd_sc[pi] = to_mask(j0 + pi, part == "diag")
        for g in range(DSA_KV_HEADS):
            if g + 1 < DSA_KV_HEADS:
                s_sc[(g + 1) % 2, :, 0:w * t] = qk(g + 1)
            slot = g % 2
            p_rows, a_rows = [], []
            for r in range(DSA_GROUP):
                hd = g * DSA_GROUP + r
                for rs in range(nsub):
                    r0 = r * t + rs * LANES
                    blocks = []
                    for cb in range(w * nsub):
                        part, cs = parts[cb // nsub], cb % nsub
                        blk = s_sc[slot, r0:r0 + LANES, cb * LANES:(cb + 1) * LANES] \
                            + madd_sc[cb // nsub, rs * LANES:(rs + 1) * LANES, cs * LANES:(cs + 1) * LANES]
                        if part in near_delta and near_delta[part] + rs - cs in (0, 1):
                            blk = blk + tb_ref[near_delta[part] + rs - cs, hd]
                        blocks.append(blk)
                    m_prev = m_sc[g, r0:r0 + LANES]
                    m_blk = blocks[0]
                    for blk in blocks[1:]:
                        m_blk = jnp.maximum(m_blk, blk)
                    m_new = jnp.maximum(m_prev, jnp.max(m_blk, axis=-1, keepdims=True))
                    a_rows.append(jnp.exp2(m_prev - m_new))
                    p_rows.append(jnp.concatenate([jnp.exp2(blk - m_new).astype(BF16) for blk in blocks], axis=1))
                    m_sc[g, r0:r0 + LANES] = m_new
            p = jnp.concatenate(p_rows, axis=0)
            a = jnp.concatenate(a_rows, axis=0)
            vc = v_ref[0, g, pl.ds(start, w * t), :]
            acc_sc[g] = a * acc_sc[g] + _dot(p, vc)

    @pl.when(i == 0)
    def _():
        attend(0, ("diag",))

    @pl.when(i >= 1)
    def _():
        nfar = i - 1

        def far_pair(u, c):
            attend(2 * u, ("far", "far"))
            return c

        lax.fori_loop(0, nfar // 2, far_pair, 0)

        @pl.when(nfar % 2 == 1)
        def _():
            attend(i - 2, ("far",))

        attend(i - 1, ("prev", "diag"))

    lane = lax.broadcasted_iota(I32, (t, LANES), 1)
    for g in range(DSA_KV_HEADS):
        acc = acc_sc[g]
        o = acc / acc[:, HALF:HALF + 1]
        for pr in range(DSA_GROUP // 2):
            lo = o[(2 * pr) * t:(2 * pr + 1) * t]
            hi = o[(2 * pr + 1) * t:(2 * pr + 2) * t]
            cblk = g * (DSA_GROUP // 2) + pr
            o_ref[0, :, cblk * LANES:(cblk + 1) * LANES] = \
                jnp.where(lane < HALF, lo, pltpu.roll(hi, HALF, 1)).astype(BF16)


def _dsa_attn(qa, ka, v, qi, kw, wf, tb, t, topk):
    b, _, s, _ = qa.shape
    nch = s // t
    gt = DSA_GROUP * t
    kern = functools.partial(_dsa_attn_kernel, t=t, topk=topk, s_len=s)
    return pl.pallas_call(
        kern, out_shape=jax.ShapeDtypeStruct((b, s, DSA_HEADS * DSA_HEAD_DIM), BF16),
        grid=(b, nch),
        in_specs=[pl.BlockSpec((1, DSA_HEADS, t, LANES), lambda bi, i: (bi, 0, i, 0)),
                  pl.BlockSpec((1, IDX_HEADS, t, LANES), lambda bi, i: (bi, 0, i, 0)),
                  pl.BlockSpec((1, t, LANES), lambda bi, i: (bi, i, 0)),
                  pl.BlockSpec((1, DSA_KV_HEADS, s, LANES), lambda bi, i: (bi, 0, 0, 0)),
                  pl.BlockSpec((1, DSA_KV_HEADS, s, LANES), lambda bi, i: (bi, 0, 0, 0)),
                  pl.BlockSpec((1, s, LANES), lambda bi, i: (bi, 0, 0)),
                  _const_spec((2, DSA_HEADS, LANES, LANES))],
        out_specs=pl.BlockSpec((1, t, DSA_HEADS * DSA_HEAD_DIM), lambda bi, i: (bi, i, 0)),
        scratch_shapes=[pltpu.VMEM((nch, t, t), I32),
                        pltpu.VMEM((2, t, t), F32),
                        pltpu.VMEM((2, gt, 2 * t), F32),
                        pltpu.VMEM((DSA_KV_HEADS, gt, LANES), F32),
                        pltpu.VMEM((DSA_KV_HEADS, gt, LANES), F32),
                        pltpu.VMEM((8, t), I32),
                        pltpu.VMEM((8, t), I32),
                        pltpu.VMEM((8, t), F32)],
        compiler_params=_params("parallel", "arbitrary"), name="dsa_attn",
    )(qa, qi, wf, ka, v, kw, tb)


def _mm_kernel(a_ref, w_ref, o_ref):
    o_ref[...] = _dot(a_ref[...].astype(BF16), w_ref[...]).astype(BF16)


def _mm(a, w, tm):
    m, k = a.shape
    n = w.shape[1]
    return pl.pallas_call(
        _mm_kernel, out_shape=jax.ShapeDtypeStruct((m, n), BF16), grid=(m // tm,),
        in_specs=[pl.BlockSpec((tm, k), lambda i: (i, 0)), _const_spec((k, n))],
        out_specs=pl.BlockSpec((tm, n), lambda i: (i, 0)),
        compiler_params=_params("parallel"), name="mem_kv",
    )(a, w)


def _post_kernel(a_ref, x_ref, wo_ref, wq_ref, kv_ref, wxo_ref, g_ref, b_ref, o_ref, oc_sc):
    y = _dot(a_ref[0], wo_ref[...])
    x1 = _layer_norm(ALPHA * x_ref[0] + y, g_ref[0:1, :], b_ref[0:1, :])
    q = (_dot(x1.astype(BF16), wq_ref[...]) * (XA_HEAD_DIM ** -0.5 * LOG2E)).astype(BF16)
    for hd in range(XA_HEADS):
        sl = slice(hd * XA_HEAD_DIM, (hd + 1) * XA_HEAD_DIM)
        sc = _dot_nt(q[:, sl], kv_ref[0, :, sl])
        m = jnp.max(sc, axis=-1, keepdims=True)
        p = jnp.exp2(sc - m)
        l = jnp.sum(p, axis=-1, keepdims=True)
        vsl = slice(D_MODEL + hd * XA_HEAD_DIM, D_MODEL + (hd + 1) * XA_HEAD_DIM)
        oc_sc[:, sl] = (_dot(p.astype(BF16), kv_ref[0, :, vsl]) / l).astype(BF16)
    y2 = _dot(oc_sc[...], wxo_ref[...])
    o_ref[0] = _layer_norm(ALPHA * x1 + y2, g_ref[1:2, :], b_ref[1:2, :])


def _post(a, x, w_o, w_q, kv, w_xo, g, bta, tm):
    b, s, _ = x.shape
    mlen = kv.shape[1]
    tok = lambda bi, i: (bi, i, 0)
    return pl.pallas_call(
        _post_kernel, out_shape=jax.ShapeDtypeStruct((b, s, D_MODEL), F32), grid=(b, s // tm),
        in_specs=[pl.BlockSpec((1, tm, D_MODEL), tok), pl.BlockSpec((1, tm, D_MODEL), tok),
                  _const_spec((D_MODEL, D_MODEL)), _const_spec((D_MODEL, D_MODEL)),
                  pl.BlockSpec((1, mlen, 2 * D_MODEL), lambda bi, i: (bi, 0, 0)),
                  _const_spec((D_MODEL, D_MODEL)), _const_spec((3, D_MODEL)), _const_spec((3, D_MODEL))],
        out_specs=pl.BlockSpec((1, tm, D_MODEL), tok),
        scratch_shapes=[pltpu.VMEM((tm, D_MODEL), BF16)],
        compiler_params=_params("parallel", "parallel"), name="post_attn",
    )(a, x, w_o, w_q, kv, w_xo, g, bta)


FFN_CHUNK = 256


def _ffn_kernel(x_ref, wi_ref, wd_ref, g_ref, b_ref, o_ref, act_sc):
    x = x_ref[...]
    xb = x.astype(BF16)
    for c in range(FFN_HIDDEN // FFN_CHUNK):
        sl = slice(c * FFN_CHUNK, (c + 1) * FFN_CHUNK)
        gte = _dot(xb, wi_ref[:, sl])
        up = _dot(xb, wi_ref[:, FFN_HIDDEN + c * FFN_CHUNK:FFN_HIDDEN + (c + 1) * FFN_CHUNK])
        act_sc[:, sl] = (gte * jax.nn.sigmoid(gte) * up).astype(BF16)
    y = _dot(act_sc[...], wd_ref[...])
    o_ref[...] = _layer_norm(ALPHA * x + y, g_ref[2:3, :], b_ref[2:3, :])


def _ffn(x2d, w_in, w_down, g, bta, tm):
    n = x2d.shape[0]
    return pl.pallas_call(
        _ffn_kernel, out_shape=jax.ShapeDtypeStruct((n, D_MODEL), F32), grid=(n // tm,),
        in_specs=[pl.BlockSpec((tm, D_MODEL), lambda i: (i, 0)),
                  _const_spec((D_MODEL, 2 * FFN_HIDDEN)), _const_spec((FFN_HIDDEN, D_MODEL)),
                  _const_spec((3, D_MODEL)), _const_spec((3, D_MODEL))],
        out_specs=pl.BlockSpec((tm, D_MODEL), lambda i: (i, 0)),
        scratch_shapes=[pltpu.VMEM((tm, FFN_HIDDEN), BF16)],
        compiler_params=_params("parallel"), name="ffn",
    )(x2d, w_in.astype(BF16), w_down.astype(BF16), g, bta)


def _tile(n, pref):
    t = min(n, pref)
    assert n % t == 0, (n, pref)
    return t


def kernel(x, mem, positions, rel_bias, mla_w_in, mla_q_norm, mla_w_uq, mla_kv_norm, mla_w_ukv, mla_w_o,
           dsa_w_in, dsa_w_o, xa_w_q, xa_w_kv, xa_w_o, ffn_w_in, ffn_w_down, ln_g, ln_b):
    b, s, d = x.shape
    assert d == D_MODEL and s % LANES == 0
    mlen = mem.shape[1]
    topk = min(TOPK_MAX, s // 4)
    t_attn = _tile(s, 256)
    tm_proj = _tile(s, 256)
    tm_post = _tile(s, 512)
    tm_ffn = _tile(b * s, 512)
    pos3 = positions.reshape(b, s, 1)
    tb = _t5_tiles(rel_bias)
    for i in range(DEPTH):
        j = i // 2
        if i % 2 == 0:
            q, k, v = _mla_proj(x, pos3, mla_w_in[j], mla_q_norm[j], mla_w_uq[j], mla_kv_norm[j],
                                mla_w_ukv[j], tm_proj)
            a = _flash_mla(q, k, v, _tile(s, 512))
            w_o = mla_w_o[j]
        else:
            qa, ka, v2, qi, kw, wf = _dsa_proj(x, dsa_w_in[j], rel_bias, tm_proj)
            a = _dsa_attn(qa, ka, v2, qi, kw, wf, tb, t_attn, topk)
            w_o = dsa_w_o[j]
        kv = _mm(mem.reshape(b * mlen, d), xa_w_kv[i].astype(BF16), _tile(b * mlen, 512)).reshape(b, mlen, 2 * d)
        x = _post(a, x, w_o.astype(BF16), xa_w_q[i].astype(BF16), kv, xa_w_o[i].astype(BF16),
                  ln_g[i], ln_b[i], tm_post)
        x = _ffn(x.reshape(b * s, d), ffn_w_in[i], ffn_w_down[i], ln_g[i], ln_b[i], tm_ffn).reshape(b, s, d)
    return x
```

```python
import functools
import math

import numpy as np
import jax
import jax.numpy as jnp
from jax import lax
from jax.experimental import pallas as pl
from jax.experimental.pallas import tpu as pltpu

D_MODEL = 1024
DEPTH = 2
MLA_HEADS = 16
MLA_Q_RANK = 512
MLA_KV_RANK = 256
MLA_NOPE = 64
MLA_ROPE = 32
MLA_V = 64
DSA_HEADS = 16
DSA_KV_HEADS = 4
DSA_GROUP = DSA_HEADS // DSA_KV_HEADS
DSA_HEAD_DIM = 64
IDX_HEADS = 8
IDX_DIM = 64
TOPK_MAX = 256
XA_HEADS = 4
XA_HEAD_DIM = D_MODEL // XA_HEADS
FFN_HIDDEN = -(-8 * D_MODEL // (3 * 256)) * 256
REL_BUCKETS = 32
REL_MAX_DIST = 128
ROPE_BASE = 10000.0
LN_EPS = 1e-5
RMS_EPS = 1e-6
ALPHA = (2 * DEPTH) ** 0.25

LANES = 128
HALF = LANES // 2
VMEM_LIMIT = 56 * 1024 * 1024

LOG2E = math.log2(math.e)
NEG_BIG = -1e30
F32 = jnp.float32
BF16 = jnp.bfloat16
I32 = jnp.int32
_NT = (((1,), (1,)), ((), ()))


def _dot(a, b):
    return jnp.dot(a, b, preferred_element_type=F32)


def _dot_nt(a, b):
    return lax.dot_general(a, b, _NT, preferred_element_type=F32)


def _layer_norm(y, g, b):
    mu = jnp.mean(y, axis=-1, keepdims=True)
    yc = y - mu
    var = jnp.mean(yc * yc, axis=-1, keepdims=True)
    return yc * lax.rsqrt(var + LN_EPS) * g + b


def _rms_norm(y, g):
    return y * lax.rsqrt(jnp.mean(y * y, axis=-1, keepdims=True) + RMS_EPS) * g


def _params(*sem):
    return pltpu.CompilerParams(dimension_semantics=sem, vmem_limit_bytes=VMEM_LIMIT)


def _const_spec(shape):
    nd = len(shape)
    return pl.BlockSpec(shape, lambda *_: (0,) * nd)


def _take_cols(w, idx):
    idx = np.asarray(idx)
    wp = jnp.concatenate([w, jnp.zeros((w.shape[0], 1), w.dtype)], axis=1)
    return wp[:, np.where(idx < 0, w.shape[1], idx)]


_R = MLA_ROPE // 2
_MLA_W = MLA_Q_RANK + MLA_KV_RANK


def _mla_proj_kernel(x_ref, pos_ref, win_ref, qg_ref, kvg_ref, wuq_ref, wuk_ref, wuv_ref,
                     freq_ref, sgn_ref, q_ref, k_ref, v_ref):
    xb = x_ref[0].astype(BF16)
    h = _dot(xb, win_ref[...])
    cq = _rms_norm(h[:, :MLA_Q_RANK], qg_ref[...]).astype(BF16)
    ckv = _rms_norm(h[:, MLA_Q_RANK:_MLA_W], kvg_ref[...]).astype(BF16)
    kr = h[:, _MLA_W:]
    ang = pos_ref[0].astype(F32) * freq_ref[...]
    cos = jnp.cos(ang)
    sin = jnp.sin(ang) * sgn_ref[...]

    def rope(t):
        return t * cos + pltpu.roll(t, HALF, 1) * sin

    kr = rope(kr)
    q_all = _dot(cq, wuq_ref[...])
    k_all = _dot(ckv, wuk_ref[...])
    v_all = _dot(ckv, wuv_ref[...])
    qscale = (MLA_NOPE + MLA_ROPE) ** -0.5 * LOG2E
    ones_hi = (lax.broadcasted_iota(I32, (1, LANES), 1) >= HALF).astype(F32)
    for hd in range(MLA_HEADS):
        sl = slice(hd * LANES, (hd + 1) * LANES)
        q_ref[0, hd] = (rope(q_all[:, sl]) * qscale).astype(BF16)
        k_ref[0, hd] = (k_all[:, sl] + kr).astype(BF16)
        v_ref[0, hd] = (v_all[:, sl] + ones_hi).astype(BF16)


def _mla_proj(x, pos3, w_in, q_norm, w_uq, kv_norm, w_ukv, tm):
    b, s, _ = x.shape
    hq = MLA_NOPE + MLA_ROPE
    in_idx = list(range(_MLA_W)) + [_MLA_W + j for j in range(_R)] + [-1] * (HALF - _R) \
        + [_MLA_W + _R + j for j in range(_R)] + [-1] * (HALF - _R)
    q_idx, k_idx, v_idx = [], [], []
    n_lo = HALF - _R
    n_hi = MLA_NOPE - n_lo
    tail = [-1] * (HALF - _R - n_hi)
    for hd in range(MLA_HEADS):
        qb = hd * hq
        q_idx += [qb + MLA_NOPE + j for j in range(_R)] + [qb + j for j in range(n_lo)] \
            + [qb + MLA_NOPE + _R + j for j in range(_R)] + [qb + n_lo + j for j in range(n_hi)] + tail
        kb = hd * (MLA_NOPE + MLA_V)
        k_idx += [-1] * _R + [kb + j for j in range(n_lo)] + [-1] * _R + [kb + n_lo + j for j in range(n_hi)] + tail
        v_idx += [kb + MLA_NOPE + j for j in range(MLA_V)] + [-1] * (LANES - MLA_V)
    win = _take_cols(w_in, in_idx).astype(BF16)
    wuq = _take_cols(w_uq, q_idx).astype(BF16)
    wuk = _take_cols(w_ukv, k_idx).astype(BF16)
    wuv = _take_cols(w_ukv, v_idx).astype(BF16)
    inv_freq = ROPE_BASE ** (-jnp.arange(0, MLA_ROPE, 2, dtype=F32) / MLA_ROPE)
    zpad = jnp.zeros((HALF - _R,), F32)
    freq = jnp.concatenate([inv_freq, zpad, inv_freq, zpad])[None]
    sgn = jnp.concatenate([-jnp.ones((_R,), F32), zpad, jnp.ones((_R,), F32), zpad])[None]
    nq = MLA_HEADS * LANES
    grid = (b, s // tm)
    out_shape = (jax.ShapeDtypeStruct((b, MLA_HEADS, s, LANES), BF16),
                 jax.ShapeDtypeStruct((b, MLA_HEADS, s, LANES), BF16),
                 jax.ShapeDtypeStruct((b, MLA_HEADS, s, LANES), BF16))
    return pl.pallas_call(
        _mla_proj_kernel, out_shape=out_shape, grid=grid,
        in_specs=[pl.BlockSpec((1, tm, D_MODEL), lambda bi, i: (bi, i, 0)),
                  pl.BlockSpec((1, tm, 1), lambda bi, i: (bi, i, 0)),
                  _const_spec((D_MODEL, _MLA_W + LANES)),
                  _const_spec((1, MLA_Q_RANK)), _const_spec((1, MLA_KV_RANK)),
                  _const_spec((MLA_Q_RANK, nq)), _const_spec((MLA_KV_RANK, nq)),
                  _const_spec((MLA_KV_RANK, nq)),
                  _const_spec((1, LANES)), _const_spec((1, LANES))],
        out_specs=(pl.BlockSpec((1, MLA_HEADS, tm, LANES), lambda bi, i: (bi, 0, i, 0)),
                   pl.BlockSpec((1, MLA_HEADS, tm, LANES), lambda bi, i: (bi, 0, i, 0)),
                   pl.BlockSpec((1, MLA_HEADS, tm, LANES), lambda bi, i: (bi, 0, i, 0))),
        compiler_params=_params("parallel", "parallel"), name="mla_proj",
    )(x, pos3, win, q_norm[None], kv_norm[None], wuq, wuk, wuv, freq, sgn)


def _flash_kernel(q_ref, k_ref, v_ref, o_ref, m_sc, acc_sc, s_sc, *, t, nh):
    i = pl.program_id(2)
    row = lax.broadcasted_iota(I32, (t, t), 0)
    col = lax.broadcasted_iota(I32, (t, t), 1)
    m_sc[...] = jnp.full(m_sc.shape, NEG_BIG, F32)
    acc_sc[...] = jnp.zeros(acc_sc.shape, F32)

    def scores(j, hh):
        kc = k_ref[0, hh, pl.ds(pl.multiple_of(j * t, t), t), :]
        return _dot_nt(q_ref[0, hh], kc)

    def step(j, slot, diag):
        for hh in range(nh):
            sc = s_sc[slot, hh]
            if diag:
                sc = jnp.where(col <= row, sc, NEG_BIG)
            else:
                s_sc[1 - slot, hh] = scores(j + 1, hh)
            m_prev = m_sc[hh]
            m_new = jnp.maximum(m_prev, jnp.max(sc, axis=-1, keepdims=True))
            a = jnp.exp2(m_prev - m_new)
            p = jnp.concatenate([jnp.exp2(sc[:, c * LANES:(c + 1) * LANES] - m_new)
                                 for c in range(t // LANES)], axis=1).astype(BF16)
            vc = v_ref[0, hh, pl.ds(pl.multiple_of(j * t, t), t), :]
            acc_sc[hh] = a * acc_sc[hh] + _dot(p, vc)
            m_sc[hh] = m_new

    for hh in range(nh):
        s_sc[0, hh] = scores(0, hh)

    def pair(j2, c):
        step(2 * j2, 0, False)
        step(2 * j2 + 1, 1, False)
        return c

    lax.fori_loop(0, i // 2, pair, 0)

    @pl.when(i % 2 == 1)
    def _():
        step(i - 1, 0, False)
        step(i, 1, True)

    @pl.when(i % 2 == 0)
    def _():
        step(i, 0, True)

    lane = lax.broadcasted_iota(I32, (t, LANES), 1)
    for pr in range(nh // 2):
        a0 = acc_sc[2 * pr]
        a1 = acc_sc[2 * pr + 1]
        o0 = a0 / a0[:, HALF:HALF + 1]
        o1 = a1 / a1[:, HALF:HALF + 1]
        o_ref[0, :, pr * LANES:(pr + 1) * LANES] = \
            jnp.where(lane < HALF, o0, pltpu.roll(o1, HALF, 1)).astype(BF16)


MLA_HEADS_PER_STEP = 4


def _flash_mla(q, k, v, t):
    b, h, s, _ = q.shape
    nh = MLA_HEADS_PER_STEP
    return pl.pallas_call(
        functools.partial(_flash_kernel, t=t, nh=nh),
        out_shape=jax.ShapeDtypeStruct((b, s, h * MLA_V), BF16),
        grid=(b, h // nh, s // t),
        in_specs=[pl.BlockSpec((1, nh, t, LANES), lambda bi, p, i: (bi, p, i, 0)),
                  pl.BlockSpec((1, nh, s, LANES), lambda bi, p, i: (bi, p, 0, 0)),
                  pl.BlockSpec((1, nh, s, LANES), lambda bi, p, i: (bi, p, 0, 0))],
        out_specs=pl.BlockSpec((1, t, nh * MLA_V), lambda bi, p, i: (bi, i, p)),
        scratch_shapes=[pltpu.VMEM((nh, t, LANES), F32), pltpu.VMEM((nh, t, LANES), F32),
                        pltpu.VMEM((2, nh, t, t), F32)],
        compiler_params=_params("parallel", "parallel", "arbitrary"), name="mla_flash",
    )(q, k, v)


_DQ0 = 0
_DK0 = DSA_HEADS * LANES
_DV0 = _DK0 + DSA_KV_HEADS * LANES
_DI0 = _DV0 + DSA_KV_HEADS * LANES
_DW0 = _DI0 + IDX_HEADS * LANES
_DN = _DW0 + LANES


def _dsa_proj_kernel(x_ref, w_ref, qhot_ref, krow_ref, wsc_ref, qa_ref, ka_ref, v2_ref, qi_ref, kw_ref, wf_ref):
    xb = x_ref[0].astype(BF16)
    h = _dot(xb, w_ref[...])
    qscale = DSA_HEAD_DIM ** -0.5 * LOG2E
    for hd in range(DSA_HEADS):
        sl = slice(_DQ0 + hd * LANES, _DQ0 + (hd + 1) * LANES)
        qa_ref[0, hd] = (h[:, sl] * qscale + qhot_ref[hd:hd + 1, :]).astype(BF16)
    ones_hi = (lax.broadcasted_iota(I32, (1, LANES), 1) >= HALF).astype(F32)
    for g in range(DSA_KV_HEADS):
        ka_ref[0, g] = (h[:, _DK0 + g * LANES:_DK0 + (g + 1) * LANES] + krow_ref[g:g + 1, :]).astype(BF16)
        v2_ref[0, g] = (h[:, _DV0 + g * LANES:_DV0 + (g + 1) * LANES] + ones_hi).astype(BF16)
    for hd in range(IDX_HEADS):
        qi_ref[0, hd] = (h[:, _DI0 + hd * LANES:_DI0 + (hd + 1) * LANES] * (IDX_DIM ** -0.5)).astype(BF16)
    kw = h[:, _DW0:]
    kw_ref[0] = kw.astype(BF16)
    wf_ref[0] = kw * wsc_ref[...]


def _dsa_proj(x, w_in, rel_bias, tm):
    b, s, _ = x.shape
    o1 = DSA_HEADS * DSA_HEAD_DIM
    o2 = o1 + DSA_KV_HEADS * DSA_HEAD_DIM
    o3 = o2 + DSA_KV_HEADS * DSA_HEAD_DIM
    o4 = o3 + IDX_HEADS * IDX_DIM
    o5 = o4 + IDX_DIM
    idx = []
    for hd in range(DSA_HEADS):
        idx += [hd * DSA_HEAD_DIM + j for j in range(DSA_HEAD_DIM)] + [-1] * HALF
    for g in range(DSA_KV_HEADS):
        idx += [o1 + g * DSA_HEAD_DIM + j for j in range(DSA_HEAD_DIM)] + [-1] * HALF
    for g in range(DSA_KV_HEADS):
        idx += [o2 + g * DSA_HEAD_DIM + j for j in range(DSA_HEAD_DIM)] + [-1] * HALF
    for hd in range(IDX_HEADS):
        idx += [o3 + hd * IDX_DIM + j for j in range(IDX_DIM)] + [-1] * HALF
    idx += [o4 + j for j in range(IDX_DIM)] + [o5 + j for j in range(IDX_HEADS)] + [-1] * (HALF - IDX_HEADS)
    assert len(idx) == _DN
    w = _take_cols(w_in, idx).astype(BF16)
    cfar = rel_bias[REL_BUCKETS - 1] * LOG2E
    chi = cfar.astype(BF16).astype(F32)
    clo = (cfar - chi).astype(BF16).astype(F32)
    qhot = np.zeros((DSA_HEADS, LANES), np.float32)
    for hd in range(DSA_HEADS):
        qhot[hd, HALF + hd % DSA_GROUP] = 1.0
        qhot[hd, HALF + DSA_GROUP + hd % DSA_GROUP] = 1.0
    zpad = jnp.zeros((DSA_KV_HEADS, LANES - HALF - 2 * DSA_GROUP), F32)
    krow = jnp.concatenate([jnp.zeros((DSA_KV_HEADS, HALF), F32), chi.reshape(DSA_KV_HEADS, DSA_GROUP),
                            clo.reshape(DSA_KV_HEADS, DSA_GROUP), zpad], axis=1)
    wsc = np.zeros((1, LANES), np.float32)
    wsc[0, HALF:HALF + IDX_HEADS] = IDX_HEADS ** -0.5
    hspec = lambda n: pl.BlockSpec((1, n, tm, LANES), lambda bi, i: (bi, 0, i, 0))
    tspec = pl.BlockSpec((1, tm, LANES), lambda bi, i: (bi, i, 0))
    out_shape = (jax.ShapeDtypeStruct((b, DSA_HEADS, s, LANES), BF16),
                 jax.ShapeDtypeStruct((b, DSA_KV_HEADS, s, LANES), BF16),
                 jax.ShapeDtypeStruct((b, DSA_KV_HEADS, s, LANES), BF16),
                 jax.ShapeDtypeStruct((b, IDX_HEADS, s, LANES), BF16),
                 jax.ShapeDtypeStruct((b, s, LANES), BF16),
                 jax.ShapeDtypeStruct((b, s, LANES), F32))
    return pl.pallas_call(
        _dsa_proj_kernel, out_shape=out_shape, grid=(b, s // tm),
        in_specs=[pl.BlockSpec((1, tm, D_MODEL), lambda bi, i: (bi, i, 0)),
                  _const_spec((D_MODEL, _DN)), _const_spec((DSA_HEADS, LANES)),
                  _const_spec((DSA_KV_HEADS, LANES)), _const_spec((1, LANES))],
        out_specs=(hspec(DSA_HEADS), hspec(DSA_KV_HEADS), hspec(DSA_KV_HEADS), hspec(IDX_HEADS), tspec, tspec),
        compiler_params=_params("parallel", "parallel"), name="dsa_proj",
    )(x, w, jnp.asarray(qhot), krow, jnp.asarray(wsc))


def _t5_tiles_kernel(rb_ref, o_ref):
    d = pl.program_id(0)
    hd = pl.program_id(1)
    row = lax.broadcasted_iota(I32, (LANES, LANES), 0)
    col = lax.broadcasted_iota(I32, (LANES, LANES), 1)
    n = jnp.maximum(d * LANES + row - col, 0)
    max_exact = REL_BUCKETS // 2
    nf = jnp.maximum(n, 1).astype(F32)
    large = max_exact + (jnp.log(nf / max_exact) / math.log(REL_MAX_DIST / max_exact)
                         * (REL_BUCKETS - max_exact)).astype(I32)
    large = jnp.minimum(large, REL_BUCKETS - 1)
    bucket = jnp.where(n < max_exact, n, large)
    far = rb_ref[REL_BUCKETS - 1, hd]
    acc = jnp.zeros((LANES, LANES), F32)
    for bk in range(REL_BUCKETS - 1):
        acc = jnp.where(bucket == bk, rb_ref[bk, hd] - far, acc)
    o_ref[0, 0] = acc * LOG2E


def _t5_tiles(rel_bias):
    return pl.pallas_call(
        _t5_tiles_kernel,
        out_shape=jax.ShapeDtypeStruct((2, DSA_HEADS, LANES, LANES), F32),
        grid=(2, DSA_HEADS),
        in_specs=[pl.BlockSpec(memory_space=pltpu.SMEM)],
        out_specs=pl.BlockSpec((1, 1, LANES, LANES), lambda d, h: (d, h, 0, 0)),
        compiler_params=_params("parallel", "parallel"), name="t5_tiles",
    )(rel_bias)


_SENT_KEY = int(np.int32(np.uint32(0xFF800000) ^ np.uint32(0x7FFFFFFF)))
_HI16 = -65536
_MIN_NORMAL_BITS = 0x00800000
_INT_MIN = -(2 ** 31)


def _dsa_attn_kernel(qa_ref, qi_ref, wf_ref, ka_ref, v_ref, kw_ref, tb_ref, o_ref,
                     keyt_sc, kb_sc, madd_sc, s_sc, m_sc, acc_sc, cut_sc, thr_sc, cge_sc, *, t, topk, s_len):
    i = pl.program_id(1)
    nsub = t // LANES
    kidx = lax.broadcasted_iota(I32, (t, t), 0)
    qidx = lax.broadcasted_iota(I32, (t, t), 1)

    qi = qi_ref[0].reshape(IDX_HEADS * t, LANES)
    wt = wf_ref[0].T
    wrows = [wt[HALF + hd:HALF + hd + 1, :] for hd in range(IDX_HEADS)]

    def scores_t(j0, w):
        kc = kw_ref[0, pl.ds(pl.multiple_of(j0 * t, t), w * t), :]
        d = _dot_nt(kc, qi)
        sc = wrows[0] * jnp.maximum(d[:, 0:t], 0.0)
        for hd in range(1, IDX_HEADS):
            sc = sc + wrows[hd] * jnp.maximum(d[:, hd * t:(hd + 1) * t], 0.0)
        return sc

    def store_keys(j, sc, causal):
        if causal:
            sc = jnp.where(kidx <= qidx, sc, -jnp.inf)
        bits = pltpu.bitcast(sc + 0.0, I32)
        keyt_sc[j] = bits ^ ((bits >> 31) & 0x7FFFFFFF)
        kb_sc[j] = pltpu.bitcast(bits & _HI16, F32).astype(BF16)

    def p1(u, c):
        sc = scores_t(2 * u, 2)
        store_keys(2 * u, sc[0:t], False)
        store_keys(2 * u + 1, sc[t:2 * t], False)
        return c

    lax.fori_loop(0, i // 2, p1, 0)

    @pl.when(i % 2 == 1)
    def _():
        sc = scores_t(i - 1, 2)
        store_keys(i - 1, sc[0:t], False)
        store_keys(i, sc[t:2 * t], True)

    @pl.when(i % 2 == 0)
    def _():
        store_keys(i, scores_t(i, 1), True)

    cacc = 32

    def count(pred):
        def body(j, acc):
            c = jnp.where(pred(keyt_sc[j], j), 1.0, 0.0)
            return acc + jnp.sum(c.reshape(t // cacc, cacc, t), axis=0)
        acc = lax.fori_loop(0, i + 1, body, jnp.zeros((cacc, t), F32))
        return jnp.sum(acc, axis=0, keepdims=True)

    kf = float(topk)

    def bisect(nchunks):
        def count_ge(cand):
            acc = jnp.zeros((cacc, t), F32)
            for j in range(nchunks):
                c = jnp.where(keyt_sc[j] >= cand, 1.0, 0.0)
                acc = acc + jnp.sum(c.reshape(t // cacc, cacc, t), axis=0)
            return jnp.sum(acc, axis=0, keepdims=True)

        one_b = jnp.ones((), BF16)
        zero_b = jnp.zeros((), BF16)

        def count_ge16(cand):
            cbits = (cand ^ ((cand >> 31) & 0x7FFFFFFF)) & _HI16
            cbits = jnp.where((cand > 0) & (cand < _MIN_NORMAL_BITS), _MIN_NORMAL_BITS, cbits)
            cf = lax.bitcast_convert_type(cbits, F32)
            cb = jnp.broadcast_to(cf, (16, t)).astype(BF16)
            cb = jnp.broadcast_to(cb[None], (t // 16, 16, t)).reshape(t, t)
            acc = jnp.zeros((cacc, t), BF16)
            for j in range(nchunks):
                c = jnp.where(kb_sc[j] >= cb, one_b, zero_b)
                for k in range(t // cacc):
                    acc = acc + c[k * cacc:(k + 1) * cacc]
            return jnp.sum(acc.astype(F32), axis=0, keepdims=True)

        cnt0 = count_ge16(jnp.zeros((1, t), I32))
        nonneg = cnt0 >= kf
        prefix0 = jnp.where(nonneg, 0, _INT_MIN).astype(I32)
        cge0 = jnp.where(nonneg, cnt0, float(nchunks * t))

        def bit_body(count_fn, bi, carry):
            prefix, cge = carry
            cand = prefix + lax.shift_left(jnp.int32(1), 30 - bi)
            cnt = count_fn(cand)
            take = cnt >= kf
            return jnp.where(take, cand, prefix), jnp.where(take, cnt, cge)

        carry = lax.fori_loop(0, 15, functools.partial(bit_body, count_ge16), (prefix0, cge0))
        thr, cge = lax.fori_loop(15, 31, functools.partial(bit_body, count_ge), carry)
        thr_sc[...] = jnp.broadcast_to(thr, thr_sc.shape)
        cge_sc[...] = jnp.broadcast_to(cge, cge_sc.shape)

    for ii in range(s_len // t):
        pl.when(i == ii)(functools.partial(bisect, ii + 1))
    thr = thr_sc[0:1, :]
    cnt_ge = cge_sc[0:1, :]
    excess = jnp.where((cnt_ge > kf) & (thr > _SENT_KEY), 1.0, 0.0)
    cut_sc[...] = jnp.full(cut_sc.shape, s_len, I32)

    @pl.when(jnp.max(excess) > 0.0)
    def _():
        nbits = max(1, (s_len - 1).bit_length())
        need = kf - count(lambda kc, j: kc > thr)

        def tie_body(bi, c):
            cand = c + lax.shift_left(jnp.int32(1), nbits - 1 - bi)
            cnt = count(lambda kc, j: (kc == thr) & (kidx + j * t < cand))
            return jnp.where(cnt < need, cand, c)

        c = lax.fori_loop(0, nbits, tie_body, jnp.zeros((1, t), I32))
        cut_sc[...] = jnp.broadcast_to(c, cut_sc.shape)

    cut = cut_sc[0:1, :]

    def to_mask(j, causal):
        kc = keyt_sc[j]
        sel = (kc > thr) | ((kc == thr) & (kidx + j * t <= cut))
        if causal:
            sel = sel & (kidx <= qidx)
        return jnp.where(sel, 0.0, NEG_BIG).T

    m_sc[...] = jnp.full(m_sc.shape, NEG_BIG, F32)
    acc_sc[...] = jnp.zeros(acc_sc.shape, F32)
    gt = DSA_GROUP * t
    near_delta = {"prev": nsub, "diag": 0}

    def attend(j0, parts):
        w = len(parts)
        start = pl.multiple_of(j0 * t, t)

        def qk(g):
            q = qa_ref[0, g * DSA_GROUP:(g + 1) * DSA_GROUP].reshape(gt, LANES)
            return _dot_nt(q, ka_ref[0, g, pl.ds(start, w * t), :])

        s_sc[0, :, 0:w * t] = qk(0)
        for pi, part in enumerate(parts):
            madd_sc[pi] = to_mask(j0 + pi, part == "diag")
        for g in range(DSA_KV_HEADS):
            if g + 1 < DSA_KV_HEADS:
                s_sc[(g + 1) % 2, :, 0:w * t] = qk(g + 1)
            slot = g % 2
            p_rows, a_rows = [], []
            for r in range(DSA_GROUP):
                hd = g * DSA_GROUP + r
                for rs in range(nsub):
                    r0 = r * t + rs * LANES
                    blocks = []
                    for cb in range(w * nsub):
                        part, cs = parts[cb // nsub], cb % nsub
                        blk = s_sc[slot, r0:r0 + LANES, cb * LANES:(cb + 1) * LANES] \
                            + madd_sc[cb // nsub, rs * LANES:(rs + 1) * LANES, cs * LANES:(cs + 1) * LANES]
                        if part in near_delta and near_delta[part] + rs - cs in (0, 1):
                            blk = blk + tb_ref[near_delta[part] + rs - cs, hd]
                        blocks.append(blk)
                    m_prev = m_sc[g, r0:r0 + LANES]
                    m_blk = blocks[0]
                    for blk in blocks[1:]:
                        m_blk = jnp.maximum(m_blk, blk)
                    m_new = jnp.maximum(m_prev, jnp.max(m_blk, axis=-1, keepdims=True))
                    a_rows.append(jnp.exp2(m_prev - m_new))
                    p_rows.append(jnp.concatenate([jnp.exp2(blk - m_new).astype(BF16) for blk in blocks], axis=1))
                    m_sc[g, r0:r0 + LANES] = m_new
            p = jnp.concatenate(p_rows, axis=0)
            a = jnp.concatenate(a_rows, axis=0)
            vc = v_ref[0, g, pl.ds(start, w * t), :]
            acc_sc[g] = a * acc_sc[g] + _dot(p, vc)

    @pl.when(i == 0)
    def _():
        attend(0, ("diag",))

    @pl.when(i >= 1)
    def _():
        nfar = i - 1

        def far_pair(u, c):
            attend(2 * u, ("far", "far"))
            return c

        lax.fori_loop(0, nfar // 2, far_pair, 0)

        @pl.when(nfar % 2 == 1)
        def _():
            attend(i - 2, ("far",))

        attend(i - 1, ("prev", "diag"))

    lane = lax.broadcasted_iota(I32, (t, LANES), 1)
    for g in range(DSA_KV_HEADS):
        acc = acc_sc[g]
        o = acc / acc[:, HALF:HALF + 1]
        for pr in range(DSA_GROUP // 2):
            lo = o[(2 * pr) * t:(2 * pr + 1) * t]
            hi = o[(2 * pr + 1) * t:(2 * pr + 2) * t]
            cblk = g * (DSA_GROUP // 2) + pr
            o_ref[0, :, cblk * LANES:(cblk + 1) * LANES] = \
                jnp.where(lane < HALF, lo, pltpu.roll(hi, HALF, 1)).astype(BF16)


def _dsa_attn(qa, ka, v, qi, kw, wf, tb, t, topk):
    b, _, s, _ = qa.shape
    nch = s // t
    gt = DSA_GROUP * t
    kern = functools.partial(_dsa_attn_kernel, t=t, topk=topk, s_len=s)
    return pl.pallas_call(
        kern, out_shape=jax.ShapeDtypeStruct((b, s, DSA_HEADS * DSA_HEAD_DIM), BF16),
        grid=(b, nch),
        in_specs=[pl.BlockSpec((1, DSA_HEADS, t, LANES), lambda bi, i: (bi, 0, i, 0)),
                  pl.BlockSpec((1, IDX_HEADS, t, LANES), lambda bi, i: (bi, 0, i, 0)),
                  pl.BlockSpec((1, t, LANES), lambda bi, i: (bi, i, 0)),
                  pl.BlockSpec((1, DSA_KV_HEADS, s, LANES), lambda bi, i: (bi, 0, 0, 0)),
                  pl.BlockSpec((1, DSA_KV_HEADS, s, LANES), lambda bi, i: (bi, 0, 0, 0)),
                  pl.BlockSpec((1, s, LANES), lambda bi, i: (bi, 0, 0)),
                  _const_spec((2, DSA_HEADS, LANES, LANES))],
        out_specs=pl.BlockSpec((1, t, DSA_HEADS * DSA_HEAD_DIM), lambda bi, i: (bi, i, 0)),
        scratch_shapes=[pltpu.VMEM((nch, t, t), I32),
                        pltpu.VMEM((nch, t, t), BF16),
                        pltpu.VMEM((2, t, t), F32),
                        pltpu.VMEM((2, gt, 2 * t), F32),
                        pltpu.VMEM((DSA_KV_HEADS, gt, LANES), F32),
                        pltpu.VMEM((DSA_KV_HEADS, gt, LANES), F32),
                        pltpu.VMEM((8, t), I32),
                        pltpu.VMEM((8, t), I32),
                        pltpu.VMEM((8, t), F32)],
        compiler_params=_params("parallel", "arbitrary"), name="dsa_attn",
    )(qa, qi, wf, ka, v, kw, tb)


def _mm_kernel(a_ref, w_ref, o_ref):
    o_ref[...] = _dot(a_ref[...].astype(BF16), w_ref[...]).astype(BF16)


def _mm(a, w, tm):
    m, k = a.shape
    n = w.shape[1]
    return pl.pallas_call(
        _mm_kernel, out_shape=jax.ShapeDtypeStruct((m, n), BF16), grid=(m // tm,),
        in_specs=[pl.BlockSpec((tm, k), lambda i: (i, 0)), _const_spec((k, n))],
        out_specs=pl.BlockSpec((tm, n), lambda i: (i, 0)),
        compiler_params=_params("parallel"), name="mem_kv",
    )(a, w)


def _post_kernel(a_ref, x_ref, wo_ref, wq_ref, kv_ref, wxo_ref, g_ref, b_ref, o_ref, oc_sc):
    y = _dot(a_ref[0], wo_ref[...])
    x1 = _layer_norm(ALPHA * x_ref[0] + y, g_ref[0:1, :], b_ref[0:1, :])
    q = (_dot(x1.astype(BF16), wq_ref[...]) * (XA_HEAD_DIM ** -0.5 * LOG2E)).astype(BF16)
    for hd in range(XA_HEADS):
        sl = slice(hd * XA_HEAD_DIM, (hd + 1) * XA_HEAD_DIM)
        sc = _dot_nt(q[:, sl], kv_ref[0, :, sl])
        m = jnp.max(sc, axis=-1, keepdims=True)
        p = jnp.exp2(sc - m)
        l = jnp.sum(p, axis=-1, keepdims=True)
        vsl = slice(D_MODEL + hd * XA_HEAD_DIM, D_MODEL + (hd + 1) * XA_HEAD_DIM)
        oc_sc[:, sl] = (_dot(p.astype(BF16), kv_ref[0, :, vsl]) / l).astype(BF16)
    y2 = _dot(oc_sc[...], wxo_ref[...])
    o_ref[0] = _layer_norm(ALPHA * x1 + y2, g_ref[1:2, :], b_ref[1:2, :])


def _post(a, x, w_o, w_q, kv, w_xo, g, bta, tm):
    b, s, _ = x.shape
    mlen = kv.shape[1]
    tok = lambda bi, i: (bi, i, 0)
    return pl.pallas_call(
        _post_kernel, out_shape=jax.ShapeDtypeStruct((b, s, D_MODEL), F32), grid=(b, s // tm),
        in_specs=[pl.BlockSpec((1, tm, D_MODEL), tok), pl.BlockSpec((1, tm, D_MODEL), tok),
                  _const_spec((D_MODEL, D_MODEL)), _const_spec((D_MODEL, D_MODEL)),
                  pl.BlockSpec((1, mlen, 2 * D_MODEL), lambda bi, i: (bi, 0, 0)),
                  _const_spec((D_MODEL, D_MODEL)), _const_spec((3, D_MODEL)), _const_spec((3, D_MODEL))],
        out_specs=pl.BlockSpec((1, tm, D_MODEL), tok),
        scratch_shapes=[pltpu.VMEM((tm, D_MODEL), BF16)],
        compiler_params=_params("parallel", "parallel"), name="post_attn",
    )(a, x, w_o, w_q, kv, w_xo, g, bta)


FFN_CHUNK = 256


def _ffn_kernel(x_ref, wi_ref, wd_ref, g_ref, b_ref, o_ref, act_sc):
    x = x_ref[...]
    xb = x.astype(BF16)
    for c in range(FFN_HIDDEN // FFN_CHUNK):
        sl = slice(c * FFN_CHUNK, (c + 1) * FFN_CHUNK)
        gte = _dot(xb, wi_ref[:, sl])
        up = _dot(xb, wi_ref[:, FFN_HIDDEN + c * FFN_CHUNK:FFN_HIDDEN + (c + 1) * FFN_CHUNK])
        act_sc[:, sl] = (gte * jax.nn.sigmoid(gte) * up).astype(BF16)
    y = _dot(act_sc[...], wd_ref[...])
    o_ref[...] = _layer_norm(ALPHA * x + y, g_ref[2:3, :], b_ref[2:3, :])


def _ffn(x2d, w_in, w_down, g, bta, tm):
    n = x2d.shape[0]
    return pl.pallas_call(
        _ffn_kernel, out_shape=jax.ShapeDtypeStruct((n, D_MODEL), F32), grid=(n // tm,),
        in_specs=[pl.BlockSpec((tm, D_MODEL), lambda i: (i, 0)),
                  _const_spec((D_MODEL, 2 * FFN_HIDDEN)), _const_spec((FFN_HIDDEN, D_MODEL)),
                  _const_spec((3, D_MODEL)), _const_spec((3, D_MODEL))],
        out_specs=pl.BlockSpec((tm, D_MODEL), lambda i: (i, 0)),
        scratch_shapes=[pltpu.VMEM((tm, FFN_HIDDEN), BF16)],
        compiler_params=_params("parallel"), name="ffn",
    )(x2d, w_in.astype(BF16), w_down.astype(BF16), g, bta)


def _tile(n, pref):
    t = min(n, pref)
    assert n % t == 0, (n, pref)
    return t


def kernel(x, mem, positions, rel_bias, mla_w_in, mla_q_norm, mla_w_uq, mla_kv_norm, mla_w_ukv, mla_w_o,
           dsa_w_in, dsa_w_o, xa_w_q, xa_w_kv, xa_w_o, ffn_w_in, ffn_w_down, ln_g, ln_b):
    b, s, d = x.shape
    assert d == D_MODEL and s % LANES == 0
    mlen = mem.shape[1]
    topk = min(TOPK_MAX, s // 4)
    t_attn = _tile(s, 256)
    tm_proj = _tile(s, 256)
    tm_post = _tile(s, 512)
    tm_ffn = _tile(b * s, 512)
    pos3 = positions.reshape(b, s, 1)
    tb = _t5_tiles(rel_bias)
    for i in range(DEPTH):
        j = i // 2
        if i % 2 == 0:
            q, k, v = _mla_proj(x, pos3, mla_w_in[j], mla_q_norm[j], mla_w_uq[j], mla_kv_norm[j],
                                mla_w_ukv[j], tm_proj)
            a = _flash_mla(q, k, v, _tile(s, 512))
            w_o = mla_w_o[j]
        else:
            qa, ka, v2, qi, kw, wf = _dsa_proj(x, dsa_w_in[j], rel_bias, tm_proj)
            a = _dsa_attn(qa, ka, v2, qi, kw, wf, tb, t_attn, topk)
            w_o = dsa_w_o[j]
        kv = _mm(mem.reshape(b * mlen, d), xa_w_kv[i].astype(BF16), _tile(b * mlen, 512)).reshape(b, mlen, 2 * d)
        x = _post(a, x, w_o.astype(BF16), xa_w_q[i].astype(BF16), kv, xa_w_o[i].astype(BF16),
                  ln_g[i], ln_b[i], tm_post)
        x = _ffn(x.reshape(b * s, d), ffn_w_in[i], ffn_w_down[i], ln_g[i], ln_b[i], tm_ffn).reshape(b, s, d)
    return x
```

```python
import functools
import math

import numpy as np
import jax
import jax.numpy as jnp
from jax import lax
from jax.experimental import pallas as pl
from jax.experimental.pallas import tpu as pltpu

D_MODEL = 1024
DEPTH = 2
MLA_HEADS = 16
MLA_Q_RANK = 512
MLA_KV_RANK = 256
MLA_NOPE = 64
MLA_ROPE = 32
MLA_V = 64
DSA_HEADS = 16
DSA_KV_HEADS = 4
DSA_GROUP = DSA_HEADS // DSA_KV_HEADS
DSA_HEAD_DIM = 64
IDX_HEADS = 8
IDX_DIM = 64
TOPK_MAX = 256
XA_HEADS = 4
XA_HEAD_DIM = D_MODEL // XA_HEADS
FFN_HIDDEN = -(-8 * D_MODEL // (3 * 256)) * 256
REL_BUCKETS = 32
REL_MAX_DIST = 128
ROPE_BASE = 10000.0
LN_EPS = 1e-5
RMS_EPS = 1e-6
ALPHA = (2 * DEPTH) ** 0.25

LANES = 128
HALF = LANES // 2
VMEM_LIMIT = 56 * 1024 * 1024

LOG2E = math.log2(math.e)
NEG_BIG = -1e30
F32 = jnp.float32
BF16 = jnp.bfloat16
I32 = jnp.int32
_NT = (((1,), (1,)), ((), ()))


def _dot(a, b):
    return jnp.dot(a, b, preferred_element_type=F32)


def _dot_nt(a, b):
    return lax.dot_general(a, b, _NT, preferred_element_type=F32)


def _layer_norm(y, g, b):
    mu = jnp.mean(y, axis=-1, keepdims=True)
    yc = y - mu
    var = jnp.mean(yc * yc, axis=-1, keepdims=True)
    return yc * lax.rsqrt(var + LN_EPS) * g + b


def _rms_norm(y, g):
    return y * lax.rsqrt(jnp.mean(y * y, axis=-1, keepdims=True) + RMS_EPS) * g


def _params(*sem):
    return pltpu.CompilerParams(dimension_semantics=sem, vmem_limit_bytes=VMEM_LIMIT)


def _const_spec(shape):
    nd = len(shape)
    return pl.BlockSpec(shape, lambda *_: (0,) * nd)


def _take_cols(w, idx):
    idx = np.asarray(idx)
    wp = jnp.concatenate([w, jnp.zeros((w.shape[0], 1), w.dtype)], axis=1)
    return wp[:, np.where(idx < 0, w.shape[1], idx)]


_R = MLA_ROPE // 2
_MLA_W = MLA_Q_RANK + MLA_KV_RANK


def _mla_proj_kernel(x_ref, pos_ref, win_ref, qg_ref, kvg_ref, wuq_ref, wuk_ref, wuv_ref,
                     freq_ref, sgn_ref, q_ref, k_ref, v_ref):
    xb = x_ref[0].astype(BF16)
    h = _dot(xb, win_ref[...])
    cq = _rms_norm(h[:, :MLA_Q_RANK], qg_ref[...]).astype(BF16)
    ckv = _rms_norm(h[:, MLA_Q_RANK:_MLA_W], kvg_ref[...]).astype(BF16)
    kr = h[:, _MLA_W:]
    ang = pos_ref[0].astype(F32) * freq_ref[...]
    cos = jnp.cos(ang)
    sin = jnp.sin(ang) * sgn_ref[...]

    def rope(t):
        return t * cos + pltpu.roll(t, HALF, 1) * sin

    kr = rope(kr)
    q_all = _dot(cq, wuq_ref[...])
    k_all = _dot(ckv, wuk_ref[...])
    v_all = _dot(ckv, wuv_ref[...])
    qscale = (MLA_NOPE + MLA_ROPE) ** -0.5 * LOG2E
    ones_hi = (lax.broadcasted_iota(I32, (1, LANES), 1) >= HALF).astype(F32)
    for hd in range(MLA_HEADS):
        sl = slice(hd * LANES, (hd + 1) * LANES)
        q_ref[0, hd] = (rope(q_all[:, sl]) * qscale).astype(BF16)
        k_ref[0, hd] = (k_all[:, sl] + kr).astype(BF16)
        v_ref[0, hd] = (v_all[:, sl] + ones_hi).astype(BF16)


def _mla_proj(x, pos3, w_in, q_norm, w_uq, kv_norm, w_ukv, tm):
    b, s, _ = x.shape
    hq = MLA_NOPE + MLA_ROPE
    in_idx = list(range(_MLA_W)) + [_MLA_W + j for j in range(_R)] + [-1] * (HALF - _R) \
        + [_MLA_W + _R + j for j in range(_R)] + [-1] * (HALF - _R)
    q_idx, k_idx, v_idx = [], [], []
    n_lo = HALF - _R
    n_hi = MLA_NOPE - n_lo
    tail = [-1] * (HALF - _R - n_hi)
    for hd in range(MLA_HEADS):
        qb = hd * hq
        q_idx += [qb + MLA_NOPE + j for j in range(_R)] + [qb + j for j in range(n_lo)] \
            + [qb + MLA_NOPE + _R + j for j in range(_R)] + [qb + n_lo + j for j in range(n_hi)] + tail
        kb = hd * (MLA_NOPE + MLA_V)
        k_idx += [-1] * _R + [kb + j for j in range(n_lo)] + [-1] * _R + [kb + n_lo + j for j in range(n_hi)] + tail
        v_idx += [kb + MLA_NOPE + j for j in range(MLA_V)] + [-1] * (LANES - MLA_V)
    win = _take_cols(w_in, in_idx).astype(BF16)
    wuq = _take_cols(w_uq, q_idx).astype(BF16)
    wuk = _take_cols(w_ukv, k_idx).astype(BF16)
    wuv = _take_cols(w_ukv, v_idx).astype(BF16)
    inv_freq = ROPE_BASE ** (-jnp.arange(0, MLA_ROPE, 2, dtype=F32) / MLA_ROPE)
    zpad = jnp.zeros((HALF - _R,), F32)
    freq = jnp.concatenate([inv_freq, zpad, inv_freq, zpad])[None]
    sgn = jnp.concatenate([-jnp.ones((_R,), F32), zpad, jnp.ones((_R,), F32), zpad])[None]
    nq = MLA_HEADS * LANES
    grid = (b, s // tm)
    out_shape = (jax.ShapeDtypeStruct((b, MLA_HEADS, s, LANES), BF16),
                 jax.ShapeDtypeStruct((b, MLA_HEADS, s, LANES), BF16),
                 jax.ShapeDtypeStruct((b, MLA_HEADS, s, LANES), BF16))
    return pl.pallas_call(
        _mla_proj_kernel, out_shape=out_shape, grid=grid,
        in_specs=[pl.BlockSpec((1, tm, D_MODEL), lambda bi, i: (bi, i, 0)),
                  pl.BlockSpec((1, tm, 1), lambda bi, i: (bi, i, 0)),
                  _const_spec((D_MODEL, _MLA_W + LANES)),
                  _const_spec((1, MLA_Q_RANK)), _const_spec((1, MLA_KV_RANK)),
                  _const_spec((MLA_Q_RANK, nq)), _const_spec((MLA_KV_RANK, nq)),
                  _const_spec((MLA_KV_RANK, nq)),
                  _const_spec((1, LANES)), _const_spec((1, LANES))],
        out_specs=(pl.BlockSpec((1, MLA_HEADS, tm, LANES), lambda bi, i: (bi, 0, i, 0)),
                   pl.BlockSpec((1, MLA_HEADS, tm, LANES), lambda bi, i: (bi, 0, i, 0)),
                   pl.BlockSpec((1, MLA_HEADS, tm, LANES), lambda bi, i: (bi, 0, i, 0))),
        compiler_params=_params("parallel", "parallel"), name="mla_proj",
    )(x, pos3, win, q_norm[None], kv_norm[None], wuq, wuk, wuv, freq, sgn)


def _flash_kernel(q_ref, k_ref, v_ref, o_ref, m_sc, acc_sc, s_sc, *, t, nh):
    i = pl.program_id(2)
    row = lax.broadcasted_iota(I32, (t, t), 0)
    col = lax.broadcasted_iota(I32, (t, t), 1)
    m_sc[...] = jnp.full(m_sc.shape, NEG_BIG, F32)
    acc_sc[...] = jnp.zeros(acc_sc.shape, F32)

    def scores(j, hh):
        kc = k_ref[0, hh, pl.ds(pl.multiple_of(j * t, t), t), :]
        return _dot_nt(q_ref[0, hh], kc)

    def step(j, slot, diag):
        for hh in range(nh):
            sc = s_sc[slot, hh]
            if diag:
                sc = jnp.where(col <= row, sc, NEG_BIG)
            else:
                s_sc[1 - slot, hh] = scores(j + 1, hh)
            m_prev = m_sc[hh]
            m_new = jnp.maximum(m_prev, jnp.max(sc, axis=-1, keepdims=True))
            a = jnp.exp2(m_prev - m_new)
            p = jnp.concatenate([jnp.exp2(sc[:, c * LANES:(c + 1) * LANES] - m_new)
                                 for c in range(t // LANES)], axis=1).astype(BF16)
            vc = v_ref[0, hh, pl.ds(pl.multiple_of(j * t, t), t), :]
            acc_sc[hh] = a * acc_sc[hh] + _dot(p, vc)
            m_sc[hh] = m_new

    for hh in range(nh):
        s_sc[0, hh] = scores(0, hh)

    def pair(j2, c):
        step(2 * j2, 0, False)
        step(2 * j2 + 1, 1, False)
        return c

    lax.fori_loop(0, i // 2, pair, 0)

    @pl.when(i % 2 == 1)
    def _():
        step(i - 1, 0, False)
        step(i, 1, True)

    @pl.when(i % 2 == 0)
    def _():
        step(i, 0, True)

    lane = lax.broadcasted_iota(I32, (t, LANES), 1)
    for pr in range(nh // 2):
        a0 = acc_sc[2 * pr]
        a1 = acc_sc[2 * pr + 1]
        o0 = a0 / a0[:, HALF:HALF + 1]
        o1 = a1 / a1[:, HALF:HALF + 1]
        o_ref[0, :, pr * LANES:(pr + 1) * LANES] = \
            jnp.where(lane < HALF, o0, pltpu.roll(o1, HALF, 1)).astype(BF16)


MLA_HEADS_PER_STEP = 4


def _flash_mla(q, k, v, t):
    b, h, s, _ = q.shape
    nh = MLA_HEADS_PER_STEP
    return pl.pallas_call(
        functools.partial(_flash_kernel, t=t, nh=nh),
        out_shape=jax.ShapeDtypeStruct((b, s, h * MLA_V), BF16),
        grid=(b, h // nh, s // t),
        in_specs=[pl.BlockSpec((1, nh, t, LANES), lambda bi, p, i: (bi, p, i, 0)),
                  pl.BlockSpec((1, nh, s, LANES), lambda bi, p, i: (bi, p, 0, 0)),
                  pl.BlockSpec((1, nh, s, LANES), lambda bi, p, i: (bi, p, 0, 0))],
        out_specs=pl.BlockSpec((1, t, nh * MLA_V), lambda bi, p, i: (bi, i, p)),
        scratch_shapes=[pltpu.VMEM((nh, t, LANES), F32), pltpu.VMEM((nh, t, LANES), F32),
                        pltpu.VMEM((2, nh, t, t), F32)],
        compiler_params=_params("parallel", "parallel", "arbitrary"), name="mla_flash",
    )(q, k, v)


_DQ0 = 0
_DK0 = DSA_HEADS * LANES
_DV0 = _DK0 + DSA_KV_HEADS * LANES
_DI0 = _DV0 + DSA_KV_HEADS * LANES
_DW0 = _DI0 + IDX_HEADS * LANES
_DN = _DW0 + LANES


def _dsa_proj_kernel(x_ref, w_ref, qhot_ref, krow_ref, wsc_ref, qa_ref, ka_ref, v2_ref, qi_ref, kw_ref, wf_ref):
    xb = x_ref[0].astype(BF16)
    h = _dot(xb, w_ref[...])
    qscale = DSA_HEAD_DIM ** -0.5 * LOG2E
    for hd in range(DSA_HEADS):
        sl = slice(_DQ0 + hd * LANES, _DQ0 + (hd + 1) * LANES)
        qa_ref[0, hd] = (h[:, sl] * qscale + qhot_ref[hd:hd + 1, :]).astype(BF16)
    ones_hi = (lax.broadcasted_iota(I32, (1, LANES), 1) >= HALF).astype(F32)
    for g in range(DSA_KV_HEADS):
        ka_ref[0, g] = (h[:, _DK0 + g * LANES:_DK0 + (g + 1) * LANES] + krow_ref[g:g + 1, :]).astype(BF16)
        v2_ref[0, g] = (h[:, _DV0 + g * LANES:_DV0 + (g + 1) * LANES] + ones_hi).astype(BF16)
    for hd in range(IDX_HEADS):
        qi_ref[0, hd] = (h[:, _DI0 + hd * LANES:_DI0 + (hd + 1) * LANES] * (IDX_DIM ** -0.5)).astype(BF16)
    kw = h[:, _DW0:]
    kw_ref[0] = kw.astype(BF16)
    wf_ref[0] = kw * wsc_ref[...]


def _dsa_proj(x, w_in, rel_bias, tm):
    b, s, _ = x.shape
    o1 = DSA_HEADS * DSA_HEAD_DIM
    o2 = o1 + DSA_KV_HEADS * DSA_HEAD_DIM
    o3 = o2 + DSA_KV_HEADS * DSA_HEAD_DIM
    o4 = o3 + IDX_HEADS * IDX_DIM
    o5 = o4 + IDX_DIM
    idx = []
    for hd in range(DSA_HEADS):
        idx += [hd * DSA_HEAD_DIM + j for j in range(DSA_HEAD_DIM)] + [-1] * HALF
    for g in range(DSA_KV_HEADS):
        idx += [o1 + g * DSA_HEAD_DIM + j for j in range(DSA_HEAD_DIM)] + [-1] * HALF
    for g in range(DSA_KV_HEADS):
        idx += [o2 + g * DSA_HEAD_DIM + j for j in range(DSA_HEAD_DIM)] + [-1] * HALF
    for hd in range(IDX_HEADS):
        idx += [o3 + hd * IDX_DIM + j for j in range(IDX_DIM)] + [-1] * HALF
    idx += [o4 + j for j in range(IDX_DIM)] + [o5 + j for j in range(IDX_HEADS)] + [-1] * (HALF - IDX_HEADS)
    assert len(idx) == _DN
    w = _take_cols(w_in, idx).astype(BF16)
    cfar = rel_bias[REL_BUCKETS - 1] * LOG2E
    chi = cfar.astype(BF16).astype(F32)
    clo = (cfar - chi).astype(BF16).astype(F32)
    qhot = np.zeros((DSA_HEADS, LANES), np.float32)
    for hd in range(DSA_HEADS):
        qhot[hd, HALF + hd % DSA_GROUP] = 1.0
        qhot[hd, HALF + DSA_GROUP + hd % DSA_GROUP] = 1.0
    zpad = jnp.zeros((DSA_KV_HEADS, LANES - HALF - 2 * DSA_GROUP), F32)
    krow = jnp.concatenate([jnp.zeros((DSA_KV_HEADS, HALF), F32), chi.reshape(DSA_KV_HEADS, DSA_GROUP),
                            clo.reshape(DSA_KV_HEADS, DSA_GROUP), zpad], axis=1)
    wsc = np.zeros((1, LANES), np.float32)
    wsc[0, HALF:HALF + IDX_HEADS] = IDX_HEADS ** -0.5
    hspec = lambda n: pl.BlockSpec((1, n, tm, LANES), lambda bi, i: (bi, 0, i, 0))
    tspec = pl.BlockSpec((1, tm, LANES), lambda bi, i: (bi, i, 0))
    out_shape = (jax.ShapeDtypeStruct((b, DSA_HEADS, s, LANES), BF16),
                 jax.ShapeDtypeStruct((b, DSA_KV_HEADS, s, LANES), BF16),
                 jax.ShapeDtypeStruct((b, DSA_KV_HEADS, s, LANES), BF16),
                 jax.ShapeDtypeStruct((b, IDX_HEADS, s, LANES), BF16),
                 jax.ShapeDtypeStruct((b, s, LANES), BF16),
                 jax.ShapeDtypeStruct((b, s, LANES), F32))
    return pl.pallas_call(
        _dsa_proj_kernel, out_shape=out_shape, grid=(b, s // tm),
        in_specs=[pl.BlockSpec((1, tm, D_MODEL), lambda bi, i: (bi, i, 0)),
                  _const_spec((D_MODEL, _DN)), _const_spec((DSA_HEADS, LANES)),
                  _const_spec((DSA_KV_HEADS, LANES)), _const_spec((1, LANES))],
        out_specs=(hspec(DSA_HEADS), hspec(DSA_KV_HEADS), hspec(DSA_KV_HEADS), hspec(IDX_HEADS), tspec, tspec),
        compiler_params=_params("parallel", "parallel"), name="dsa_proj",
    )(x, w, jnp.asarray(qhot), krow, jnp.asarray(wsc))


def _t5_tiles_kernel(rb_ref, o_ref):
    d = pl.program_id(0)
    hd = pl.program_id(1)
    row = lax.broadcasted_iota(I32, (LANES, LANES), 0)
    col = lax.broadcasted_iota(I32, (LANES, LANES), 1)
    n = jnp.maximum(d * LANES + row - col, 0)
    max_exact = REL_BUCKETS // 2
    nf = jnp.maximum(n, 1).astype(F32)
    large = max_exact + (jnp.log(nf / max_exact) / math.log(REL_MAX_DIST / max_exact)
                         * (REL_BUCKETS - max_exact)).astype(I32)
    large = jnp.minimum(large, REL_BUCKETS - 1)
    bucket = jnp.where(n < max_exact, n, large)
    far = rb_ref[REL_BUCKETS - 1, hd]
    acc = jnp.zeros((LANES, LANES), F32)
    for bk in range(REL_BUCKETS - 1):
        acc = jnp.where(bucket == bk, rb_ref[bk, hd] - far, acc)
    o_ref[0, 0] = acc * LOG2E


def _t5_tiles(rel_bias):
    return pl.pallas_call(
        _t5_tiles_kernel,
        out_shape=jax.ShapeDtypeStruct((2, DSA_HEADS, LANES, LANES), F32),
        grid=(2, DSA_HEADS),
        in_specs=[pl.BlockSpec(memory_space=pltpu.SMEM)],
        out_specs=pl.BlockSpec((1, 1, LANES, LANES), lambda d, h: (d, h, 0, 0)),
        compiler_params=_params("parallel", "parallel"), name="t5_tiles",
    )(rel_bias)


_SENT_KEY = int(np.int32(np.uint32(0xFF800000) ^ np.uint32(0x7FFFFFFF)))
_HI16 = -65536
_MIN_NORMAL_BITS = 0x00800000
_INT_MIN = -(2 ** 31)


def _dsa_attn_kernel(qa_ref, qi_ref, wf_ref, ka_ref, v_ref, kw_ref, tb_ref, o_ref,
                     keyt_sc, kb_sc, madd_sc, s_sc, m_sc, acc_sc, cut_sc, thr_sc, cge_sc, *, t, topk, s_len):
    i = pl.program_id(1)
    nsub = t // LANES
    kidx = lax.broadcasted_iota(I32, (t, t), 0)
    qidx = lax.broadcasted_iota(I32, (t, t), 1)

    qi = qi_ref[0].reshape(IDX_HEADS * t, LANES)
    wt = wf_ref[0].T
    wrows = [wt[HALF + hd:HALF + hd + 1, :] for hd in range(IDX_HEADS)]

    def scores_t(j0, w):
        kc = kw_ref[0, pl.ds(pl.multiple_of(j0 * t, t), w * t), :]
        d = _dot_nt(kc, qi)
        sc = wrows[0] * jnp.maximum(d[:, 0:t], 0.0)
        for hd in range(1, IDX_HEADS):
            sc = sc + wrows[hd] * jnp.maximum(d[:, hd * t:(hd + 1) * t], 0.0)
        return sc

    def store_keys(j, sc, causal):
        if causal:
            sc = jnp.where(kidx <= qidx, sc, -jnp.inf)
        bits = pltpu.bitcast(sc + 0.0, I32)
        keyt_sc[j] = bits ^ ((bits >> 31) & 0x7FFFFFFF)
        kb_sc[j] = pltpu.bitcast(bits & _HI16, F32).astype(BF16)

    def p1(u, c):
        sc = scores_t(2 * u, 2)
        store_keys(2 * u, sc[0:t], False)
        store_keys(2 * u + 1, sc[t:2 * t], False)
        return c

    lax.fori_loop(0, i // 2, p1, 0)

    @pl.when(i % 2 == 1)
    def _():
        sc = scores_t(i - 1, 2)
        store_keys(i - 1, sc[0:t], False)
        store_keys(i, sc[t:2 * t], True)

    @pl.when(i % 2 == 0)
    def _():
        store_keys(i, scores_t(i, 1), True)

    cacc = 32

    def count(pred):
        def body(j, acc):
            c = jnp.where(pred(keyt_sc[j], j), 1.0, 0.0)
            return acc + jnp.sum(c.reshape(t // cacc, cacc, t), axis=0)
        acc = lax.fori_loop(0, i + 1, body, jnp.zeros((cacc, t), F32))
        return jnp.sum(acc, axis=0, keepdims=True)

    kf = float(topk)

    def bisect(nchunks):
        def count_ge(cand):
            acc = jnp.zeros((cacc, t), F32)
            for j in range(nchunks):
                c = jnp.where(keyt_sc[j] >= cand, 1.0, 0.0)
                acc = acc + jnp.sum(c.reshape(t // cacc, cacc, t), axis=0)
            return jnp.sum(acc, axis=0, keepdims=True)

        one_b = jnp.ones((), BF16)
        zero_b = jnp.zeros((), BF16)

        def count_ge16(cand):
            cbits = (cand ^ ((cand >> 31) & 0x7FFFFFFF)) & _HI16
            cbits = jnp.where((cand > 0) & (cand < _MIN_NORMAL_BITS), _MIN_NORMAL_BITS, cbits)
            cf = lax.bitcast_convert_type(cbits, F32)
            cb = jnp.broadcast_to(cf, (16, t)).astype(BF16)
            cb = jnp.broadcast_to(cb[None], (t // 16, 16, t)).reshape(t, t)
            acc = jnp.zeros((cacc, t), BF16)
            for j in range(nchunks):
                c = jnp.where(kb_sc[j] >= cb, one_b, zero_b)
                for k in range(t // cacc):
                    acc = acc + c[k * cacc:(k + 1) * cacc]
            return jnp.sum(acc.astype(F32), axis=0, keepdims=True)

        cnt0 = count_ge16(jnp.zeros((1, t), I32))
        nonneg = cnt0 >= kf
        prefix0 = jnp.where(nonneg, 0, _INT_MIN).astype(I32)
        cge0 = jnp.where(nonneg, cnt0, float(nchunks * t))

        def bit_body(count_fn, bi, carry):
            prefix, cge = carry
            cand = prefix + lax.shift_left(jnp.int32(1), 30 - bi)
            cnt = count_fn(cand)
            take = cnt >= kf
            return jnp.where(take, cand, prefix), jnp.where(take, cnt, cge)

        carry = lax.fori_loop(0, 15, functools.partial(bit_body, count_ge16), (prefix0, cge0))
        thr, cge = lax.fori_loop(15, 31, functools.partial(bit_body, count_ge), carry)
        thr_sc[...] = jnp.broadcast_to(thr, thr_sc.shape)
        cge_sc[...] = jnp.broadcast_to(cge, cge_sc.shape)

    for ii in range(s_len // t):
        pl.when(i == ii)(functools.partial(bisect, ii + 1))
    thr = thr_sc[0:1, :]
    cnt_ge = cge_sc[0:1, :]
    excess = jnp.where((cnt_ge > kf) & (thr > _SENT_KEY), 1.0, 0.0)
    cut_sc[...] = jnp.full(cut_sc.shape, s_len, I32)

    @pl.when(jnp.max(excess) > 0.0)
    def _():
        nbits = max(1, (s_len - 1).bit_length())
        need = kf - count(lambda kc, j: kc > thr)

        def tie_body(bi, c):
            cand = c + lax.shift_left(jnp.int32(1), nbits - 1 - bi)
            cnt = count(lambda kc, j: (kc == thr) & (kidx + j * t < cand))
            return jnp.where(cnt < need, cand, c)

        c = lax.fori_loop(0, nbits, tie_body, jnp.zeros((1, t), I32))
        cut_sc[...] = jnp.broadcast_to(c, cut_sc.shape)

    cut = cut_sc[0:1, :]

    def to_mask(j, causal):
        kc = keyt_sc[j]
        sel = (kc > thr) | ((kc == thr) & (kidx + j * t <= cut))
        if causal:
            sel = sel & (kidx <= qidx)
        return jnp.where(sel, 0.0, NEG_BIG).T

    m_sc[...] = jnp.full(m_sc.shape, NEG_BIG, F32)
    acc_sc[...] = jnp.zeros(acc_sc.shape, F32)
    gt = DSA_GROUP * t
    near_delta = {"prev": nsub, "diag": 0}

    def attend(j0, parts):
        w = len(parts)
        start = pl.multiple_of(j0 * t, t)

        def qk(g):
            q = qa_ref[0, g * DSA_GROUP:(g + 1) * DSA_GROUP].reshape(gt, LANES)
            return _dot_nt(q, ka_ref[0, g, pl.ds(start, w * t), :])

        s_sc[0, :, 0:w * t] = qk(0)
        for pi, part in enumerate(parts):
            madd_sc[pi] = to_mask(j0 + pi, part == "diag")
        for g in range(DSA_KV_HEADS):
            if g + 1 < DSA_KV_HEADS:
                s_sc[(g + 1) % 2, :, 0:w * t] = qk(g + 1)
            slot = g % 2
            p_rows, a_rows = [], []
            for r in range(DSA_GROUP):
                hd = g * DSA_GROUP + r
                for rs in range(nsub):
                    r0 = r * t + rs * LANES
                    blocks = []
                    for cb in range(w * nsub):
                        part, cs = parts[cb // nsub], cb % nsub
                        blk = s_sc[slot, r0:r0 + LANES, cb * LANES:(cb + 1) * LANES] \
                            + madd_sc[cb // nsub, rs * LANES:(rs + 1) * LANES, cs * LANES:(cs + 1) * LANES]
                        if part in near_delta and near_delta[part] + rs - cs in (0, 1):
                            blk = blk + tb_ref[near_delta[part] + rs - cs, hd]
                        blocks.append(blk)
                    m_prev = m_sc[g, r0:r0 + LANES]
                    m_blk = blocks[0]
                    for blk in blocks[1:]:
                        m_blk = jnp.maximum(m_blk, blk)
                    m_new = jnp.maximum(m_prev, jnp.max(m_blk, axis=-1, keepdims=True))
                    a_rows.append(jnp.exp2(m_prev - m_new))
                    p_rows.append(jnp.concatenate([jnp.exp2(blk - m_new).astype(BF16) for blk in blocks], axis=1))
                    m_sc[g, r0:r0 + LANES] = m_new
            p = jnp.concatenate(p_rows, axis=0)
            a = jnp.concatenate(a_rows, axis=0)
            vc = v_ref[0, g, pl.ds(start, w * t), :]
            acc_sc[g] = a * acc_sc[g] + _dot(p, vc)

    @pl.when(i == 0)
    def _():
        attend(0, ("diag",))

    @pl.when(i >= 1)
    def _():
        nfar = i - 1

        def far_pair(u, c):
            attend(2 * u, ("far", "far"))
            return c

        lax.fori_loop(0, nfar // 2, far_pair, 0)

        @pl.when(nfar % 2 == 1)
        def _():
            attend(i - 2, ("far",))

        attend(i - 1, ("prev", "diag"))

    lane = lax.broadcasted_iota(I32, (t, LANES), 1)
    for g in range(DSA_KV_HEADS):
        acc = acc_sc[g]
        o = acc / acc[:, HALF:HALF + 1]
        for pr in range(DSA_GROUP // 2):
            lo = o[(2 * pr) * t:(2 * pr + 1) * t]
            hi = o[(2 * pr + 1) * t:(2 * pr + 2) * t]
            cblk = g * (DSA_GROUP // 2) + pr
            o_ref[0, :, cblk * LANES:(cblk + 1) * LANES] = \
                jnp.where(lane < HALF, lo, pltpu.roll(hi, HALF, 1)).astype(BF16)


def _dsa_attn(qa, ka, v, qi, kw, wf, tb, t, topk):
    b, _, s, _ = qa.shape
    nch = s // t
    gt = DSA_GROUP * t
    kern = functools.partial(_dsa_attn_kernel, t=t, topk=topk, s_len=s)
    return pl.pallas_call(
        kern, out_shape=jax.ShapeDtypeStruct((b, s, DSA_HEADS * DSA_HEAD_DIM), BF16),
        grid=(b, nch),
        in_specs=[pl.BlockSpec((1, DSA_HEADS, t, LANES), lambda bi, i: (bi, 0, i, 0)),
                  pl.BlockSpec((1, IDX_HEADS, t, LANES), lambda bi, i: (bi, 0, i, 0)),
                  pl.BlockSpec((1, t, LANES), lambda bi, i: (bi, i, 0)),
                  pl.BlockSpec((1, DSA_KV_HEADS, s, LANES), lambda bi, i: (bi, 0, 0, 0)),
                  pl.BlockSpec((1, DSA_KV_HEADS, s, LANES), lambda bi, i: (bi, 0, 0, 0)),
                  pl.BlockSpec((1, s, LANES), lambda bi, i: (bi, 0, 0)),
                  _const_spec((2, DSA_HEADS, LANES, LANES))],
        out_specs=pl.BlockSpec((1, t, DSA_HEADS * DSA_HEAD_DIM), lambda bi, i: (bi, i, 0)),
        scratch_shapes=[pltpu.VMEM((nch, t, t), I32),
                        pltpu.VMEM((nch, t, t), BF16),
                        pltpu.VMEM((2, t, t), F32),
                        pltpu.VMEM((2, gt, 2 * t), F32),
                        pltpu.VMEM((DSA_KV_HEADS, gt, LANES), F32),
                        pltpu.VMEM((DSA_KV_HEADS, gt, LANES), F32),
                        pltpu.VMEM((8, t), I32),
                        pltpu.VMEM((8, t), I32),
                        pltpu.VMEM((8, t), F32)],
        compiler_params=_params("parallel", "arbitrary"), name="dsa_attn",
    )(qa, qi, wf, ka, v, kw, tb)


def _mm_kernel(a_ref, w_ref, o_ref):
    o_ref[...] = _dot(a_ref[...].astype(BF16), w_ref[...]).astype(BF16)


def _mm(a, w, tm):
    m, k = a.shape
    n = w.shape[1]
    return pl.pallas_call(
        _mm_kernel, out_shape=jax.ShapeDtypeStruct((m, n), BF16), grid=(m // tm,),
        in_specs=[pl.BlockSpec((tm, k), lambda i: (i, 0)), _const_spec((k, n))],
        out_specs=pl.BlockSpec((tm, n), lambda i: (i, 0)),
        compiler_params=_params("parallel"), name="mem_kv",
    )(a, w)


def _post_kernel(a_ref, x_ref, wo_ref, wq_ref, kv_ref, wxo_ref, g_ref, b_ref, o_ref, oc_sc):
    y = _dot(a_ref[0], wo_ref[...])
    x1 = _layer_norm(ALPHA * x_ref[0] + y, g_ref[0:1, :], b_ref[0:1, :])
    q = (_dot(x1.astype(BF16), wq_ref[...]) * (XA_HEAD_DIM ** -0.5 * LOG2E)).astype(BF16)
    for hd in range(XA_HEADS):
        sl = slice(hd * XA_HEAD_DIM, (hd + 1) * XA_HEAD_DIM)
        sc = _dot_nt(q[:, sl], kv_ref[0, :, sl])
        m = jnp.max(sc, axis=-1, keepdims=True)
        p = jnp.exp2(sc - m)
        l = jnp.sum(p, axis=-1, keepdims=True)
        vsl = slice(D_MODEL + hd * XA_HEAD_DIM, D_MODEL + (hd + 1) * XA_HEAD_DIM)
        oc_sc[:, sl] = (_dot(p.astype(BF16), kv_ref[0, :, vsl]) / l).astype(BF16)
    y2 = _dot(oc_sc[...], wxo_ref[...])
    o_ref[0] = _layer_norm(ALPHA * x1 + y2, g_ref[1:2, :], b_ref[1:2, :])


def _post(a, x, w_o, w_q, kv, w_xo, g, bta, tm):
    b, s, _ = x.shape
    mlen = kv.shape[1]
    tok = lambda bi, i: (bi, i, 0)
    return pl.pallas_call(
        _post_kernel, out_shape=jax.ShapeDtypeStruct((b, s, D_MODEL), F32), grid=(b, s // tm),
        in_specs=[pl.BlockSpec((1, tm, D_MODEL), tok), pl.BlockSpec((1, tm, D_MODEL), tok),
                  _const_spec((D_MODEL, D_MODEL)), _const_spec((D_MODEL, D_MODEL)),
                  pl.BlockSpec((1, mlen, 2 * D_MODEL), lambda bi, i: (bi, 0, 0)),
                  _const_spec((D_MODEL, D_MODEL)), _const_spec((3, D_MODEL)), _const_spec((3, D_MODEL))],
        out_specs=pl.BlockSpec((1, tm, D_MODEL), tok),
        scratch_shapes=[pltpu.VMEM((tm, D_MODEL), BF16)],
        compiler_params=_params("parallel", "parallel"), name="post_attn",
    )(a, x, w_o, w_q, kv, w_xo, g, bta)


FFN_CHUNK = 256


def _ffn_kernel(x_ref, wi_ref, wd_ref, g_ref, b_ref, o_ref, act_sc):
    x = x_ref[...]
    xb = x.astype(BF16)
    for c in range(FFN_HIDDEN // FFN_CHUNK):
        sl = slice(c * FFN_CHUNK, (c + 1) * FFN_CHUNK)
        gte = _dot(xb, wi_ref[:, sl])
        up = _dot(xb, wi_ref[:, FFN_HIDDEN + c * FFN_CHUNK:FFN_HIDDEN + (c + 1) * FFN_CHUNK])
        act_sc[:, sl] = (gte * jax.nn.sigmoid(gte) * up).astype(BF16)
    y = _dot(act_sc[...], wd_ref[...])
    o_ref[...] = _layer_norm(ALPHA * x + y, g_ref[2:3, :], b_ref[2:3, :])


def _ffn(x2d, w_in, w_down, g, bta, tm):
    n = x2d.shape[0]
    return pl.pallas_call(
        _ffn_kernel, out_shape=jax.ShapeDtypeStruct((n, D_MODEL), F32), grid=(n // tm,),
        in_specs=[pl.BlockSpec((tm, D_MODEL), lambda i: (i, 0)),
                  _const_spec((D_MODEL, 2 * FFN_HIDDEN)), _const_spec((FFN_HIDDEN, D_MODEL)),
                  _const_spec((3, D_MODEL)), _const_spec((3, D_MODEL))],
        out_specs=pl.BlockSpec((tm, D_MODEL), lambda i: (i, 0)),
        scratch_shapes=[pltpu.VMEM((tm, FFN_HIDDEN), BF16)],
        compiler_params=_params("parallel"), name="ffn",
    )(x2d, w_in.astype(BF16), w_down.astype(BF16), g, bta)


def _tile(n, pref):
    t = min(n, pref)
    assert n % t == 0, (n, pref)
    return t


def kernel(x, mem, positions, rel_bias, mla_w_in, mla_q_norm, mla_w_uq, mla_kv_norm, mla_w_ukv, mla_w_o,
           dsa_w_in, dsa_w_o, xa_w_q, xa_w_kv, xa_w_o, ffn_w_in, ffn_w_down, ln_g, ln_b):
    b, s, d = x.shape
    assert d == D_MODEL and s % LANES == 0
    mlen = mem.shape[1]
    topk = min(TOPK_MAX, s // 4)
    t_attn = _tile(s, 256)
    tm_proj = _tile(s, 512)
    tm_post = _tile(s, 1024)
    tm_ffn = _tile(b * s, 1024)
    pos3 = positions.reshape(b, s, 1)
    tb = _t5_tiles(rel_bias)
    for i in range(DEPTH):
        j = i // 2
        if i % 2 == 0:
            q, k, v = _mla_proj(x, pos3, mla_w_in[j], mla_q_norm[j], mla_w_uq[j], mla_kv_norm[j],
                                mla_w_ukv[j], tm_proj)
            a = _flash_mla(q, k, v, _tile(s, 512))
            w_o = mla_w_o[j]
        else:
            qa, ka, v2, qi, kw, wf = _dsa_proj(x, dsa_w_in[j], rel_bias, tm_proj)
            a = _dsa_attn(qa, ka, v2, qi, kw, wf, tb, t_attn, topk)
            w_o = dsa_w_o[j]
        kv = _mm(mem.reshape(b * mlen, d), xa_w_kv[i].astype(BF16), _tile(b * mlen, 512)).reshape(b, mlen, 2 * d)
        x = _post(a, x, w_o.astype(BF16), xa_w_q[i].astype(BF16), kv, xa_w_o[i].astype(BF16),
                  ln_g[i], ln_b[i], tm_post)
        x = _ffn(x.reshape(b * s, d), ffn_w_in[i], ffn_w_down[i], ln_g[i], ln_b[i], tm_ffn).reshape(b, s, d)
    return x
```

```python
import functools
import math

import numpy as np
import jax
import jax.numpy as jnp
from jax import lax
from jax.experimental import pallas as pl
from jax.experimental.pallas import tpu as pltpu

D_MODEL = 1024
DEPTH = 2
MLA_HEADS = 16
MLA_Q_RANK = 512
MLA_KV_RANK = 256
MLA_NOPE = 64
MLA_ROPE = 32
MLA_V = 64
DSA_HEADS = 16
DSA_KV_HEADS = 4
DSA_GROUP = DSA_HEADS // DSA_KV_HEADS
DSA_HEAD_DIM = 64
IDX_HEADS = 8
IDX_DIM = 64
TOPK_MAX = 256
XA_HEADS = 4
XA_HEAD_DIM = D_MODEL // XA_HEADS
FFN_HIDDEN = -(-8 * D_MODEL // (3 * 256)) * 256
REL_BUCKETS = 32
REL_MAX_DIST = 128
ROPE_BASE = 10000.0
LN_EPS = 1e-5
RMS_EPS = 1e-6
ALPHA = (2 * DEPTH) ** 0.25

LANES = 128
HALF = LANES // 2
VMEM_LIMIT = 56 * 1024 * 1024

LOG2E = math.log2(math.e)
NEG_BIG = -1e30
F32 = jnp.float32
BF16 = jnp.bfloat16
I32 = jnp.int32
_NT = (((1,), (1,)), ((), ()))


def _dot(a, b):
    return jnp.dot(a, b, preferred_element_type=F32)


def _dot_nt(a, b):
    return lax.dot_general(a, b, _NT, preferred_element_type=F32)


def _layer_norm(y, g, b):
    mu = jnp.mean(y, axis=-1, keepdims=True)
    yc = y - mu
    var = jnp.mean(yc * yc, axis=-1, keepdims=True)
    return yc * lax.rsqrt(var + LN_EPS) * g + b


def _rms_norm(y, g):
    return y * lax.rsqrt(jnp.mean(y * y, axis=-1, keepdims=True) + RMS_EPS) * g


def _params(*sem):
    return pltpu.CompilerParams(dimension_semantics=sem, vmem_limit_bytes=VMEM_LIMIT)


def _const_spec(shape):
    nd = len(shape)
    return pl.BlockSpec(shape, lambda *_: (0,) * nd)


def _take_cols(w, idx):
    idx = np.asarray(idx)
    wp = jnp.concatenate([w, jnp.zeros((w.shape[0], 1), w.dtype)], axis=1)
    return wp[:, np.where(idx < 0, w.shape[1], idx)]


_R = MLA_ROPE // 2
_MLA_W = MLA_Q_RANK + MLA_KV_RANK


def _mla_proj_kernel(x_ref, pos_ref, win_ref, qg_ref, kvg_ref, wuq_ref, wuk_ref, wuv_ref,
                     freq_ref, sgn_ref, q_ref, k_ref, v_ref):
    xb = x_ref[0].astype(BF16)
    h = _dot(xb, win_ref[...])
    cq = _rms_norm(h[:, :MLA_Q_RANK], qg_ref[...]).astype(BF16)
    ckv = _rms_norm(h[:, MLA_Q_RANK:_MLA_W], kvg_ref[...]).astype(BF16)
    kr = h[:, _MLA_W:]
    ang = pos_ref[0].astype(F32) * freq_ref[...]
    cos = jnp.cos(ang)
    sin = jnp.sin(ang) * sgn_ref[...]

    def rope(t):
        return t * cos + pltpu.roll(t, HALF, 1) * sin

    kr = rope(kr)
    q_all = _dot(cq, wuq_ref[...])
    k_all = _dot(ckv, wuk_ref[...])
    v_all = _dot(ckv, wuv_ref[...])
    qscale = (MLA_NOPE + MLA_ROPE) ** -0.5 * LOG2E
    ones_hi = (lax.broadcasted_iota(I32, (1, LANES), 1) >= HALF).astype(F32)
    for hd in range(MLA_HEADS):
        sl = slice(hd * LANES, (hd + 1) * LANES)
        q_ref[0, hd] = (rope(q_all[:, sl]) * qscale).astype(BF16)
        k_ref[0, hd] = (k_all[:, sl] + kr).astype(BF16)
        v_ref[0, hd] = (v_all[:, sl] + ones_hi).astype(BF16)


def _mla_proj(x, pos3, w_in, q_norm, w_uq, kv_norm, w_ukv, tm):
    b, s, _ = x.shape
    hq = MLA_NOPE + MLA_ROPE
    in_idx = list(range(_MLA_W)) + [_MLA_W + j for j in range(_R)] + [-1] * (HALF - _R) \
        + [_MLA_W + _R + j for j in range(_R)] + [-1] * (HALF - _R)
    q_idx, k_idx, v_idx = [], [], []
    n_lo = HALF - _R
    n_hi = MLA_NOPE - n_lo
    tail = [-1] * (HALF - _R - n_hi)
    for hd in range(MLA_HEADS):
        qb = hd * hq
        q_idx += [qb + MLA_NOPE + j for j in range(_R)] + [qb + j for j in range(n_lo)] \
            + [qb + MLA_NOPE + _R + j for j in range(_R)] + [qb + n_lo + j for j in range(n_hi)] + tail
        kb = hd * (MLA_NOPE + MLA_V)
        k_idx += [-1] * _R + [kb + j for j in range(n_lo)] + [-1] * _R + [kb + n_lo + j for j in range(n_hi)] + tail
        v_idx += [kb + MLA_NOPE + j for j in range(MLA_V)] + [-1] * (LANES - MLA_V)
    win = _take_cols(w_in, in_idx).astype(BF16)
    wuq = _take_cols(w_uq, q_idx).astype(BF16)
    wuk = _take_cols(w_ukv, k_idx).astype(BF16)
    wuv = _take_cols(w_ukv, v_idx).astype(BF16)
    inv_freq = ROPE_BASE ** (-jnp.arange(0, MLA_ROPE, 2, dtype=F32) / MLA_ROPE)
    zpad = jnp.zeros((HALF - _R,), F32)
    freq = jnp.concatenate([inv_freq, zpad, inv_freq, zpad])[None]
    sgn = jnp.concatenate([-jnp.ones((_R,), F32), zpad, jnp.ones((_R,), F32), zpad])[None]
    nq = MLA_HEADS * LANES
    grid = (b, s // tm)
    out_shape = (jax.ShapeDtypeStruct((b, MLA_HEADS, s, LANES), BF16),
                 jax.ShapeDtypeStruct((b, MLA_HEADS, s, LANES), BF16),
                 jax.ShapeDtypeStruct((b, MLA_HEADS, s, LANES), BF16))
    return pl.pallas_call(
        _mla_proj_kernel, out_shape=out_shape, grid=grid,
        in_specs=[pl.BlockSpec((1, tm, D_MODEL), lambda bi, i: (bi, i, 0)),
                  pl.BlockSpec((1, tm, 1), lambda bi, i: (bi, i, 0)),
                  _const_spec((D_MODEL, _MLA_W + LANES)),
                  _const_spec((1, MLA_Q_RANK)), _const_spec((1, MLA_KV_RANK)),
                  _const_spec((MLA_Q_RANK, nq)), _const_spec((MLA_KV_RANK, nq)),
                  _const_spec((MLA_KV_RANK, nq)),
                  _const_spec((1, LANES)), _const_spec((1, LANES))],
        out_specs=(pl.BlockSpec((1, MLA_HEADS, tm, LANES), lambda bi, i: (bi, 0, i, 0)),
                   pl.BlockSpec((1, MLA_HEADS, tm, LANES), lambda bi, i: (bi, 0, i, 0)),
                   pl.BlockSpec((1, MLA_HEADS, tm, LANES), lambda bi, i: (bi, 0, i, 0))),
        compiler_params=_params("parallel", "parallel"), name="mla_proj",
    )(x, pos3, win, q_norm[None], kv_norm[None], wuq, wuk, wuv, freq, sgn)


def _flash_kernel(q_ref, k_ref, v_ref, o_ref, m_sc, acc_sc, s_sc, *, t, nh):
    i = pl.program_id(2)
    row = lax.broadcasted_iota(I32, (t, t), 0)
    col = lax.broadcasted_iota(I32, (t, t), 1)
    m_sc[...] = jnp.full(m_sc.shape, NEG_BIG, F32)
    acc_sc[...] = jnp.zeros(acc_sc.shape, F32)

    def scores(j, hh):
        kc = k_ref[0, hh, pl.ds(pl.multiple_of(j * t, t), t), :]
        return _dot_nt(q_ref[0, hh], kc)

    def step(j, slot, diag):
        for hh in range(nh):
            sc = s_sc[slot, hh]
            if diag:
                sc = jnp.where(col <= row, sc, NEG_BIG)
            else:
                s_sc[1 - slot, hh] = scores(j + 1, hh)
            m_prev = m_sc[hh]
            m_new = jnp.maximum(m_prev, jnp.max(sc, axis=-1, keepdims=True))
            a = jnp.exp2(m_prev - m_new)
            p = jnp.concatenate([jnp.exp2(sc[:, c * LANES:(c + 1) * LANES] - m_new)
                                 for c in range(t // LANES)], axis=1).astype(BF16)
            vc = v_ref[0, hh, pl.ds(pl.multiple_of(j * t, t), t), :]
            acc_sc[hh] = a * acc_sc[hh] + _dot(p, vc)
            m_sc[hh] = m_new

    for hh in range(nh):
        s_sc[0, hh] = scores(0, hh)

    def pair(j2, c):
        step(2 * j2, 0, False)
        step(2 * j2 + 1, 1, False)
        return c

    lax.fori_loop(0, i // 2, pair, 0)

    @pl.when(i % 2 == 1)
    def _():
        step(i - 1, 0, False)
        step(i, 1, True)

    @pl.when(i % 2 == 0)
    def _():
        step(i, 0, True)

    lane = lax.broadcasted_iota(I32, (t, LANES), 1)
    for pr in range(nh // 2):
        a0 = acc_sc[2 * pr]
        a1 = acc_sc[2 * pr + 1]
        o0 = a0 / a0[:, HALF:HALF + 1]
        o1 = a1 / a1[:, HALF:HALF + 1]
        o_ref[0, :, pr * LANES:(pr + 1) * LANES] = \
            jnp.where(lane < HALF, o0, pltpu.roll(o1, HALF, 1)).astype(BF16)


MLA_HEADS_PER_STEP = 4


def _flash_mla(q, k, v, t):
    b, h, s, _ = q.shape
    nh = MLA_HEADS_PER_STEP
    return pl.pallas_call(
        functools.partial(_flash_kernel, t=t, nh=nh),
        out_shape=jax.ShapeDtypeStruct((b, s, h * MLA_V), BF16),
        grid=(b, h // nh, s // t),
        in_specs=[pl.BlockSpec((1, nh, t, LANES), lambda bi, p, i: (bi, p, i, 0)),
                  pl.BlockSpec((1, nh, s, LANES), lambda bi, p, i: (bi, p, 0, 0)),
                  pl.BlockSpec((1, nh, s, LANES), lambda bi, p, i: (bi, p, 0, 0))],
        out_specs=pl.BlockSpec((1, t, nh * MLA_V), lambda bi, p, i: (bi, i, p)),
        scratch_shapes=[pltpu.VMEM((nh, t, LANES), F32), pltpu.VMEM((nh, t, LANES), F32),
                        pltpu.VMEM((2, nh, t, t), F32)],
        compiler_params=_params("parallel", "parallel", "arbitrary"), name="mla_flash",
    )(q, k, v)


_DQ0 = 0
_DK0 = DSA_HEADS * LANES
_DV0 = _DK0 + DSA_KV_HEADS * LANES
_DI0 = _DV0 + DSA_KV_HEADS * LANES
_DW0 = _DI0 + IDX_HEADS * LANES
_DN = _DW0 + LANES


def _dsa_proj_kernel(x_ref, w_ref, qhot_ref, krow_ref, wsc_ref, qa_ref, ka_ref, v2_ref, qi_ref, kw_ref, wf_ref):
    xb = x_ref[0].astype(BF16)
    h = _dot(xb, w_ref[...])
    qscale = DSA_HEAD_DIM ** -0.5 * LOG2E
    for hd in range(DSA_HEADS):
        sl = slice(_DQ0 + hd * LANES, _DQ0 + (hd + 1) * LANES)
        qa_ref[0, hd] = (h[:, sl] * qscale + qhot_ref[hd:hd + 1, :]).astype(BF16)
    ones_hi = (lax.broadcasted_iota(I32, (1, LANES), 1) >= HALF).astype(F32)
    for g in range(DSA_KV_HEADS):
        ka_ref[0, g] = (h[:, _DK0 + g * LANES:_DK0 + (g + 1) * LANES] + krow_ref[g:g + 1, :]).astype(BF16)
        v2_ref[0, g] = (h[:, _DV0 + g * LANES:_DV0 + (g + 1) * LANES] + ones_hi).astype(BF16)
    for hd in range(IDX_HEADS):
        qi_ref[0, hd] = (h[:, _DI0 + hd * LANES:_DI0 + (hd + 1) * LANES] * (IDX_DIM ** -0.5)).astype(BF16)
    kw = h[:, _DW0:]
    kw_ref[0] = kw.astype(BF16)
    wf_ref[0] = kw * wsc_ref[...]


def _dsa_proj(x, w_in, rel_bias, tm):
    b, s, _ = x.shape
    o1 = DSA_HEADS * DSA_HEAD_DIM
    o2 = o1 + DSA_KV_HEADS * DSA_HEAD_DIM
    o3 = o2 + DSA_KV_HEADS * DSA_HEAD_DIM
    o4 = o3 + IDX_HEADS * IDX_DIM
    o5 = o4 + IDX_DIM
    idx = []
    for hd in range(DSA_HEADS):
        idx += [hd * DSA_HEAD_DIM + j for j in range(DSA_HEAD_DIM)] + [-1] * HALF
    for g in range(DSA_KV_HEADS):
        idx += [o1 + g * DSA_HEAD_DIM + j for j in range(DSA_HEAD_DIM)] + [-1] * HALF
    for g in range(DSA_KV_HEADS):
        idx += [o2 + g * DSA_HEAD_DIM + j for j in range(DSA_HEAD_DIM)] + [-1] * HALF
    for hd in range(IDX_HEADS):
        idx += [o3 + hd * IDX_DIM + j for j in range(IDX_DIM)] + [-1] * HALF
    idx += [o4 + j for j in range(IDX_DIM)] + [o5 + j for j in range(IDX_HEADS)] + [-1] * (HALF - IDX_HEADS)
    assert len(idx) == _DN
    w = _take_cols(w_in, idx).astype(BF16)
    cfar = rel_bias[REL_BUCKETS - 1] * LOG2E
    chi = cfar.astype(BF16).astype(F32)
    clo = (cfar - chi).astype(BF16).astype(F32)
    qhot = np.zeros((DSA_HEADS, LANES), np.float32)
    for hd in range(DSA_HEADS):
        qhot[hd, HALF + hd % DSA_GROUP] = 1.0
        qhot[hd, HALF + DSA_GROUP + hd % DSA_GROUP] = 1.0
    zpad = jnp.zeros((DSA_KV_HEADS, LANES - HALF - 2 * DSA_GROUP), F32)
    krow = jnp.concatenate([jnp.zeros((DSA_KV_HEADS, HALF), F32), chi.reshape(DSA_KV_HEADS, DSA_GROUP),
                            clo.reshape(DSA_KV_HEADS, DSA_GROUP), zpad], axis=1)
    wsc = np.zeros((1, LANES), np.float32)
    wsc[0, HALF:HALF + IDX_HEADS] = IDX_HEADS ** -0.5
    hspec = lambda n: pl.BlockSpec((1, n, tm, LANES), lambda bi, i: (bi, 0, i, 0))
    tspec = pl.BlockSpec((1, tm, LANES), lambda bi, i: (bi, i, 0))
    out_shape = (jax.ShapeDtypeStruct((b, DSA_HEADS, s, LANES), BF16),
                 jax.ShapeDtypeStruct((b, DSA_KV_HEADS, s, LANES), BF16),
                 jax.ShapeDtypeStruct((b, DSA_KV_HEADS, s, LANES), BF16),
                 jax.ShapeDtypeStruct((b, IDX_HEADS, s, LANES), BF16),
                 jax.ShapeDtypeStruct((b, s, LANES), BF16),
                 jax.ShapeDtypeStruct((b, s, LANES), F32))
    return pl.pallas_call(
        _dsa_proj_kernel, out_shape=out_shape, grid=(b, s // tm),
        in_specs=[pl.BlockSpec((1, tm, D_MODEL), lambda bi, i: (bi, i, 0)),
                  _const_spec((D_MODEL, _DN)), _const_spec((DSA_HEADS, LANES)),
                  _const_spec((DSA_KV_HEADS, LANES)), _const_spec((1, LANES))],
        out_specs=(hspec(DSA_HEADS), hspec(DSA_KV_HEADS), hspec(DSA_KV_HEADS), hspec(IDX_HEADS), tspec, tspec),
        compiler_params=_params("parallel", "parallel"), name="dsa_proj",
    )(x, w, jnp.asarray(qhot), krow, jnp.asarray(wsc))


def _t5_tiles_kernel(rb_ref, o_ref):
    d = pl.program_id(0)
    hd = pl.program_id(1)
    row = lax.broadcasted_iota(I32, (LANES, LANES), 0)
    col = lax.broadcasted_iota(I32, (LANES, LANES), 1)
    n = jnp.maximum(d * LANES + row - col, 0)
    max_exact = REL_BUCKETS // 2
    nf = jnp.maximum(n, 1).astype(F32)
    large = max_exact + (jnp.log(nf / max_exact) / math.log(REL_MAX_DIST / max_exact)
                         * (REL_BUCKETS - max_exact)).astype(I32)
    large = jnp.minimum(large, REL_BUCKETS - 1)
    bucket = jnp.where(n < max_exact, n, large)
    far = rb_ref[REL_BUCKETS - 1, hd]
    acc = jnp.zeros((LANES, LANES), F32)
    for bk in range(REL_BUCKETS - 1):
        acc = jnp.where(bucket == bk, rb_ref[bk, hd] - far, acc)
    o_ref[0, 0] = acc * LOG2E


def _t5_tiles(rel_bias):
    return pl.pallas_call(
        _t5_tiles_kernel,
        out_shape=jax.ShapeDtypeStruct((2, DSA_HEADS, LANES, LANES), F32),
        grid=(2, DSA_HEADS),
        in_specs=[pl.BlockSpec(memory_space=pltpu.SMEM)],
        out_specs=pl.BlockSpec((1, 1, LANES, LANES), lambda d, h: (d, h, 0, 0)),
        compiler_params=_params("parallel", "parallel"), name="t5_tiles",
    )(rel_bias)


_SENT_KEY = int(np.int32(np.uint32(0xFF800000) ^ np.uint32(0x7FFFFFFF)))
_HI16 = -65536
_MIN_NORMAL_BITS = 0x00800000
_INT_MIN = -(2 ** 31)


def _dsa_attn_kernel(qa_ref, qi_ref, wf_ref, ka_ref, v_ref, kw_ref, tb_ref, o_ref,
                     keyt_sc, kb_sc, madd_sc, s_sc, m_sc, acc_sc, cut_sc, thr_sc, cge_sc, tcnt_sc,
                     *, t, topk, s_len):
    i = pl.program_id(1)
    nsub = t // LANES
    kidx = lax.broadcasted_iota(I32, (t, t), 0)
    qidx = lax.broadcasted_iota(I32, (t, t), 1)

    qi = qi_ref[0].reshape(IDX_HEADS * t, LANES)
    wt = wf_ref[0].T
    wrows = [wt[HALF + hd:HALF + hd + 1, :] for hd in range(IDX_HEADS)]

    def scores_t(j0, w):
        kc = kw_ref[0, pl.ds(pl.multiple_of(j0 * t, t), w * t), :]
        d = _dot_nt(kc, qi)
        sc = wrows[0] * jnp.maximum(d[:, 0:t], 0.0)
        for hd in range(1, IDX_HEADS):
            sc = sc + wrows[hd] * jnp.maximum(d[:, hd * t:(hd + 1) * t], 0.0)
        return sc

    def store_keys(j, sc, causal):
        if causal:
            sc = jnp.where(kidx <= qidx, sc, -jnp.inf)
        bits = pltpu.bitcast(sc + 0.0, I32)
        keyt_sc[j] = bits ^ ((bits >> 31) & 0x7FFFFFFF)
        kb_sc[j] = pltpu.bitcast(bits & _HI16, F32).astype(BF16)

    def p1(u, c):
        sc = scores_t(2 * u, 2)
        store_keys(2 * u, sc[0:t], False)
        store_keys(2 * u + 1, sc[t:2 * t], False)
        return c

    lax.fori_loop(0, i // 2, p1, 0)

    @pl.when(i % 2 == 1)
    def _():
        sc = scores_t(i - 1, 2)
        store_keys(i - 1, sc[0:t], False)
        store_keys(i, sc[t:2 * t], True)

    @pl.when(i % 2 == 0)
    def _():
        store_keys(i, scores_t(i, 1), True)

    cacc = 32

    kf = float(topk)

    def bisect(nchunks):
        def count_ge(cand):
            acc = jnp.zeros((cacc, t), F32)
            for j in range(nchunks):
                c = jnp.where(keyt_sc[j] >= cand, 1.0, 0.0)
                acc = acc + jnp.sum(c.reshape(t // cacc, cacc, t), axis=0)
            return jnp.sum(acc, axis=0, keepdims=True)

        one_b = jnp.ones((), BF16)
        zero_b = jnp.zeros((), BF16)

        def count_ge16(cand):
            cbits = (cand ^ ((cand >> 31) & 0x7FFFFFFF)) & _HI16
            cbits = jnp.where((cand > 0) & (cand < _MIN_NORMAL_BITS), _MIN_NORMAL_BITS, cbits)
            cf = lax.bitcast_convert_type(cbits, F32)
            cb = jnp.broadcast_to(cf, (16, t)).astype(BF16)
            cb = jnp.broadcast_to(cb[None], (t // 16, 16, t)).reshape(t, t)
            acc = jnp.zeros((cacc, t), BF16)
            for j in range(nchunks):
                c = jnp.where(kb_sc[j] >= cb, one_b, zero_b)
                for k in range(t // cacc):
                    acc = acc + c[k * cacc:(k + 1) * cacc]
            return jnp.sum(acc.astype(F32), axis=0, keepdims=True)

        cnt0 = count_ge16(jnp.zeros((1, t), I32))
        nonneg = cnt0 >= kf
        prefix0 = jnp.where(nonneg, 0, _INT_MIN).astype(I32)
        cge0 = jnp.where(nonneg, cnt0, float(nchunks * t))

        def bit_body(count_fn, bi, carry):
            prefix, cge = carry
            cand = prefix + lax.shift_left(jnp.int32(1), 30 - bi)
            cnt = count_fn(cand)
            take = cnt >= kf
            return jnp.where(take, cand, prefix), jnp.where(take, cnt, cge)

        carry = lax.fori_loop(0, 15, functools.partial(bit_body, count_ge16), (prefix0, cge0))
        thr, cge = lax.fori_loop(15, 31, functools.partial(bit_body, count_ge), carry)
        thr_sc[...] = jnp.broadcast_to(thr, thr_sc.shape)
        cge_sc[...] = jnp.broadcast_to(cge, cge_sc.shape)

    for ii in range(s_len // t):
        pl.when(i == ii)(functools.partial(bisect, ii + 1))
    thr = thr_sc[0:1, :]
    cnt_ge = cge_sc[0:1, :]
    excess = jnp.where((cnt_ge > kf) & (thr > _SENT_KEY), 1.0, 0.0)
    cut_sc[...] = jnp.full(cut_sc.shape, s_len, I32)

    @pl.when(jnp.max(excess) > 0.0)
    def _():
        def sweep(j, gt):
            kc = keyt_sc[j]
            tie = jnp.where(kc == thr, 1.0, 0.0)
            tcnt_sc[j] = jnp.broadcast_to(jnp.sum(tie, axis=0, keepdims=True), (8, t))
            g = jnp.where(kc > thr, 1.0, 0.0)
            return gt + jnp.sum(g.reshape(t // cacc, cacc, t), axis=0)

        gt = lax.fori_loop(0, i + 1, sweep, jnp.zeros((cacc, t), F32))
        need = kf - jnp.sum(gt, axis=0, keepdims=True)

        def scan(j, carry):
            run, cstar, base = carry
            new = run + tcnt_sc[j, 0:1, :]
            hit = (run < need) & (new >= need)
            return new, jnp.where(hit, j, cstar), jnp.where(hit, run, base)

        zrow = jnp.zeros((1, t), F32)
        _, cstar, base = lax.fori_loop(0, i + 1, scan, (zrow, jnp.zeros((1, t), I32), zrow))

        def gather(j, tile):
            return jnp.where(cstar == j, keyt_sc[j], tile)

        tile = lax.fori_loop(0, i + 1, gather, jnp.full((t, t), _SENT_KEY, I32))
        tie = jnp.where(tile == thr, 1.0, 0.0)
        need_in = need - base
        rbits = max(1, (t - 1).bit_length())

        def row_body(bi, r):
            cand = r + lax.shift_left(jnp.int32(1), rbits - 1 - bi)
            cnt = jnp.sum(jnp.where(kidx < cand, tie, 0.0), axis=0, keepdims=True)
            return jnp.where(cnt < need_in, cand, r)

        r = lax.fori_loop(0, rbits, row_body, jnp.zeros((1, t), I32))
        cut_sc[...] = jnp.broadcast_to(cstar * t + r, cut_sc.shape)

    cut = cut_sc[0:1, :]

    def to_mask(j, causal):
        kc = keyt_sc[j]
        sel = (kc > thr) | ((kc == thr) & (kidx + j * t <= cut))
        if causal:
            sel = sel & (kidx <= qidx)
        return jnp.where(sel, 0.0, NEG_BIG).T

    m_sc[...] = jnp.full(m_sc.shape, NEG_BIG, F32)
    acc_sc[...] = jnp.zeros(acc_sc.shape, F32)
    gt = DSA_GROUP * t
    near_delta = {"prev": nsub, "diag": 0}

    def attend(j0, parts):
        w = len(parts)
        start = pl.multiple_of(j0 * t, t)

        def qk(g):
            q = qa_ref[0, g * DSA_GROUP:(g + 1) * DSA_GROUP].reshape(gt, LANES)
            return _dot_nt(q, ka_ref[0, g, pl.ds(start, w * t), :])

        s_sc[0, :, 0:w * t] = qk(0)
        for pi, part in enumerate(parts):
            madd_sc[pi] = to_mask(j0 + pi, part == "diag")
        for g in range(DSA_KV_HEADS):
            if g + 1 < DSA_KV_HEADS:
                s_sc[(g + 1) % 2, :, 0:w * t] = qk(g + 1)
            slot = g % 2
            p_rows, a_rows = [], []
            for r in range(DSA_GROUP):
                hd = g * DSA_GROUP + r
                for rs in range(nsub):
                    r0 = r * t + rs * LANES
                    blocks = []
                    for cb in range(w * nsub):
                        part, cs = parts[cb // nsub], cb % nsub
                        blk = s_sc[slot, r0:r0 + LANES, cb * LANES:(cb + 1) * LANES] \
                            + madd_sc[cb // nsub, rs * LANES:(rs + 1) * LANES, cs * LANES:(cs + 1) * LANES]
                        if part in near_delta and near_delta[part] + rs - cs in (0, 1):
                            blk = blk + tb_ref[near_delta[part] + rs - cs, hd]
                        blocks.append(blk)
                    m_prev = m_sc[g, r0:r0 + LANES]
                    m_blk = blocks[0]
                    for blk in blocks[1:]:
                        m_blk = jnp.maximum(m_blk, blk)
                    m_new = jnp.maximum(m_prev, jnp.max(m_blk, axis=-1, keepdims=True))
                    a_rows.append(jnp.exp2(m_prev - m_new))
                    p_rows.append(jnp.concatenate([jnp.exp2(blk - m_new).astype(BF16) for blk in blocks], axis=1))
                    m_sc[g, r0:r0 + LANES] = m_new
            p = jnp.concatenate(p_rows, axis=0)
            a = jnp.concatenate(a_rows, axis=0)
            vc = v_ref[0, g, pl.ds(start, w * t), :]
            acc_sc[g] = a * acc_sc[g] + _dot(p, vc)

    @pl.when(i == 0)
    def _():
        attend(0, ("diag",))

    @pl.when(i >= 1)
    def _():
        nfar = i - 1

        def far_pair(u, c):
            attend(2 * u, ("far", "far"))
            return c

        lax.fori_loop(0, nfar // 2, far_pair, 0)

        @pl.when(nfar % 2 == 1)
        def _():
            attend(i - 2, ("far",))

        attend(i - 1, ("prev", "diag"))

    lane = lax.broadcasted_iota(I32, (t, LANES), 1)
    for g in range(DSA_KV_HEADS):
        acc = acc_sc[g]
        o = acc / acc[:, HALF:HALF + 1]
        for pr in range(DSA_GROUP // 2):
            lo = o[(2 * pr) * t:(2 * pr + 1) * t]
            hi = o[(2 * pr + 1) * t:(2 * pr + 2) * t]
            cblk = g * (DSA_GROUP // 2) + pr
            o_ref[0, :, cblk * LANES:(cblk + 1) * LANES] = \
                jnp.where(lane < HALF, lo, pltpu.roll(hi, HALF, 1)).astype(BF16)


def _dsa_attn(qa, ka, v, qi, kw, wf, tb, t, topk):
    b, _, s, _ = qa.shape
    nch = s // t
    gt = DSA_GROUP * t
    kern = functools.partial(_dsa_attn_kernel, t=t, topk=topk, s_len=s)
    return pl.pallas_call(
        kern, out_shape=jax.ShapeDtypeStruct((b, s, DSA_HEADS * DSA_HEAD_DIM), BF16),
        grid=(b, nch),
        in_specs=[pl.BlockSpec((1, DSA_HEADS, t, LANES), lambda bi, i: (bi, 0, i, 0)),
                  pl.BlockSpec((1, IDX_HEADS, t, LANES), lambda bi, i: (bi, 0, i, 0)),
                  pl.BlockSpec((1, t, LANES), lambda bi, i: (bi, i, 0)),
                  pl.BlockSpec((1, DSA_KV_HEADS, s, LANES), lambda bi, i: (bi, 0, 0, 0)),
                  pl.BlockSpec((1, DSA_KV_HEADS, s, LANES), lambda bi, i: (bi, 0, 0, 0)),
                  pl.BlockSpec((1, s, LANES), lambda bi, i: (bi, 0, 0)),
                  _const_spec((2, DSA_HEADS, LANES, LANES))],
        out_specs=pl.BlockSpec((1, t, DSA_HEADS * DSA_HEAD_DIM), lambda bi, i: (bi, i, 0)),
        scratch_shapes=[pltpu.VMEM((nch, t, t), I32),
                        pltpu.VMEM((nch, t, t), BF16),
                        pltpu.VMEM((2, t, t), F32),
                        pltpu.VMEM((2, gt, 2 * t), F32),
                        pltpu.VMEM((DSA_KV_HEADS, gt, LANES), F32),
                        pltpu.VMEM((DSA_KV_HEADS, gt, LANES), F32),
                        pltpu.VMEM((8, t), I32),
                        pltpu.VMEM((8, t), I32),
                        pltpu.VMEM((8, t), F32),
                        pltpu.VMEM((nch, 8, t), F32)],
        compiler_params=_params("parallel", "arbitrary"), name="dsa_attn",
    )(qa, qi, wf, ka, v, kw, tb)


def _mm_kernel(a_ref, w_ref, o_ref):
    o_ref[...] = _dot(a_ref[...].astype(BF16), w_ref[...]).astype(BF16)


def _mm(a, w, tm):
    m, k = a.shape
    n = w.shape[1]
    return pl.pallas_call(
        _mm_kernel, out_shape=jax.ShapeDtypeStruct((m, n), BF16), grid=(m // tm,),
        in_specs=[pl.BlockSpec((tm, k), lambda i: (i, 0)), _const_spec((k, n))],
        out_specs=pl.BlockSpec((tm, n), lambda i: (i, 0)),
        compiler_params=_params("parallel"), name="mem_kv",
    )(a, w)


def _post_kernel(a_ref, x_ref, wo_ref, wq_ref, kv_ref, wxo_ref, g_ref, b_ref, o_ref, oc_sc):
    y = _dot(a_ref[0], wo_ref[...])
    x1 = _layer_norm(ALPHA * x_ref[0] + y, g_ref[0:1, :], b_ref[0:1, :])
    q = (_dot(x1.astype(BF16), wq_ref[...]) * (XA_HEAD_DIM ** -0.5 * LOG2E)).astype(BF16)
    for hd in range(XA_HEADS):
        sl = slice(hd * XA_HEAD_DIM, (hd + 1) * XA_HEAD_DIM)
        sc = _dot_nt(q[:, sl], kv_ref[0, :, sl])
        m = jnp.max(sc, axis=-1, keepdims=True)
        p = jnp.exp2(sc - m)
        l = jnp.sum(p, axis=-1, keepdims=True)
        vsl = slice(D_MODEL + hd * XA_HEAD_DIM, D_MODEL + (hd + 1) * XA_HEAD_DIM)
        oc_sc[:, sl] = (_dot(p.astype(BF16), kv_ref[0, :, vsl]) / l).astype(BF16)
    y2 = _dot(oc_sc[...], wxo_ref[...])
    o_ref[0] = _layer_norm(ALPHA * x1 + y2, g_ref[1:2, :], b_ref[1:2, :])


def _post(a, x, w_o, w_q, kv, w_xo, g, bta, tm):
    b, s, _ = x.shape
    mlen = kv.shape[1]
    tok = lambda bi, i: (bi, i, 0)
    return pl.pallas_call(
        _post_kernel, out_shape=jax.ShapeDtypeStruct((b, s, D_MODEL), F32), grid=(b, s // tm),
        in_specs=[pl.BlockSpec((1, tm, D_MODEL), tok), pl.BlockSpec((1, tm, D_MODEL), tok),
                  _const_spec((D_MODEL, D_MODEL)), _const_spec((D_MODEL, D_MODEL)),
                  pl.BlockSpec((1, mlen, 2 * D_MODEL), lambda bi, i: (bi, 0, 0)),
                  _const_spec((D_MODEL, D_MODEL)), _const_spec((3, D_MODEL)), _const_spec((3, D_MODEL))],
        out_specs=pl.BlockSpec((1, tm, D_MODEL), tok),
        scratch_shapes=[pltpu.VMEM((tm, D_MODEL), BF16)],
        compiler_params=_params("parallel", "parallel"), name="post_attn",
    )(a, x, w_o, w_q, kv, w_xo, g, bta)


FFN_CHUNK = 256


def _ffn_kernel(x_ref, wi_ref, wd_ref, g_ref, b_ref, o_ref, act_sc):
    x = x_ref[...]
    xb = x.astype(BF16)
    for c in range(FFN_HIDDEN // FFN_CHUNK):
        sl = slice(c * FFN_CHUNK, (c + 1) * FFN_CHUNK)
        gte = _dot(xb, wi_ref[:, sl])
        up = _dot(xb, wi_ref[:, FFN_HIDDEN + c * FFN_CHUNK:FFN_HIDDEN + (c + 1) * FFN_CHUNK])
        act_sc[:, sl] = (gte * jax.nn.sigmoid(gte) * up).astype(BF16)
    y = _dot(act_sc[...], wd_ref[...])
    o_ref[...] = _layer_norm(ALPHA * x + y, g_ref[2:3, :], b_ref[2:3, :])


def _ffn(x2d, w_in, w_down, g, bta, tm):
    n = x2d.shape[0]
    return pl.pallas_call(
        _ffn_kernel, out_shape=jax.ShapeDtypeStruct((n, D_MODEL), F32), grid=(n // tm,),
        in_specs=[pl.BlockSpec((tm, D_MODEL), lambda i: (i, 0)),
                  _const_spec((D_MODEL, 2 * FFN_HIDDEN)), _const_spec((FFN_HIDDEN, D_MODEL)),
                  _const_spec((3, D_MODEL)), _const_spec((3, D_MODEL))],
        out_specs=pl.BlockSpec((tm, D_MODEL), lambda i: (i, 0)),
        scratch_shapes=[pltpu.VMEM((tm, FFN_HIDDEN), BF16)],
        compiler_params=_params("parallel"), name="ffn",
    )(x2d, w_in.astype(BF16), w_down.astype(BF16), g, bta)


def _tile(n, pref):
    t = min(n, pref)
    assert n % t == 0, (n, pref)
    return t


def kernel(x, mem, positions, rel_bias, mla_w_in, mla_q_norm, mla_w_uq, mla_kv_norm, mla_w_ukv, mla_w_o,
           dsa_w_in, dsa_w_o, xa_w_q, xa_w_kv, xa_w_o, ffn_w_in, ffn_w_down, ln_g, ln_b):
    b, s, d = x.shape
    assert d == D_MODEL and s % LANES == 0
    mlen = mem.shape[1]
    topk = min(TOPK_MAX, s // 4)
    t_attn = _tile(s, 256)
    tm_proj = _tile(s, 512)
    tm_post = _tile(s, 1024)
    tm_ffn = _tile(b * s, 1024)
    pos3 = positions.reshape(b, s, 1)
    tb = _t5_tiles(rel_bias)
    for i in range(DEPTH):
        j = i // 2
        if i % 2 == 0:
            q, k, v = _mla_proj(x, pos3, mla_w_in[j], mla_q_norm[j], mla_w_uq[j], mla_kv_norm[j],
                                mla_w_ukv[j], tm_proj)
            a = _flash_mla(q, k, v, _tile(s, 512))
            w_o = mla_w_o[j]
        else:
            qa, ka, v2, qi, kw, wf = _dsa_proj(x, dsa_w_in[j], rel_bias, tm_proj)
            a = _dsa_attn(qa, ka, v2, qi, kw, wf, tb, t_attn, topk)
            w_o = dsa_w_o[j]
        kv = _mm(mem.reshape(b * mlen, d), xa_w_kv[i].astype(BF16), _tile(b * mlen, 512)).reshape(b, mlen, 2 * d)
        x = _post(a, x, w_o.astype(BF16), xa_w_q[i].astype(BF16), kv, xa_w_o[i].astype(BF16),
                  ln_g[i], ln_b[i], tm_post)
        x = _ffn(x.reshape(b * s, d), ffn_w_in[i], ffn_w_down[i], ln_g[i], ln_b[i], tm_ffn).reshape(b, s, d)
    return x
```

```python
import functools
import math

import numpy as np
import jax
import jax.numpy as jnp
from jax import lax
from jax.experimental import pallas as pl
from jax.experimental.pallas import tpu as pltpu

D_MODEL = 1024
DEPTH = 2
MLA_HEADS = 16
MLA_Q_RANK = 512
MLA_KV_RANK = 256
MLA_NOPE = 64
MLA_ROPE = 32
MLA_V = 64
DSA_HEADS = 16
DSA_KV_HEADS = 4
DSA_GROUP = DSA_HEADS // DSA_KV_HEADS
DSA_HEAD_DIM = 64
IDX_HEADS = 8
IDX_DIM = 64
TOPK_MAX = 256
XA_HEADS = 4
XA_HEAD_DIM = D_MODEL // XA_HEADS
FFN_HIDDEN = -(-8 * D_MODEL // (3 * 256)) * 256
REL_BUCKETS = 32
REL_MAX_DIST = 128
ROPE_BASE = 10000.0
LN_EPS = 1e-5
RMS_EPS = 1e-6
ALPHA = (2 * DEPTH) ** 0.25

LANES = 128
HALF = LANES // 2
VMEM_LIMIT = 56 * 1024 * 1024

LOG2E = math.log2(math.e)
NEG_BIG = -1e30
F32 = jnp.float32
BF16 = jnp.bfloat16
I32 = jnp.int32
_NT = (((1,), (1,)), ((), ()))


def _dot(a, b):
    return jnp.dot(a, b, preferred_element_type=F32)


def _dot_nt(a, b):
    return lax.dot_general(a, b, _NT, preferred_element_type=F32)


def _layer_norm(y, g, b):
    mu = jnp.mean(y, axis=-1, keepdims=True)
    yc = y - mu
    var = jnp.mean(yc * yc, axis=-1, keepdims=True)
    return yc * lax.rsqrt(var + LN_EPS) * g + b


def _rms_norm(y, g):
    return y * lax.rsqrt(jnp.mean(y * y, axis=-1, keepdims=True) + RMS_EPS) * g


def _params(*sem):
    return pltpu.CompilerParams(dimension_semantics=sem, vmem_limit_bytes=VMEM_LIMIT)


def _const_spec(shape):
    nd = len(shape)
    return pl.BlockSpec(shape, lambda *_: (0,) * nd)


def _take_cols(w, idx):
    idx = np.asarray(idx)
    wp = jnp.concatenate([w, jnp.zeros((w.shape[0], 1), w.dtype)], axis=1)
    return wp[:, np.where(idx < 0, w.shape[1], idx)]


_R = MLA_ROPE // 2
_MLA_W = MLA_Q_RANK + MLA_KV_RANK


def _mla_proj_kernel(x_ref, pos_ref, win_ref, qg_ref, kvg_ref, wuq_ref, wuk_ref, wuv_ref,
                     freq_ref, sgn_ref, q_ref, k_ref, v_ref):
    xb = x_ref[0].astype(BF16)
    h = _dot(xb, win_ref[...])
    cq = _rms_norm(h[:, :MLA_Q_RANK], qg_ref[...]).astype(BF16)
    ckv = _rms_norm(h[:, MLA_Q_RANK:_MLA_W], kvg_ref[...]).astype(BF16)
    kr = h[:, _MLA_W:]
    ang = pos_ref[0].astype(F32) * freq_ref[...]
    cos = jnp.cos(ang)
    sin = jnp.sin(ang) * sgn_ref[...]

    def rope(t):
        return t * cos + pltpu.roll(t, HALF, 1) * sin

    kr = rope(kr)
    q_all = _dot(cq, wuq_ref[...])
    k_all = _dot(ckv, wuk_ref[...])
    v_all = _dot(ckv, wuv_ref[...])
    qscale = (MLA_NOPE + MLA_ROPE) ** -0.5 * LOG2E
    ones_hi = (lax.broadcasted_iota(I32, (1, LANES), 1) >= HALF).astype(F32)
    for hd in range(MLA_HEADS):
        sl = slice(hd * LANES, (hd + 1) * LANES)
        q_ref[0, hd] = (rope(q_all[:, sl]) * qscale).astype(BF16)
        k_ref[0, hd] = (k_all[:, sl] + kr).astype(BF16)
        v_ref[0, hd] = (v_all[:, sl] + ones_hi).astype(BF16)


def _mla_proj(x, pos3, w_in, q_norm, w_uq, kv_norm, w_ukv, tm):
    b, s, _ = x.shape
    hq = MLA_NOPE + MLA_ROPE
    in_idx = list(range(_MLA_W)) + [_MLA_W + j for j in range(_R)] + [-1] * (HALF - _R) \
        + [_MLA_W + _R + j for j in range(_R)] + [-1] * (HALF - _R)
    q_idx, k_idx, v_idx = [], [], []
    n_lo = HALF - _R
    n_hi = MLA_NOPE - n_lo
    tail = [-1] * (HALF - _R - n_hi)
    for hd in range(MLA_HEADS):
        qb = hd * hq
        q_idx += [qb + MLA_NOPE + j for j in range(_R)] + [qb + j for j in range(n_lo)] \
            + [qb + MLA_NOPE + _R + j for j in range(_R)] + [qb + n_lo + j for j in range(n_hi)] + tail
        kb = hd * (MLA_NOPE + MLA_V)
        k_idx += [-1] * _R + [kb + j for j in range(n_lo)] + [-1] * _R + [kb + n_lo + j for j in range(n_hi)] + tail
        v_idx += [kb + MLA_NOPE + j for j in range(MLA_V)] + [-1] * (LANES - MLA_V)
    win = _take_cols(w_in, in_idx).astype(BF16)
    wuq = _take_cols(w_uq, q_idx).astype(BF16)
    wuk = _take_cols(w_ukv, k_idx).astype(BF16)
    wuv = _take_cols(w_ukv, v_idx).astype(BF16)
    inv_freq = ROPE_BASE ** (-jnp.arange(0, MLA_ROPE, 2, dtype=F32) / MLA_ROPE)
    zpad = jnp.zeros((HALF - _R,), F32)
    freq = jnp.concatenate([inv_freq, zpad, inv_freq, zpad])[None]
    sgn = jnp.concatenate([-jnp.ones((_R,), F32), zpad, jnp.ones((_R,), F32), zpad])[None]
    nq = MLA_HEADS * LANES
    grid = (b, s // tm)
    out_shape = (jax.ShapeDtypeStruct((b, MLA_HEADS, s, LANES), BF16),
                 jax.ShapeDtypeStruct((b, MLA_HEADS, s, LANES), BF16),
                 jax.ShapeDtypeStruct((b, MLA_HEADS, s, LANES), BF16))
    return pl.pallas_call(
        _mla_proj_kernel, out_shape=out_shape, grid=grid,
        in_specs=[pl.BlockSpec((1, tm, D_MODEL), lambda bi, i: (bi, i, 0)),
                  pl.BlockSpec((1, tm, 1), lambda bi, i: (bi, i, 0)),
                  _const_spec((D_MODEL, _MLA_W + LANES)),
                  _const_spec((1, MLA_Q_RANK)), _const_spec((1, MLA_KV_RANK)),
                  _const_spec((MLA_Q_RANK, nq)), _const_spec((MLA_KV_RANK, nq)),
                  _const_spec((MLA_KV_RANK, nq)),
                  _const_spec((1, LANES)), _const_spec((1, LANES))],
        out_specs=(pl.BlockSpec((1, MLA_HEADS, tm, LANES), lambda bi, i: (bi, 0, i, 0)),
                   pl.BlockSpec((1, MLA_HEADS, tm, LANES), lambda bi, i: (bi, 0, i, 0)),
                   pl.BlockSpec((1, MLA_HEADS, tm, LANES), lambda bi, i: (bi, 0, i, 0))),
        compiler_params=_params("parallel", "parallel"), name="mla_proj",
    )(x, pos3, win, q_norm[None], kv_norm[None], wuq, wuk, wuv, freq, sgn)


def _flash_kernel(q_ref, k_ref, v_ref, o_ref, m_sc, acc_sc, s_sc, *, t, nh):
    i = pl.program_id(2)
    row = lax.broadcasted_iota(I32, (t, t), 0)
    col = lax.broadcasted_iota(I32, (t, t), 1)
    m_sc[...] = jnp.full(m_sc.shape, NEG_BIG, F32)
    acc_sc[...] = jnp.zeros(acc_sc.shape, F32)

    def scores(j, hh):
        kc = k_ref[0, hh, pl.ds(pl.multiple_of(j * t, t), t), :]
        return _dot_nt(q_ref[0, hh], kc)

    def step(j, slot, diag):
        for hh in range(nh):
            sc = s_sc[slot, hh]
            if diag:
                sc = jnp.where(col <= row, sc, NEG_BIG)
            else:
                s_sc[1 - slot, hh] = scores(j + 1, hh)
            m_prev = m_sc[hh]
            m_new = jnp.maximum(m_prev, jnp.max(sc, axis=-1, keepdims=True))
            a = jnp.exp2(m_prev - m_new)
            p = jnp.concatenate([jnp.exp2(sc[:, c * LANES:(c + 1) * LANES] - m_new)
                                 for c in range(t // LANES)], axis=1).astype(BF16)
            vc = v_ref[0, hh, pl.ds(pl.multiple_of(j * t, t), t), :]
            acc_sc[hh] = a * acc_sc[hh] + _dot(p, vc)
            m_sc[hh] = m_new

    for hh in range(nh):
        s_sc[0, hh] = scores(0, hh)

    def pair(j2, c):
        step(2 * j2, 0, False)
        step(2 * j2 + 1, 1, False)
        return c

    lax.fori_loop(0, i // 2, pair, 0)

    @pl.when(i % 2 == 1)
    def _():
        step(i - 1, 0, False)
        step(i, 1, True)

    @pl.when(i % 2 == 0)
    def _():
        step(i, 0, True)

    lane = lax.broadcasted_iota(I32, (t, LANES), 1)
    for pr in range(nh // 2):
        a0 = acc_sc[2 * pr]
        a1 = acc_sc[2 * pr + 1]
        o0 = a0 / a0[:, HALF:HALF + 1]
        o1 = a1 / a1[:, HALF:HALF + 1]
        o_ref[0, :, pr * LANES:(pr + 1) * LANES] = \
            jnp.where(lane < HALF, o0, pltpu.roll(o1, HALF, 1)).astype(BF16)


MLA_HEADS_PER_STEP = 4


def _flash_mla(q, k, v, t):
    b, h, s, _ = q.shape
    nh = MLA_HEADS_PER_STEP
    return pl.pallas_call(
        functools.partial(_flash_kernel, t=t, nh=nh),
        out_shape=jax.ShapeDtypeStruct((b, s, h * MLA_V), BF16),
        grid=(b, h // nh, s // t),
        in_specs=[pl.BlockSpec((1, nh, t, LANES), lambda bi, p, i: (bi, p, i, 0)),
                  pl.BlockSpec((1, nh, s, LANES), lambda bi, p, i: (bi, p, 0, 0)),
                  pl.BlockSpec((1, nh, s, LANES), lambda bi, p, i: (bi, p, 0, 0))],
        out_specs=pl.BlockSpec((1, t, nh * MLA_V), lambda bi, p, i: (bi, i, p)),
        scratch_shapes=[pltpu.VMEM((nh, t, LANES), F32), pltpu.VMEM((nh, t, LANES), F32),
                        pltpu.VMEM((2, nh, t, t), F32)],
        compiler_params=_params("parallel", "parallel", "arbitrary"), name="mla_flash",
    )(q, k, v)


_DQ0 = 0
_DK0 = DSA_HEADS * LANES
_DV0 = _DK0 + DSA_KV_HEADS * LANES
_DI0 = _DV0 + DSA_KV_HEADS * LANES
_DW0 = _DI0 + IDX_HEADS * LANES
_DN = _DW0 + LANES


def _dsa_proj_kernel(x_ref, w_ref, qhot_ref, krow_ref, wsc_ref, qa_ref, ka_ref, v2_ref, qi_ref, kw_ref, wf_ref):
    xb = x_ref[0].astype(BF16)
    h = _dot(xb, w_ref[...])
    qscale = DSA_HEAD_DIM ** -0.5 * LOG2E
    for hd in range(DSA_HEADS):
        sl = slice(_DQ0 + hd * LANES, _DQ0 + (hd + 1) * LANES)
        qa_ref[0, hd] = (h[:, sl] * qscale + qhot_ref[hd:hd + 1, :]).astype(BF16)
    ones_hi = (lax.broadcasted_iota(I32, (1, LANES), 1) >= HALF).astype(F32)
    for g in range(DSA_KV_HEADS):
        ka_ref[0, g] = (h[:, _DK0 + g * LANES:_DK0 + (g + 1) * LANES] + krow_ref[g:g + 1, :]).astype(BF16)
        v2_ref[0, g] = (h[:, _DV0 + g * LANES:_DV0 + (g + 1) * LANES] + ones_hi).astype(BF16)
    for hd in range(IDX_HEADS):
        qi_ref[0, hd] = (h[:, _DI0 + hd * LANES:_DI0 + (hd + 1) * LANES] * (IDX_DIM ** -0.5)).astype(BF16)
    kw = h[:, _DW0:]
    kw_ref[0] = kw.astype(BF16)
    wf_ref[0] = kw * wsc_ref[...]


def _dsa_proj(x, w_in, rel_bias, tm):
    b, s, _ = x.shape
    o1 = DSA_HEADS * DSA_HEAD_DIM
    o2 = o1 + DSA_KV_HEADS * DSA_HEAD_DIM
    o3 = o2 + DSA_KV_HEADS * DSA_HEAD_DIM
    o4 = o3 + IDX_HEADS * IDX_DIM
    o5 = o4 + IDX_DIM
    idx = []
    for hd in range(DSA_HEADS):
        idx += [hd * DSA_HEAD_DIM + j for j in range(DSA_HEAD_DIM)] + [-1] * HALF
    for g in range(DSA_KV_HEADS):
        idx += [o1 + g * DSA_HEAD_DIM + j for j in range(DSA_HEAD_DIM)] + [-1] * HALF
    for g in range(DSA_KV_HEADS):
        idx += [o2 + g * DSA_HEAD_DIM + j for j in range(DSA_HEAD_DIM)] + [-1] * HALF
    for hd in range(IDX_HEADS):
        idx += [o3 + hd * IDX_DIM + j for j in range(IDX_DIM)] + [-1] * HALF
    idx += [o4 + j for j in range(IDX_DIM)] + [o5 + j for j in range(IDX_HEADS)] + [-1] * (HALF - IDX_HEADS)
    assert len(idx) == _DN
    w = _take_cols(w_in, idx).astype(BF16)
    cfar = rel_bias[REL_BUCKETS - 1] * LOG2E
    chi = cfar.astype(BF16).astype(F32)
    clo = (cfar - chi).astype(BF16).astype(F32)
    qhot = np.zeros((DSA_HEADS, LANES), np.float32)
    for hd in range(DSA_HEADS):
        qhot[hd, HALF + hd % DSA_GROUP] = 1.0
        qhot[hd, HALF + DSA_GROUP + hd % DSA_GROUP] = 1.0
    zpad = jnp.zeros((DSA_KV_HEADS, LANES - HALF - 2 * DSA_GROUP), F32)
    krow = jnp.concatenate([jnp.zeros((DSA_KV_HEADS, HALF), F32), chi.reshape(DSA_KV_HEADS, DSA_GROUP),
                            clo.reshape(DSA_KV_HEADS, DSA_GROUP), zpad], axis=1)
    wsc = np.zeros((1, LANES), np.float32)
    wsc[0, HALF:HALF + IDX_HEADS] = IDX_HEADS ** -0.5
    hspec = lambda n: pl.BlockSpec((1, n, tm, LANES), lambda bi, i: (bi, 0, i, 0))
    tspec = pl.BlockSpec((1, tm, LANES), lambda bi, i: (bi, i, 0))
    out_shape = (jax.ShapeDtypeStruct((b, DSA_HEADS, s, LANES), BF16),
                 jax.ShapeDtypeStruct((b, DSA_KV_HEADS, s, LANES), BF16),
                 jax.ShapeDtypeStruct((b, DSA_KV_HEADS, s, LANES), BF16),
                 jax.ShapeDtypeStruct((b, IDX_HEADS, s, LANES), BF16),
                 jax.ShapeDtypeStruct((b, s, LANES), BF16),
                 jax.ShapeDtypeStruct((b, s, LANES), F32))
    return pl.pallas_call(
        _dsa_proj_kernel, out_shape=out_shape, grid=(b, s // tm),
        in_specs=[pl.BlockSpec((1, tm, D_MODEL), lambda bi, i: (bi, i, 0)),
                  _const_spec((D_MODEL, _DN)), _const_spec((DSA_HEADS, LANES)),
                  _const_spec((DSA_KV_HEADS, LANES)), _const_spec((1, LANES))],
        out_specs=(hspec(DSA_HEADS), hspec(DSA_KV_HEADS), hspec(DSA_KV_HEADS), hspec(IDX_HEADS), tspec, tspec),
        compiler_params=_params("parallel", "parallel"), name="dsa_proj",
    )(x, w, jnp.asarray(qhot), krow, jnp.asarray(wsc))


def _t5_tiles_kernel(rb_ref, o_ref):
    d = pl.program_id(0)
    hd = pl.program_id(1)
    row = lax.broadcasted_iota(I32, (LANES, LANES), 0)
    col = lax.broadcasted_iota(I32, (LANES, LANES), 1)
    n = jnp.maximum(d * LANES + row - col, 0)
    max_exact = REL_BUCKETS // 2
    nf = jnp.maximum(n, 1).astype(F32)
    large = max_exact + (jnp.log(nf / max_exact) / math.log(REL_MAX_DIST / max_exact)
                         * (REL_BUCKETS - max_exact)).astype(I32)
    large = jnp.minimum(large, REL_BUCKETS - 1)
    bucket = jnp.where(n < max_exact, n, large)
    far = rb_ref[REL_BUCKETS - 1, hd]
    acc = jnp.zeros((LANES, LANES), F32)
    for bk in range(REL_BUCKETS - 1):
        acc = jnp.where(bucket == bk, rb_ref[bk, hd] - far, acc)
    o_ref[0, 0] = acc * LOG2E


def _t5_tiles(rel_bias):
    return pl.pallas_call(
        _t5_tiles_kernel,
        out_shape=jax.ShapeDtypeStruct((2, DSA_HEADS, LANES, LANES), F32),
        grid=(2, DSA_HEADS),
        in_specs=[pl.BlockSpec(memory_space=pltpu.SMEM)],
        out_specs=pl.BlockSpec((1, 1, LANES, LANES), lambda d, h: (d, h, 0, 0)),
        compiler_params=_params("parallel", "parallel"), name="t5_tiles",
    )(rel_bias)


_SENT_KEY = int(np.int32(np.uint32(0xFF800000) ^ np.uint32(0x7FFFFFFF)))
_HI16 = -65536
_MIN_NORMAL_BITS = 0x00800000
_BITS_ALWAYS = 23
_INT_MIN = -(2 ** 31)


def _dsa_attn_kernel(qa_ref, qi_ref, wf_ref, ka_ref, v_ref, kw_ref, tb_ref, o_ref,
                     keyt_sc, kb_sc, madd_sc, s_sc, m_sc, acc_sc, cut_sc, thr_sc, cge_sc, tcnt_sc,
                     *, t, topk, s_len):
    i = pl.program_id(1)
    nsub = t // LANES
    kidx = lax.broadcasted_iota(I32, (t, t), 0)
    qidx = lax.broadcasted_iota(I32, (t, t), 1)

    qi = qi_ref[0].reshape(IDX_HEADS * t, LANES)
    wt = wf_ref[0].T
    wrows = [wt[HALF + hd:HALF + hd + 1, :] for hd in range(IDX_HEADS)]

    def scores_t(j0, w):
        kc = kw_ref[0, pl.ds(pl.multiple_of(j0 * t, t), w * t), :]
        d = _dot_nt(kc, qi)
        sc = wrows[0] * jnp.maximum(d[:, 0:t], 0.0)
        for hd in range(1, IDX_HEADS):
            sc = sc + wrows[hd] * jnp.maximum(d[:, hd * t:(hd + 1) * t], 0.0)
        return sc

    def store_keys(j, sc, causal):
        if causal:
            sc = jnp.where(kidx <= qidx, sc, -jnp.inf)
        bits = pltpu.bitcast(sc + 0.0, I32)
        keyt_sc[j] = bits ^ ((bits >> 31) & 0x7FFFFFFF)
        kb_sc[j] = pltpu.bitcast(bits & _HI16, F32).astype(BF16)

    def p1(u, c):
        sc = scores_t(2 * u, 2)
        store_keys(2 * u, sc[0:t], False)
        store_keys(2 * u + 1, sc[t:2 * t], False)
        return c

    lax.fori_loop(0, i // 2, p1, 0)

    @pl.when(i % 2 == 1)
    def _():
        sc = scores_t(i - 1, 2)
        store_keys(i - 1, sc[0:t], False)
        store_keys(i, sc[t:2 * t], True)

    @pl.when(i % 2 == 0)
    def _():
        store_keys(i, scores_t(i, 1), True)

    cacc = 32

    kf = float(topk)

    def bisect(nchunks):
        def count_ge(cand):
            acc = jnp.zeros((cacc, t), F32)
            for j in range(nchunks):
                c = jnp.where(keyt_sc[j] >= cand, 1.0, 0.0)
                acc = acc + jnp.sum(c.reshape(t // cacc, cacc, t), axis=0)
            return jnp.sum(acc, axis=0, keepdims=True)

        one_b = jnp.ones((), BF16)
        zero_b = jnp.zeros((), BF16)

        def count_ge16(cand):
            cbits = (cand ^ ((cand >> 31) & 0x7FFFFFFF)) & _HI16
            cbits = jnp.where((cand > 0) & (cand < _MIN_NORMAL_BITS), _MIN_NORMAL_BITS, cbits)
            cf = lax.bitcast_convert_type(cbits, F32)
            cb = jnp.broadcast_to(cf, (16, t)).astype(BF16)
            cb = jnp.broadcast_to(cb[None], (t // 16, 16, t)).reshape(t, t)
            acc = jnp.zeros((cacc, t), BF16)
            for j in range(nchunks):
                c = jnp.where(kb_sc[j] >= cb, one_b, zero_b)
                for k in range(t // cacc):
                    acc = acc + c[k * cacc:(k + 1) * cacc]
            return jnp.sum(acc.astype(F32), axis=0, keepdims=True)

        cnt0 = count_ge16(jnp.zeros((1, t), I32))
        nonneg = cnt0 >= kf
        prefix0 = jnp.where(nonneg, 0, _INT_MIN).astype(I32)
        cge0 = jnp.where(nonneg, cnt0, float(nchunks * t))

        def bit_body(count_fn, bi, carry):
            prefix, cge = carry
            cand = prefix + lax.shift_left(jnp.int32(1), 30 - bi)
            cnt = count_fn(cand)
            take = cnt >= kf
            return jnp.where(take, cand, prefix), jnp.where(take, cnt, cge)

        carry = lax.fori_loop(0, 15, functools.partial(bit_body, count_ge16), (prefix0, cge0))
        carry = lax.fori_loop(15, _BITS_ALWAYS, functools.partial(bit_body, count_ge), carry)

        def more(state):
            bi, prefix, cge = state
            open_ = (cge != kf) & jnp.logical_not(nonneg & (prefix == 0))
            return (bi < 31) & (jnp.max(jnp.where(open_, 1.0, 0.0)) > 0.0)

        def one_bit(state):
            bi, prefix, cge = state
            return (bi + 1,) + bit_body(count_ge, bi, (prefix, cge))

        _, thr, cge = lax.while_loop(more, one_bit, (jnp.int32(_BITS_ALWAYS),) + carry)
        thr_sc[...] = jnp.broadcast_to(thr, thr_sc.shape)
        cge_sc[...] = jnp.broadcast_to(cge, cge_sc.shape)

    for ii in range(s_len // t):
        pl.when(i == ii)(functools.partial(bisect, ii + 1))
    thr = thr_sc[0:1, :]
    cnt_ge = cge_sc[0:1, :]
    excess = jnp.where((cnt_ge > kf) & (thr > _SENT_KEY), 1.0, 0.0)
    cut_sc[...] = jnp.full(cut_sc.shape, s_len, I32)

    @pl.when(jnp.max(excess) > 0.0)
    def _():
        def sweep(j, gt):
            kc = keyt_sc[j]
            tie = jnp.where(kc == thr, 1.0, 0.0)
            tcnt_sc[j] = jnp.broadcast_to(jnp.sum(tie, axis=0, keepdims=True), (8, t))
            g = jnp.where(kc > thr, 1.0, 0.0)
            return gt + jnp.sum(g.reshape(t // cacc, cacc, t), axis=0)

        gt = lax.fori_loop(0, i + 1, sweep, jnp.zeros((cacc, t), F32))
        need = kf - jnp.sum(gt, axis=0, keepdims=True)

        def scan(j, carry):
            run, cstar, base = carry
            new = run + tcnt_sc[j, 0:1, :]
            hit = (run < need) & (new >= need)
            return new, jnp.where(hit, j, cstar), jnp.where(hit, run, base)

        zrow = jnp.zeros((1, t), F32)
        _, cstar, base = lax.fori_loop(0, i + 1, scan, (zrow, jnp.zeros((1, t), I32), zrow))

        def gather(j, tile):
            return jnp.where(cstar == j, keyt_sc[j], tile)

        tile = lax.fori_loop(0, i + 1, gather, jnp.full((t, t), _SENT_KEY, I32))
        tie = jnp.where(tile == thr, 1.0, 0.0)
        need_in = need - base
        rbits = max(1, (t - 1).bit_length())

        def row_body(bi, r):
            cand = r + lax.shift_left(jnp.int32(1), rbits - 1 - bi)
            cnt = jnp.sum(jnp.where(kidx < cand, tie, 0.0), axis=0, keepdims=True)
            return jnp.where(cnt < need_in, cand, r)

        r = lax.fori_loop(0, rbits, row_body, jnp.zeros((1, t), I32))
        cut_sc[...] = jnp.broadcast_to(cstar * t + r, cut_sc.shape)

    cut = cut_sc[0:1, :]

    def to_mask(j, causal):
        kc = keyt_sc[j]
        sel = (kc > thr) | ((kc == thr) & (kidx + j * t <= cut))
        if causal:
            sel = sel & (kidx <= qidx)
        return jnp.where(sel, 0.0, NEG_BIG).T

    m_sc[...] = jnp.full(m_sc.shape, NEG_BIG, F32)
    acc_sc[...] = jnp.zeros(acc_sc.shape, F32)
    gt = DSA_GROUP * t
    near_delta = {"prev": nsub, "diag": 0}

    def attend(j0, parts):
        w = len(parts)
        start = pl.multiple_of(j0 * t, t)

        def qk(g):
            q = qa_ref[0, g * DSA_GROUP:(g + 1) * DSA_GROUP].reshape(gt, LANES)
            return _dot_nt(q, ka_ref[0, g, pl.ds(start, w * t), :])

        s_sc[0, :, 0:w * t] = qk(0)
        for pi, part in enumerate(parts):
            madd_sc[pi] = to_mask(j0 + pi, part == "diag")
        for g in range(DSA_KV_HEADS):
            if g + 1 < DSA_KV_HEADS:
                s_sc[(g + 1) % 2, :, 0:w * t] = qk(g + 1)
            slot = g % 2
            p_rows, a_rows = [], []
            for r in range(DSA_GROUP):
                hd = g * DSA_GROUP + r
                for rs in range(nsub):
                    r0 = r * t + rs * LANES
                    blocks = []
                    for cb in range(w * nsub):
                        part, cs = parts[cb // nsub], cb % nsub
                        blk = s_sc[slot, r0:r0 + LANES, cb * LANES:(cb + 1) * LANES] \
                            + madd_sc[cb // nsub, rs * LANES:(rs + 1) * LANES, cs * LANES:(cs + 1) * LANES]
                        if part in near_delta and near_delta[part] + rs - cs in (0, 1):
                            blk = blk + tb_ref[near_delta[part] + rs - cs, hd]
                        blocks.append(blk)
                    m_prev = m_sc[g, r0:r0 + LANES]
                    m_blk = blocks[0]
                    for blk in blocks[1:]:
                        m_blk = jnp.maximum(m_blk, blk)
                    m_new = jnp.maximum(m_prev, jnp.max(m_blk, axis=-1, keepdims=True))
                    a_rows.append(jnp.exp2(m_prev - m_new))
                    p_rows.append(jnp.concatenate([jnp.exp2(blk - m_new).astype(BF16) for blk in blocks], axis=1))
                    m_sc[g, r0:r0 + LANES] = m_new
            p = jnp.concatenate(p_rows, axis=0)
            a = jnp.concatenate(a_rows, axis=0)
            vc = v_ref[0, g, pl.ds(start, w * t), :]
            acc_sc[g] = a * acc_sc[g] + _dot(p, vc)

    @pl.when(i == 0)
    def _():
        attend(0, ("diag",))

    @pl.when(i >= 1)
    def _():
        nfar = i - 1

        def far_pair(u, c):
            attend(2 * u, ("far", "far"))
            return c

        lax.fori_loop(0, nfar // 2, far_pair, 0)

        @pl.when(nfar % 2 == 1)
        def _():
            attend(i - 2, ("far",))

        attend(i - 1, ("prev", "diag"))

    lane = lax.broadcasted_iota(I32, (t, LANES), 1)
    for g in range(DSA_KV_HEADS):
        acc = acc_sc[g]
        o = acc / acc[:, HALF:HALF + 1]
        for pr in range(DSA_GROUP // 2):
            lo = o[(2 * pr) * t:(2 * pr + 1) * t]
            hi = o[(2 * pr + 1) * t:(2 * pr + 2) * t]
            cblk = g * (DSA_GROUP // 2) + pr
            o_ref[0, :, cblk * LANES:(cblk + 1) * LANES] = \
                jnp.where(lane < HALF, lo, pltpu.roll(hi, HALF, 1)).astype(BF16)


def _dsa_attn(qa, ka, v, qi, kw, wf, tb, t, topk):
    b, _, s, _ = qa.shape
    nch = s // t
    gt = DSA_GROUP * t
    kern = functools.partial(_dsa_attn_kernel, t=t, topk=topk, s_len=s)
    return pl.pallas_call(
        kern, out_shape=jax.ShapeDtypeStruct((b, s, DSA_HEADS * DSA_HEAD_DIM), BF16),
        grid=(b, nch),
        in_specs=[pl.BlockSpec((1, DSA_HEADS, t, LANES), lambda bi, i: (bi, 0, i, 0)),
                  pl.BlockSpec((1, IDX_HEADS, t, LANES), lambda bi, i: (bi, 0, i, 0)),
                  pl.BlockSpec((1, t, LANES), lambda bi, i: (bi, i, 0)),
                  pl.BlockSpec((1, DSA_KV_HEADS, s, LANES), lambda bi, i: (bi, 0, 0, 0)),
                  pl.BlockSpec((1, DSA_KV_HEADS, s, LANES), lambda bi, i: (bi, 0, 0, 0)),
                  pl.BlockSpec((1, s, LANES), lambda bi, i: (bi, 0, 0)),
                  _const_spec((2, DSA_HEADS, LANES, LANES))],
        out_specs=pl.BlockSpec((1, t, DSA_HEADS * DSA_HEAD_DIM), lambda bi, i: (bi, i, 0)),
        scratch_shapes=[pltpu.VMEM((nch, t, t), I32),
                        pltpu.VMEM((nch, t, t), BF16),
                        pltpu.VMEM((2, t, t), F32),
                        pltpu.VMEM((2, gt, 2 * t), F32),
                        pltpu.VMEM((DSA_KV_HEADS, gt, LANES), F32),
                        pltpu.VMEM((DSA_KV_HEADS, gt, LANES), F32),
                        pltpu.VMEM((8, t), I32),
                        pltpu.VMEM((8, t), I32),
                        pltpu.VMEM((8, t), F32),
                        pltpu.VMEM((nch, 8, t), F32)],
        compiler_params=_params("parallel", "arbitrary"), name="dsa_attn",
    )(qa, qi, wf, ka, v, kw, tb)


def _mm_kernel(a_ref, w_ref, o_ref):
    o_ref[...] = _dot(a_ref[...].astype(BF16), w_ref[...]).astype(BF16)


def _mm(a, w, tm):
    m, k = a.shape
    n = w.shape[1]
    return pl.pallas_call(
        _mm_kernel, out_shape=jax.ShapeDtypeStruct((m, n), BF16), grid=(m // tm,),
        in_specs=[pl.BlockSpec((tm, k), lambda i: (i, 0)), _const_spec((k, n))],
        out_specs=pl.BlockSpec((tm, n), lambda i: (i, 0)),
        compiler_params=_params("parallel"), name="mem_kv",
    )(a, w)


def _post_kernel(a_ref, x_ref, wo_ref, wq_ref, kv_ref, wxo_ref, g_ref, b_ref, o_ref, oc_sc):
    y = _dot(a_ref[0], wo_ref[...])
    x1 = _layer_norm(ALPHA * x_ref[0] + y, g_ref[0:1, :], b_ref[0:1, :])
    q = (_dot(x1.astype(BF16), wq_ref[...]) * (XA_HEAD_DIM ** -0.5 * LOG2E)).astype(BF16)
    for hd in range(XA_HEADS):
        sl = slice(hd * XA_HEAD_DIM, (hd + 1) * XA_HEAD_DIM)
        sc = _dot_nt(q[:, sl], kv_ref[0, :, sl])
        m = jnp.max(sc, axis=-1, keepdims=True)
        p = jnp.exp2(sc - m)
        l = jnp.sum(p, axis=-1, keepdims=True)
        vsl = slice(D_MODEL + hd * XA_HEAD_DIM, D_MODEL + (hd + 1) * XA_HEAD_DIM)
        oc_sc[:, sl] = (_dot(p.astype(BF16), kv_ref[0, :, vsl]) / l).astype(BF16)
    y2 = _dot(oc_sc[...], wxo_ref[...])
    o_ref[0] = _layer_norm(ALPHA * x1 + y2, g_ref[1:2, :], b_ref[1:2, :])


def _post(a, x, w_o, w_q, kv, w_xo, g, bta, tm):
    b, s, _ = x.shape
    mlen = kv.shape[1]
    tok = lambda bi, i: (bi, i, 0)
    return pl.pallas_call(
        _post_kernel, out_shape=jax.ShapeDtypeStruct((b, s, D_MODEL), F32), grid=(b, s // tm),
        in_specs=[pl.BlockSpec((1, tm, D_MODEL), tok), pl.BlockSpec((1, tm, D_MODEL), tok),
                  _const_spec((D_MODEL, D_MODEL)), _const_spec((D_MODEL, D_MODEL)),
                  pl.BlockSpec((1, mlen, 2 * D_MODEL), lambda bi, i: (bi, 0, 0)),
                  _const_spec((D_MODEL, D_MODEL)), _const_spec((3, D_MODEL)), _const_spec((3, D_MODEL))],
        out_specs=pl.BlockSpec((1, tm, D_MODEL), tok),
        scratch_shapes=[pltpu.VMEM((tm, D_MODEL), BF16)],
        compiler_params=_params("parallel", "parallel"), name="post_attn",
    )(a, x, w_o, w_q, kv, w_xo, g, bta)


FFN_CHUNK = 256


def _ffn_kernel(x_ref, wi_ref, wd_ref, g_ref, b_ref, o_ref, act_sc):
    x = x_ref[...]
    xb = x.astype(BF16)
    for c in range(FFN_HIDDEN // FFN_CHUNK):
        sl = slice(c * FFN_CHUNK, (c + 1) * FFN_CHUNK)
        gte = _dot(xb, wi_ref[:, sl])
        up = _dot(xb, wi_ref[:, FFN_HIDDEN + c * FFN_CHUNK:FFN_HIDDEN + (c + 1) * FFN_CHUNK])
        act_sc[:, sl] = (gte * jax.nn.sigmoid(gte) * up).astype(BF16)
    y = _dot(act_sc[...], wd_ref[...])
    o_ref[...] = _layer_norm(ALPHA * x + y, g_ref[2:3, :], b_ref[2:3, :])


def _ffn(x2d, w_in, w_down, g, bta, tm):
    n = x2d.shape[0]
    return pl.pallas_call(
        _ffn_kernel, out_shape=jax.ShapeDtypeStruct((n, D_MODEL), F32), grid=(n // tm,),
        in_specs=[pl.BlockSpec((tm, D_MODEL), lambda i: (i, 0)),
                  _const_spec((D_MODEL, 2 * FFN_HIDDEN)), _const_spec((FFN_HIDDEN, D_MODEL)),
                  _const_spec((3, D_MODEL)), _const_spec((3, D_MODEL))],
        out_specs=pl.BlockSpec((tm, D_MODEL), lambda i: (i, 0)),
        scratch_shapes=[pltpu.VMEM((tm, FFN_HIDDEN), BF16)],
        compiler_params=_params("parallel"), name="ffn",
    )(x2d, w_in.astype(BF16), w_down.astype(BF16), g, bta)


def _tile(n, pref):
    t = min(n, pref)
    assert n % t == 0, (n, pref)
    return t


def kernel(x, mem, positions, rel_bias, mla_w_in, mla_q_norm, mla_w_uq, mla_kv_norm, mla_w_ukv, mla_w_o,
           dsa_w_in, dsa_w_o, xa_w_q, xa_w_kv, xa_w_o, ffn_w_in, ffn_w_down, ln_g, ln_b):
    b, s, d = x.shape
    assert d == D_MODEL and s % LANES == 0
    mlen = mem.shape[1]
    topk = min(TOPK_MAX, s // 4)
    t_attn = _tile(s, 256)
    tm_proj = _tile(s, 512)
    tm_post = _tile(s, 1024)
    tm_ffn = _tile(b * s, 1024)
    pos3 = positions.reshape(b, s, 1)
    tb = _t5_tiles(rel_bias)
    for i in range(DEPTH):
        j = i // 2
        if i % 2 == 0:
            q, k, v = _mla_proj(x, pos3, mla_w_in[j], mla_q_norm[j], mla_w_uq[j], mla_kv_norm[j],
                                mla_w_ukv[j], tm_proj)
            a = _flash_mla(q, k, v, _tile(s, 512))
            w_o = mla_w_o[j]
        else:
            qa, ka, v2, qi, kw, wf = _dsa_proj(x, dsa_w_in[j], rel_bias, tm_proj)
            a = _dsa_attn(qa, ka, v2, qi, kw, wf, tb, t_attn, topk)
            w_o = dsa_w_o[j]
        kv = _mm(mem.reshape(b * mlen, d), xa_w_kv[i].astype(BF16), _tile(b * mlen, 512)).reshape(b, mlen, 2 * d)
        x = _post(a, x, w_o.astype(BF16), xa_w_q[i].astype(BF16), kv, xa_w_o[i].astype(BF16),
                  ln_g[i], ln_b[i], tm_post)
        x = _ffn(x.reshape(b * s, d), ffn_w_in[i], ffn_w_down[i], ln_g[i], ln_b[i], tm_ffn).reshape(b, s, d)
    return x
```

```python
import functools
import math

import numpy as np
import jax
import jax.numpy as jnp
from jax import lax
from jax.experimental import pallas as pl
from jax.experimental.pallas import tpu as pltpu

D_MODEL = 1024
DEPTH = 2
MLA_HEADS = 16
MLA_Q_RANK = 512
MLA_KV_RANK = 256
MLA_NOPE = 64
MLA_ROPE = 32
MLA_V = 64
DSA_HEADS = 16
DSA_KV_HEADS = 4
DSA_GROUP = DSA_HEADS // DSA_KV_HEADS
DSA_HEAD_DIM = 64
IDX_HEADS = 8
IDX_DIM = 64
TOPK_MAX = 256
XA_HEADS = 4
XA_HEAD_DIM = D_MODEL // XA_HEADS
FFN_HIDDEN = -(-8 * D_MODEL // (3 * 256)) * 256
REL_BUCKETS = 32
REL_MAX_DIST = 128
ROPE_BASE = 10000.0
LN_EPS = 1e-5
RMS_EPS = 1e-6
ALPHA = (2 * DEPTH) ** 0.25

LANES = 128
HALF = LANES // 2
VMEM_LIMIT = 56 * 1024 * 1024

LOG2E = math.log2(math.e)
NEG_BIG = -1e30
F32 = jnp.float32
BF16 = jnp.bfloat16
I32 = jnp.int32
_NT = (((1,), (1,)), ((), ()))


def _dot(a, b):
    return jnp.dot(a, b, preferred_element_type=F32)


def _dot_nt(a, b):
    return lax.dot_general(a, b, _NT, preferred_element_type=F32)


def _layer_norm(y, g, b):
    mu = jnp.mean(y, axis=-1, keepdims=True)
    yc = y - mu
    var = jnp.mean(yc * yc, axis=-1, keepdims=True)
    return yc * lax.rsqrt(var + LN_EPS) * g + b


def _rms_norm(y, g):
    return y * lax.rsqrt(jnp.mean(y * y, axis=-1, keepdims=True) + RMS_EPS) * g


def _params(*sem):
    return pltpu.CompilerParams(dimension_semantics=sem, vmem_limit_bytes=VMEM_LIMIT)


def _const_spec(shape):
    nd = len(shape)
    return pl.BlockSpec(shape, lambda *_: (0,) * nd)


def _take_cols(w, idx):
    idx = np.asarray(idx)
    wp = jnp.concatenate([w, jnp.zeros((w.shape[0], 1), w.dtype)], axis=1)
    return wp[:, np.where(idx < 0, w.shape[1], idx)]


_R = MLA_ROPE // 2
_MLA_W = MLA_Q_RANK + MLA_KV_RANK


def _mla_proj_kernel(x_ref, pos_ref, win_ref, qg_ref, kvg_ref, wuq_ref, wuk_ref, wuv_ref,
                     freq_ref, sgn_ref, q_ref, k_ref, v_ref):
    xb = x_ref[0].astype(BF16)
    h = _dot(xb, win_ref[...])
    cq = _rms_norm(h[:, :MLA_Q_RANK], qg_ref[...]).astype(BF16)
    ckv = _rms_norm(h[:, MLA_Q_RANK:_MLA_W], kvg_ref[...]).astype(BF16)
    kr = h[:, _MLA_W:]
    ang = pos_ref[0].astype(F32) * freq_ref[...]
    cos = jnp.cos(ang)
    sin = jnp.sin(ang) * sgn_ref[...]

    def rope(t):
        return t * cos + pltpu.roll(t, HALF, 1) * sin

    kr = rope(kr)
    q_all = _dot(cq, wuq_ref[...])
    k_all = _dot(ckv, wuk_ref[...])
    v_all = _dot(ckv, wuv_ref[...])
    qscale = (MLA_NOPE + MLA_ROPE) ** -0.5 * LOG2E
    ones_hi = (lax.broadcasted_iota(I32, (1, LANES), 1) >= HALF).astype(F32)
    for hd in range(MLA_HEADS):
        sl = slice(hd * LANES, (hd + 1) * LANES)
        q_ref[0, hd] = (rope(q_all[:, sl]) * qscale).astype(BF16)
        k_ref[0, hd] = (k_all[:, sl] + kr).astype(BF16)
        v_ref[0, hd] = (v_all[:, sl] + ones_hi).astype(BF16)


def _mla_proj(x, pos3, w_in, q_norm, w_uq, kv_norm, w_ukv, tm):
    b, s, _ = x.shape
    hq = MLA_NOPE + MLA_ROPE
    in_idx = list(range(_MLA_W)) + [_MLA_W + j for j in range(_R)] + [-1] * (HALF - _R) \
        + [_MLA_W + _R + j for j in range(_R)] + [-1] * (HALF - _R)
    q_idx, k_idx, v_idx = [], [], []
    n_lo = HALF - _R
    n_hi = MLA_NOPE - n_lo
    tail = [-1] * (HALF - _R - n_hi)
    for hd in range(MLA_HEADS):
        qb = hd * hq
        q_idx += [qb + MLA_NOPE + j for j in range(_R)] + [qb + j for j in range(n_lo)] \
            + [qb + MLA_NOPE + _R + j for j in range(_R)] + [qb + n_lo + j for j in range(n_hi)] + tail
        kb = hd * (MLA_NOPE + MLA_V)
        k_idx += [-1] * _R + [kb + j for j in range(n_lo)] + [-1] * _R + [kb + n_lo + j for j in range(n_hi)] + tail
        v_idx += [kb + MLA_NOPE + j for j in range(MLA_V)] + [-1] * (LANES - MLA_V)
    win = _take_cols(w_in, in_idx).astype(BF16)
    wuq = _take_cols(w_uq, q_idx).astype(BF16)
    wuk = _take_cols(w_ukv, k_idx).astype(BF16)
    wuv = _take_cols(w_ukv, v_idx).astype(BF16)
    inv_freq = ROPE_BASE ** (-jnp.arange(0, MLA_ROPE, 2, dtype=F32) / MLA_ROPE)
    zpad = jnp.zeros((HALF - _R,), F32)
    freq = jnp.concatenate([inv_freq, zpad, inv_freq, zpad])[None]
    sgn = jnp.concatenate([-jnp.ones((_R,), F32), zpad, jnp.ones((_R,), F32), zpad])[None]
    nq = MLA_HEADS * LANES
    grid = (b, s // tm)
    out_shape = (jax.ShapeDtypeStruct((b, MLA_HEADS, s, LANES), BF16),
                 jax.ShapeDtypeStruct((b, MLA_HEADS, s, LANES), BF16),
                 jax.ShapeDtypeStruct((b, MLA_HEADS, s, LANES), BF16))
    return pl.pallas_call(
        _mla_proj_kernel, out_shape=out_shape, grid=grid,
        in_specs=[pl.BlockSpec((1, tm, D_MODEL), lambda bi, i: (bi, i, 0)),
                  pl.BlockSpec((1, tm, 1), lambda bi, i: (bi, i, 0)),
                  _const_spec((D_MODEL, _MLA_W + LANES)),
                  _const_spec((1, MLA_Q_RANK)), _const_spec((1, MLA_KV_RANK)),
                  _const_spec((MLA_Q_RANK, nq)), _const_spec((MLA_KV_RANK, nq)),
                  _const_spec((MLA_KV_RANK, nq)),
                  _const_spec((1, LANES)), _const_spec((1, LANES))],
        out_specs=(pl.BlockSpec((1, MLA_HEADS, tm, LANES), lambda bi, i: (bi, 0, i, 0)),
                   pl.BlockSpec((1, MLA_HEADS, tm, LANES), lambda bi, i: (bi, 0, i, 0)),
                   pl.BlockSpec((1, MLA_HEADS, tm, LANES), lambda bi, i: (bi, 0, i, 0))),
        compiler_params=_params("parallel", "parallel"), name="mla_proj",
    )(x, pos3, win, q_norm[None], kv_norm[None], wuq, wuk, wuv, freq, sgn)


def _flash_kernel(q_ref, k_ref, v_ref, o_ref, m_sc, acc_sc, s_sc, *, t, nh):
    i = pl.program_id(2)
    tk = t // 2
    row = lax.broadcasted_iota(I32, (t, tk), 0)
    col = lax.broadcasted_iota(I32, (t, tk), 1)
    m_sc[...] = jnp.full(m_sc.shape, NEG_BIG, F32)
    acc_sc[...] = jnp.zeros(acc_sc.shape, F32)

    def scores(j, hh):
        kc = k_ref[0, hh, pl.ds(pl.multiple_of(j * tk, tk), tk), :]
        return _dot_nt(q_ref[0, hh], kc)

    def step(j, slot, own=None, last=False):
        for hh in range(nh):
            sc = s_sc[slot, hh]
            if own is not None:
                sc = jnp.where(col + own * tk <= row, sc, NEG_BIG)
            if not last:
                s_sc[1 - slot, hh] = scores(j + 1, hh)
            m_prev = m_sc[hh]
            m_new = jnp.maximum(m_prev, jnp.max(sc, axis=-1, keepdims=True))
            a = jnp.exp2(m_prev - m_new)
            p = jnp.concatenate([jnp.exp2(sc[:, c * LANES:(c + 1) * LANES] - m_new)
                                 for c in range(tk // LANES)], axis=1).astype(BF16)
            vc = v_ref[0, hh, pl.ds(pl.multiple_of(j * tk, tk), tk), :]
            acc_sc[hh] = a * acc_sc[hh] + _dot(p, vc)
            m_sc[hh] = m_new

    for hh in range(nh):
        s_sc[0, hh] = scores(0, hh)

    def pair(u, c):
        step(2 * u, 0)
        step(2 * u + 1, 1)
        return c

    lax.fori_loop(0, i, pair, 0)
    step(2 * i, 0, own=0)
    step(2 * i + 1, 1, own=1, last=True)

    lane = lax.broadcasted_iota(I32, (t, LANES), 1)
    for pr in range(nh // 2):
        a0 = acc_sc[2 * pr]
        a1 = acc_sc[2 * pr + 1]
        o0 = a0 / a0[:, HALF:HALF + 1]
        o1 = a1 / a1[:, HALF:HALF + 1]
        o_ref[0, :, pr * LANES:(pr + 1) * LANES] = \
            jnp.where(lane < HALF, o0, pltpu.roll(o1, HALF, 1)).astype(BF16)


MLA_HEADS_PER_STEP = 4


def _flash_mla(q, k, v, t):
    b, h, s, _ = q.shape
    nh = MLA_HEADS_PER_STEP
    return pl.pallas_call(
        functools.partial(_flash_kernel, t=t, nh=nh),
        out_shape=jax.ShapeDtypeStruct((b, s, h * MLA_V), BF16),
        grid=(b, h // nh, s // t),
        in_specs=[pl.BlockSpec((1, nh, t, LANES), lambda bi, p, i: (bi, p, i, 0)),
                  pl.BlockSpec((1, nh, s, LANES), lambda bi, p, i: (bi, p, 0, 0)),
                  pl.BlockSpec((1, nh, s, LANES), lambda bi, p, i: (bi, p, 0, 0))],
        out_specs=pl.BlockSpec((1, t, nh * MLA_V), lambda bi, p, i: (bi, i, p)),
        scratch_shapes=[pltpu.VMEM((nh, t, LANES), F32), pltpu.VMEM((nh, t, LANES), F32),
                        pltpu.VMEM((2, nh, t, t // 2), F32)],
        compiler_params=_params("parallel", "parallel", "arbitrary"), name="mla_flash",
    )(q, k, v)


_DQ0 = 0
_DK0 = DSA_HEADS * LANES
_DV0 = _DK0 + DSA_KV_HEADS * LANES
_DI0 = _DV0 + DSA_KV_HEADS * LANES
_DW0 = _DI0 + IDX_HEADS * LANES
_DN = _DW0 + LANES


def _dsa_proj_kernel(x_ref, w_ref, qhot_ref, krow_ref, wsc_ref, qa_ref, ka_ref, v2_ref, qi_ref, kw_ref, wf_ref):
    xb = x_ref[0].astype(BF16)
    h = _dot(xb, w_ref[...])
    qscale = DSA_HEAD_DIM ** -0.5 * LOG2E
    for hd in range(DSA_HEADS):
        sl = slice(_DQ0 + hd * LANES, _DQ0 + (hd + 1) * LANES)
        qa_ref[0, hd] = (h[:, sl] * qscale + qhot_ref[hd:hd + 1, :]).astype(BF16)
    ones_hi = (lax.broadcasted_iota(I32, (1, LANES), 1) >= HALF).astype(F32)
    for g in range(DSA_KV_HEADS):
        ka_ref[0, g] = (h[:, _DK0 + g * LANES:_DK0 + (g + 1) * LANES] + krow_ref[g:g + 1, :]).astype(BF16)
        v2_ref[0, g] = (h[:, _DV0 + g * LANES:_DV0 + (g + 1) * LANES] + ones_hi).astype(BF16)
    for hd in range(IDX_HEADS):
        qi_ref[0, hd] = (h[:, _DI0 + hd * LANES:_DI0 + (hd + 1) * LANES] * (IDX_DIM ** -0.5)).astype(BF16)
    kw = h[:, _DW0:]
    kw_ref[0] = kw.astype(BF16)
    wf_ref[0] = kw * wsc_ref[...]


def _dsa_proj(x, w_in, rel_bias, tm):
    b, s, _ = x.shape
    o1 = DSA_HEADS * DSA_HEAD_DIM
    o2 = o1 + DSA_KV_HEADS * DSA_HEAD_DIM
    o3 = o2 + DSA_KV_HEADS * DSA_HEAD_DIM
    o4 = o3 + IDX_HEADS * IDX_DIM
    o5 = o4 + IDX_DIM
    idx = []
    for hd in range(DSA_HEADS):
        idx += [hd * DSA_HEAD_DIM + j for j in range(DSA_HEAD_DIM)] + [-1] * HALF
    for g in range(DSA_KV_HEADS):
        idx += [o1 + g * DSA_HEAD_DIM + j for j in range(DSA_HEAD_DIM)] + [-1] * HALF
    for g in range(DSA_KV_HEADS):
        idx += [o2 + g * DSA_HEAD_DIM + j for j in range(DSA_HEAD_DIM)] + [-1] * HALF
    for hd in range(IDX_HEADS):
        idx += [o3 + hd * IDX_DIM + j for j in range(IDX_DIM)] + [-1] * HALF
    idx += [o4 + j for j in range(IDX_DIM)] + [o5 + j for j in range(IDX_HEADS)] + [-1] * (HALF - IDX_HEADS)
    assert len(idx) == _DN
    w = _take_cols(w_in, idx).astype(BF16)
    cfar = rel_bias[REL_BUCKETS - 1] * LOG2E
    chi = cfar.astype(BF16).astype(F32)
    clo = (cfar - chi).astype(BF16).astype(F32)
    qhot = np.zeros((DSA_HEADS, LANES), np.float32)
    for hd in range(DSA_HEADS):
        qhot[hd, HALF + hd % DSA_GROUP] = 1.0
        qhot[hd, HALF + DSA_GROUP + hd % DSA_GROUP] = 1.0
    zpad = jnp.zeros((DSA_KV_HEADS, LANES - HALF - 2 * DSA_GROUP), F32)
    krow = jnp.concatenate([jnp.zeros((DSA_KV_HEADS, HALF), F32), chi.reshape(DSA_KV_HEADS, DSA_GROUP),
                            clo.reshape(DSA_KV_HEADS, DSA_GROUP), zpad], axis=1)
    wsc = np.zeros((1, LANES), np.float32)
    wsc[0, HALF:HALF + IDX_HEADS] = IDX_HEADS ** -0.5
    hspec = lambda n: pl.BlockSpec((1, n, tm, LANES), lambda bi, i: (bi, 0, i, 0))
    tspec = pl.BlockSpec((1, tm, LANES), lambda bi, i: (bi, i, 0))
    out_shape = (jax.ShapeDtypeStruct((b, DSA_HEADS, s, LANES), BF16),
                 jax.ShapeDtypeStruct((b, DSA_KV_HEADS, s, LANES), BF16),
                 jax.ShapeDtypeStruct((b, DSA_KV_HEADS, s, LANES), BF16),
                 jax.ShapeDtypeStruct((b, IDX_HEADS, s, LANES), BF16),
                 jax.ShapeDtypeStruct((b, s, LANES), BF16),
                 jax.ShapeDtypeStruct((b, s, LANES), F32))
    return pl.pallas_call(
        _dsa_proj_kernel, out_shape=out_shape, grid=(b, s // tm),
        in_specs=[pl.BlockSpec((1, tm, D_MODEL), lambda bi, i: (bi, i, 0)),
                  _const_spec((D_MODEL, _DN)), _const_spec((DSA_HEADS, LANES)),
                  _const_spec((DSA_KV_HEADS, LANES)), _const_spec((1, LANES))],
        out_specs=(hspec(DSA_HEADS), hspec(DSA_KV_HEADS), hspec(DSA_KV_HEADS), hspec(IDX_HEADS), tspec, tspec),
        compiler_params=_params("parallel", "parallel"), name="dsa_proj",
    )(x, w, jnp.asarray(qhot), krow, jnp.asarray(wsc))


def _t5_tiles_kernel(rb_ref, o_ref):
    d = pl.program_id(0)
    hd = pl.program_id(1)
    row = lax.broadcasted_iota(I32, (LANES, LANES), 0)
    col = lax.broadcasted_iota(I32, (LANES, LANES), 1)
    n = jnp.maximum(d * LANES + row - col, 0)
    max_exact = REL_BUCKETS // 2
    nf = jnp.maximum(n, 1).astype(F32)
    large = max_exact + (jnp.log(nf / max_exact) / math.log(REL_MAX_DIST / max_exact)
                         * (REL_BUCKETS - max_exact)).astype(I32)
    large = jnp.minimum(large, REL_BUCKETS - 1)
    bucket = jnp.where(n < max_exact, n, large)
    far = rb_ref[REL_BUCKETS - 1, hd]
    acc = jnp.zeros((LANES, LANES), F32)
    for bk in range(REL_BUCKETS - 1):
        acc = jnp.where(bucket == bk, rb_ref[bk, hd] - far, acc)
    o_ref[0, 0] = acc * LOG2E


def _t5_tiles(rel_bias):
    return pl.pallas_call(
        _t5_tiles_kernel,
        out_shape=jax.ShapeDtypeStruct((2, DSA_HEADS, LANES, LANES), F32),
        grid=(2, DSA_HEADS),
        in_specs=[pl.BlockSpec(memory_space=pltpu.SMEM)],
        out_specs=pl.BlockSpec((1, 1, LANES, LANES), lambda d, h: (d, h, 0, 0)),
        compiler_params=_params("parallel", "parallel"), name="t5_tiles",
    )(rel_bias)


_SENT_KEY = int(np.int32(np.uint32(0xFF800000) ^ np.uint32(0x7FFFFFFF)))
_HI16 = -65536
_MIN_NORMAL_BITS = 0x00800000
_BITS_ALWAYS = 23
_INT_MIN = -(2 ** 31)


def _dsa_attn_kernel(qa_ref, qi_ref, wf_ref, ka_ref, v_ref, kw_ref, tb_ref, o_ref,
                     keyt_sc, kb_sc, madd_sc, s_sc, m_sc, acc_sc, cut_sc, thr_sc, cge_sc, tcnt_sc,
                     *, t, topk, s_len):
    i = pl.program_id(1)
    nsub = t // LANES
    kidx = lax.broadcasted_iota(I32, (t, t), 0)
    qidx = lax.broadcasted_iota(I32, (t, t), 1)

    qi = qi_ref[0].reshape(IDX_HEADS * t, LANES)
    wt = wf_ref[0].T
    wrows = [wt[HALF + hd:HALF + hd + 1, :] for hd in range(IDX_HEADS)]

    def scores_t(j0, w):
        kc = kw_ref[0, pl.ds(pl.multiple_of(j0 * t, t), w * t), :]
        d = _dot_nt(kc, qi)
        sc = wrows[0] * jnp.maximum(d[:, 0:t], 0.0)
        for hd in range(1, IDX_HEADS):
            sc = sc + wrows[hd] * jnp.maximum(d[:, hd * t:(hd + 1) * t], 0.0)
        return sc

    def store_keys(j, sc, causal):
        if causal:
            sc = jnp.where(kidx <= qidx, sc, -jnp.inf)
        bits = pltpu.bitcast(sc + 0.0, I32)
        keyt_sc[j] = bits ^ ((bits >> 31) & 0x7FFFFFFF)
        kb_sc[j] = pltpu.bitcast(bits & _HI16, F32).astype(BF16)

    def p1(u, c):
        sc = scores_t(2 * u, 2)
        store_keys(2 * u, sc[0:t], False)
        store_keys(2 * u + 1, sc[t:2 * t], False)
        return c

    lax.fori_loop(0, i // 2, p1, 0)

    @pl.when(i % 2 == 1)
    def _():
        sc = scores_t(i - 1, 2)
        store_keys(i - 1, sc[0:t], False)
        store_keys(i, sc[t:2 * t], True)

    @pl.when(i % 2 == 0)
    def _():
        store_keys(i, scores_t(i, 1), True)

    cacc = 32

    kf = float(topk)

    def bisect(nchunks):
        def count_ge(cand):
            acc = jnp.zeros((cacc, t), F32)
            for j in range(nchunks):
                c = jnp.where(keyt_sc[j] >= cand, 1.0, 0.0)
                acc = acc + jnp.sum(c.reshape(t // cacc, cacc, t), axis=0)
            return jnp.sum(acc, axis=0, keepdims=True)

        one_b = jnp.ones((), BF16)
        zero_b = jnp.zeros((), BF16)

        def count_ge16(cand):
            cbits = (cand ^ ((cand >> 31) & 0x7FFFFFFF)) & _HI16
            cbits = jnp.where((cand > 0) & (cand < _MIN_NORMAL_BITS), _MIN_NORMAL_BITS, cbits)
            cf = lax.bitcast_convert_type(cbits, F32)
            cb = jnp.broadcast_to(cf, (16, t)).astype(BF16)
            cb = jnp.broadcast_to(cb[None], (t // 16, 16, t)).reshape(t, t)
            acc = jnp.zeros((cacc, t), BF16)
            for j in range(nchunks):
                c = jnp.where(kb_sc[j] >= cb, one_b, zero_b)
                for k in range(t // cacc):
                    acc = acc + c[k * cacc:(k + 1) * cacc]
            return jnp.sum(acc.astype(F32), axis=0, keepdims=True)

        cnt0 = count_ge16(jnp.zeros((1, t), I32))
        nonneg = cnt0 >= kf
        prefix0 = jnp.where(nonneg, 0, _INT_MIN).astype(I32)
        cge0 = jnp.where(nonneg, cnt0, float(nchunks * t))

        def bit_body(count_fn, bi, carry):
            prefix, cge = carry
            cand = prefix + lax.shift_left(jnp.int32(1), 30 - bi)
            cnt = count_fn(cand)
            take = cnt >= kf
            return jnp.where(take, cand, prefix), jnp.where(take, cnt, cge)

        carry = lax.fori_loop(0, 15, functools.partial(bit_body, count_ge16), (prefix0, cge0))
        carry = lax.fori_loop(15, _BITS_ALWAYS, functools.partial(bit_body, count_ge), carry)

        def more(state):
            bi, prefix, cge = state
            open_ = (cge != kf) & jnp.logical_not(nonneg & (prefix == 0))
            return (bi < 31) & (jnp.max(jnp.where(open_, 1.0, 0.0)) > 0.0)

        def one_bit(state):
            bi, prefix, cge = state
            return (bi + 1,) + bit_body(count_ge, bi, (prefix, cge))

        _, thr, cge = lax.while_loop(more, one_bit, (jnp.int32(_BITS_ALWAYS),) + carry)
        thr_sc[...] = jnp.broadcast_to(thr, thr_sc.shape)
        cge_sc[...] = jnp.broadcast_to(cge, cge_sc.shape)

    for ii in range(s_len // t):
        pl.when(i == ii)(functools.partial(bisect, ii + 1))
    thr = thr_sc[0:1, :]
    cnt_ge = cge_sc[0:1, :]
    excess = jnp.where((cnt_ge > kf) & (thr > _SENT_KEY), 1.0, 0.0)
    cut_sc[...] = jnp.full(cut_sc.shape, s_len, I32)

    @pl.when(jnp.max(excess) > 0.0)
    def _():
        def sweep(j, gt):
            kc = keyt_sc[j]
            tie = jnp.where(kc == thr, 1.0, 0.0)
            tcnt_sc[j] = jnp.broadcast_to(jnp.sum(tie, axis=0, keepdims=True), (8, t))
            g = jnp.where(kc > thr, 1.0, 0.0)
            return gt + jnp.sum(g.reshape(t // cacc, cacc, t), axis=0)

        gt = lax.fori_loop(0, i + 1, sweep, jnp.zeros((cacc, t), F32))
        need = kf - jnp.sum(gt, axis=0, keepdims=True)

        def scan(j, carry):
            run, cstar, base = carry
            new = run + tcnt_sc[j, 0:1, :]
            hit = (run < need) & (new >= need)
            return new, jnp.where(hit, j, cstar), jnp.where(hit, run, base)

        zrow = jnp.zeros((1, t), F32)
        _, cstar, base = lax.fori_loop(0, i + 1, scan, (zrow, jnp.zeros((1, t), I32), zrow))

        def gather(j, tile):
            return jnp.where(cstar == j, keyt_sc[j], tile)

        tile = lax.fori_loop(0, i + 1, gather, jnp.full((t, t), _SENT_KEY, I32))
        tie = jnp.where(tile == thr, 1.0, 0.0)
        need_in = need - base
        rbits = max(1, (t - 1).bit_length())

        def row_body(bi, r):
            cand = r + lax.shift_left(jnp.int32(1), rbits - 1 - bi)
            cnt = jnp.sum(jnp.where(kidx < cand, tie, 0.0), axis=0, keepdims=True)
            return jnp.where(cnt < need_in, cand, r)

        r = lax.fori_loop(0, rbits, row_body, jnp.zeros((1, t), I32))
        cut_sc[...] = jnp.broadcast_to(cstar * t + r, cut_sc.shape)

    cut = cut_sc[0:1, :]

    def to_mask(j, causal):
        kc = keyt_sc[j]
        sel = (kc > thr) | ((kc == thr) & (kidx + j * t <= cut))
        if causal:
            sel = sel & (kidx <= qidx)
        return jnp.where(sel, 0.0, NEG_BIG).T

    m_sc[...] = jnp.full(m_sc.shape, NEG_BIG, F32)
    acc_sc[...] = jnp.zeros(acc_sc.shape, F32)
    gt = DSA_GROUP * t
    near_delta = {"prev": nsub, "diag": 0}

    def attend(j0, parts):
        w = len(parts)
        start = pl.multiple_of(j0 * t, t)

        def qk(g):
            q = qa_ref[0, g * DSA_GROUP:(g + 1) * DSA_GROUP].reshape(gt, LANES)
            return _dot_nt(q, ka_ref[0, g, pl.ds(start, w * t), :])

        s_sc[0, :, 0:w * t] = qk(0)
        for pi, part in enumerate(parts):
            madd_sc[pi] = to_mask(j0 + pi, part == "diag")
        for g in range(DSA_KV_HEADS):
            if g + 1 < DSA_KV_HEADS:
                s_sc[(g + 1) % 2, :, 0:w * t] = qk(g + 1)
            slot = g % 2
            p_rows, a_rows = [], []
            for r in range(DSA_GROUP):
                hd = g * DSA_GROUP + r
                for rs in range(nsub):
                    r0 = r * t + rs * LANES
                    blocks = []
                    for cb in range(w * nsub):
                        part, cs = parts[cb // nsub], cb % nsub
                        blk = s_sc[slot, r0:r0 + LANES, cb * LANES:(cb + 1) * LANES] \
                            + madd_sc[cb // nsub, rs * LANES:(rs + 1) * LANES, cs * LANES:(cs + 1) * LANES]
                        if part in near_delta and near_delta[part] + rs - cs in (0, 1):
                            blk = blk + tb_ref[near_delta[part] + rs - cs, hd]
                        blocks.append(blk)
                    m_prev = m_sc[g, r0:r0 + LANES]
                    m_blk = blocks[0]
                    for blk in blocks[1:]:
                        m_blk = jnp.maximum(m_blk, blk)
                    m_new = jnp.maximum(m_prev, jnp.max(m_blk, axis=-1, keepdims=True))
                    a_rows.append(jnp.exp2(m_prev - m_new))
                    p_rows.append(jnp.concatenate([jnp.exp2(blk - m_new).astype(BF16) for blk in blocks], axis=1))
                    m_sc[g, r0:r0 + LANES] = m_new
            p = jnp.concatenate(p_rows, axis=0)
            a = jnp.concatenate(a_rows, axis=0)
            vc = v_ref[0, g, pl.ds(start, w * t), :]
            acc_sc[g] = a * acc_sc[g] + _dot(p, vc)

    @pl.when(i == 0)
    def _():
        attend(0, ("diag",))

    @pl.when(i >= 1)
    def _():
        nfar = i - 1

        def far_pair(u, c):
            attend(2 * u, ("far", "far"))
            return c

        lax.fori_loop(0, nfar // 2, far_pair, 0)

        @pl.when(nfar % 2 == 1)
        def _():
            attend(i - 2, ("far",))

        attend(i - 1, ("prev", "diag"))

    lane = lax.broadcasted_iota(I32, (t, LANES), 1)
    for g in range(DSA_KV_HEADS):
        acc = acc_sc[g]
        o = acc / acc[:, HALF:HALF + 1]
        for pr in range(DSA_GROUP // 2):
            lo = o[(2 * pr) * t:(2 * pr + 1) * t]
            hi = o[(2 * pr + 1) * t:(2 * pr + 2) * t]
            cblk = g * (DSA_GROUP // 2) + pr
            o_ref[0, :, cblk * LANES:(cblk + 1) * LANES] = \
                jnp.where(lane < HALF, lo, pltpu.roll(hi, HALF, 1)).astype(BF16)


def _dsa_attn(qa, ka, v, qi, kw, wf, tb, t, topk):
    b, _, s, _ = qa.shape
    nch = s // t
    gt = DSA_GROUP * t
    kern = functools.partial(_dsa_attn_kernel, t=t, topk=topk, s_len=s)
    return pl.pallas_call(
        kern, out_shape=jax.ShapeDtypeStruct((b, s, DSA_HEADS * DSA_HEAD_DIM), BF16),
        grid=(b, nch),
        in_specs=[pl.BlockSpec((1, DSA_HEADS, t, LANES), lambda bi, i: (bi, 0, i, 0)),
                  pl.BlockSpec((1, IDX_HEADS, t, LANES), lambda bi, i: (bi, 0, i, 0)),
                  pl.BlockSpec((1, t, LANES), lambda bi, i: (bi, i, 0)),
                  pl.BlockSpec((1, DSA_KV_HEADS, s, LANES), lambda bi, i: (bi, 0, 0, 0)),
                  pl.BlockSpec((1, DSA_KV_HEADS, s, LANES), lambda bi, i: (bi, 0, 0, 0)),
                  pl.BlockSpec((1, s, LANES), lambda bi, i: (bi, 0, 0)),
                  _const_spec((2, DSA_HEADS, LANES, LANES))],
        out_specs=pl.BlockSpec((1, t, DSA_HEADS * DSA_HEAD_DIM), lambda bi, i: (bi, i, 0)),
        scratch_shapes=[pltpu.VMEM((nch, t, t), I32),
                        pltpu.VMEM((nch, t, t), BF16),
                        pltpu.VMEM((2, t, t), F32),
                        pltpu.VMEM((2, gt, 2 * t), F32),
                        pltpu.VMEM((DSA_KV_HEADS, gt, LANES), F32),
                        pltpu.VMEM((DSA_KV_HEADS, gt, LANES), F32),
                        pltpu.VMEM((8, t), I32),
                        pltpu.VMEM((8, t), I32),
                        pltpu.VMEM((8, t), F32),
                        pltpu.VMEM((nch, 8, t), F32)],
        compiler_params=_params("parallel", "arbitrary"), name="dsa_attn",
    )(qa, qi, wf, ka, v, kw, tb)


def _mm_kernel(a_ref, w_ref, o_ref):
    o_ref[...] = _dot(a_ref[...].astype(BF16), w_ref[...]).astype(BF16)


def _mm(a, w, tm):
    m, k = a.shape
    n = w.shape[1]
    return pl.pallas_call(
        _mm_kernel, out_shape=jax.ShapeDtypeStruct((m, n), BF16), grid=(m // tm,),
        in_specs=[pl.BlockSpec((tm, k), lambda i: (i, 0)), _const_spec((k, n))],
        out_specs=pl.BlockSpec((tm, n), lambda i: (i, 0)),
        compiler_params=_params("parallel"), name="mem_kv",
    )(a, w)


def _post_kernel(a_ref, x_ref, wo_ref, wq_ref, kv_ref, wxo_ref, g_ref, b_ref, o_ref, oc_sc):
    y = _dot(a_ref[0], wo_ref[...])
    x1 = _layer_norm(ALPHA * x_ref[0] + y, g_ref[0:1, :], b_ref[0:1, :])
    q = (_dot(x1.astype(BF16), wq_ref[...]) * (XA_HEAD_DIM ** -0.5 * LOG2E)).astype(BF16)
    for hd in range(XA_HEADS):
        sl = slice(hd * XA_HEAD_DIM, (hd + 1) * XA_HEAD_DIM)
        sc = _dot_nt(q[:, sl], kv_ref[0, :, sl])
        m = jnp.max(sc, axis=-1, keepdims=True)
        p = jnp.exp2(sc - m)
        l = jnp.sum(p, axis=-1, keepdims=True)
        vsl = slice(D_MODEL + hd * XA_HEAD_DIM, D_MODEL + (hd + 1) * XA_HEAD_DIM)
        oc_sc[:, sl] = (_dot(p.astype(BF16), kv_ref[0, :, vsl]) / l).astype(BF16)
    y2 = _dot(oc_sc[...], wxo_ref[...])
    o_ref[0] = _layer_norm(ALPHA * x1 + y2, g_ref[1:2, :], b_ref[1:2, :])


def _post(a, x, w_o, w_q, kv, w_xo, g, bta, tm):
    b, s, _ = x.shape
    mlen = kv.shape[1]
    tok = lambda bi, i: (bi, i, 0)
    return pl.pallas_call(
        _post_kernel, out_shape=jax.ShapeDtypeStruct((b, s, D_MODEL), F32), grid=(b, s // tm),
        in_specs=[pl.BlockSpec((1, tm, D_MODEL), tok), pl.BlockSpec((1, tm, D_MODEL), tok),
                  _const_spec((D_MODEL, D_MODEL)), _const_spec((D_MODEL, D_MODEL)),
                  pl.BlockSpec((1, mlen, 2 * D_MODEL), lambda bi, i: (bi, 0, 0)),
                  _const_spec((D_MODEL, D_MODEL)), _const_spec((3, D_MODEL)), _const_spec((3, D_MODEL))],
        out_specs=pl.BlockSpec((1, tm, D_MODEL), tok),
        scratch_shapes=[pltpu.VMEM((tm, D_MODEL), BF16)],
        compiler_params=_params("parallel", "parallel"), name="post_attn",
    )(a, x, w_o, w_q, kv, w_xo, g, bta)


FFN_CHUNK = 256


def _ffn_kernel(x_ref, wi_ref, wd_ref, g_ref, b_ref, o_ref, act_sc):
    x = x_ref[...]
    xb = x.astype(BF16)
    for c in range(FFN_HIDDEN // FFN_CHUNK):
        sl = slice(c * FFN_CHUNK, (c + 1) * FFN_CHUNK)
        gte = _dot(xb, wi_ref[:, sl])
        up = _dot(xb, wi_ref[:, FFN_HIDDEN + c * FFN_CHUNK:FFN_HIDDEN + (c + 1) * FFN_CHUNK])
        act_sc[:, sl] = (gte * jax.nn.sigmoid(gte) * up).astype(BF16)
    y = _dot(act_sc[...], wd_ref[...])
    o_ref[...] = _layer_norm(ALPHA * x + y, g_ref[2:3, :], b_ref[2:3, :])


def _ffn(x2d, w_in, w_down, g, bta, tm):
    n = x2d.shape[0]
    return pl.pallas_call(
        _ffn_kernel, out_shape=jax.ShapeDtypeStruct((n, D_MODEL), F32), grid=(n // tm,),
        in_specs=[pl.BlockSpec((tm, D_MODEL), lambda i: (i, 0)),
                  _const_spec((D_MODEL, 2 * FFN_HIDDEN)), _const_spec((FFN_HIDDEN, D_MODEL)),
                  _const_spec((3, D_MODEL)), _const_spec((3, D_MODEL))],
        out_specs=pl.BlockSpec((tm, D_MODEL), lambda i: (i, 0)),
        scratch_shapes=[pltpu.VMEM((tm, FFN_HIDDEN), BF16)],
        compiler_params=_params("parallel"), name="ffn",
    )(x2d, w_in.astype(BF16), w_down.astype(BF16), g, bta)


def _tile(n, pref):
    t = min(n, pref)
    assert n % t == 0, (n, pref)
    return t


def kernel(x, mem, positions, rel_bias, mla_w_in, mla_q_norm, mla_w_uq, mla_kv_norm, mla_w_ukv, mla_w_o,
           dsa_w_in, dsa_w_o, xa_w_q, xa_w_kv, xa_w_o, ffn_w_in, ffn_w_down, ln_g, ln_b):
    b, s, d = x.shape
    assert d == D_MODEL and s % LANES == 0
    mlen = mem.shape[1]
    topk = min(TOPK_MAX, s // 4)
    t_attn = _tile(s, 256)
    tm_proj = _tile(s, 512)
    tm_post = _tile(s, 1024)
    tm_ffn = _tile(b * s, 1024)
    pos3 = positions.reshape(b, s, 1)
    tb = _t5_tiles(rel_bias)
    for i in range(DEPTH):
        j = i // 2
        if i % 2 == 0:
            q, k, v = _mla_proj(x, pos3, mla_w_in[j], mla_q_norm[j], mla_w_uq[j], mla_kv_norm[j],
                                mla_w_ukv[j], tm_proj)
            a = _flash_mla(q, k, v, _tile(s, 1024))
            w_o = mla_w_o[j]
        else:
            qa, ka, v2, qi, kw, wf = _dsa_proj(x, dsa_w_in[j], rel_bias, tm_proj)
            a = _dsa_attn(qa, ka, v2, qi, kw, wf, tb, t_attn, topk)
            w_o = dsa_w_o[j]
        kv = _mm(mem.reshape(b * mlen, d), xa_w_kv[i].astype(BF16), _tile(b * mlen, 512)).reshape(b, mlen, 2 * d)
        x = _post(a, x, w_o.astype(BF16), xa_w_q[i].astype(BF16), kv, xa_w_o[i].astype(BF16),
                  ln_g[i], ln_b[i], tm_post)
        x = _ffn(x.reshape(b * s, d), ffn_w_in[i], ffn_w_down[i], ln_g[i], ln_b[i], tm_ffn).reshape(b, s, d)
    return x
```

```python
import functools
import math

import numpy as np
import jax
import jax.numpy as jnp
from jax import lax
from jax.experimental import pallas as pl
from jax.experimental.pallas import tpu as pltpu

D_MODEL = 1024
DEPTH = 2
MLA_HEADS = 16
MLA_Q_RANK = 512
MLA_KV_RANK = 256
MLA_NOPE = 64
MLA_ROPE = 32
MLA_V = 64
DSA_HEADS = 16
DSA_KV_HEADS = 4
DSA_GROUP = DSA_HEADS // DSA_KV_HEADS
DSA_HEAD_DIM = 64
IDX_HEADS = 8
IDX_DIM = 64
TOPK_MAX = 256
XA_HEADS = 4
XA_HEAD_DIM = D_MODEL // XA_HEADS
FFN_HIDDEN = -(-8 * D_MODEL // (3 * 256)) * 256
REL_BUCKETS = 32
REL_MAX_DIST = 128
ROPE_BASE = 10000.0
LN_EPS = 1e-5
RMS_EPS = 1e-6
ALPHA = (2 * DEPTH) ** 0.25

LANES = 128
HALF = LANES // 2
VMEM_LIMIT = 56 * 1024 * 1024

LOG2E = math.log2(math.e)
NEG_BIG = -1e30
F32 = jnp.float32
BF16 = jnp.bfloat16
I32 = jnp.int32
_NT = (((1,), (1,)), ((), ()))


def _dot(a, b):
    return jnp.dot(a, b, preferred_element_type=F32)


def _dot_nt(a, b):
    return lax.dot_general(a, b, _NT, preferred_element_type=F32)


def _layer_norm(y, g, b):
    mu = jnp.mean(y, axis=-1, keepdims=True)
    yc = y - mu
    var = jnp.mean(yc * yc, axis=-1, keepdims=True)
    return yc * lax.rsqrt(var + LN_EPS) * g + b


def _rms_norm(y, g):
    return y * lax.rsqrt(jnp.mean(y * y, axis=-1, keepdims=True) + RMS_EPS) * g


def _params(*sem):
    return pltpu.CompilerParams(dimension_semantics=sem, vmem_limit_bytes=VMEM_LIMIT)


def _const_spec(shape):
    nd = len(shape)
    return pl.BlockSpec(shape, lambda *_: (0,) * nd)


def _take_cols(w, idx):
    idx = np.asarray(idx)
    wp = jnp.concatenate([w, jnp.zeros((w.shape[0], 1), w.dtype)], axis=1)
    return wp[:, np.where(idx < 0, w.shape[1], idx)]


_R = MLA_ROPE // 2
_MLA_W = MLA_Q_RANK + MLA_KV_RANK


def _mla_proj_kernel(x_ref, pos_ref, win_ref, qg_ref, kvg_ref, wuq_ref, wuk_ref, wuv_ref,
                     freq_ref, sgn_ref, q_ref, k_ref, v_ref):
    xb = x_ref[0].astype(BF16)
    h = _dot(xb, win_ref[...])
    cq = _rms_norm(h[:, :MLA_Q_RANK], qg_ref[...]).astype(BF16)
    ckv = _rms_norm(h[:, MLA_Q_RANK:_MLA_W], kvg_ref[...]).astype(BF16)
    kr = h[:, _MLA_W:]
    ang = pos_ref[0].astype(F32) * freq_ref[...]
    cos = jnp.cos(ang)
    sin = jnp.sin(ang) * sgn_ref[...]

    def rope(t):
        return t * cos + pltpu.roll(t, HALF, 1) * sin

    kr = rope(kr)
    q_all = _dot(cq, wuq_ref[...])
    k_all = _dot(ckv, wuk_ref[...])
    v_all = _dot(ckv, wuv_ref[...])
    qscale = (MLA_NOPE + MLA_ROPE) ** -0.5 * LOG2E
    ones_hi = (lax.broadcasted_iota(I32, (1, LANES), 1) >= HALF).astype(F32)
    for hd in range(MLA_HEADS):
        sl = slice(hd * LANES, (hd + 1) * LANES)
        q_ref[0, hd] = (rope(q_all[:, sl]) * qscale).astype(BF16)
        k_ref[0, hd] = (k_all[:, sl] + kr).astype(BF16)
        v_ref[0, hd] = (v_all[:, sl] + ones_hi).astype(BF16)


def _mla_proj(x, pos3, w_in, q_norm, w_uq, kv_norm, w_ukv, tm):
    b, s, _ = x.shape
    hq = MLA_NOPE + MLA_ROPE
    in_idx = list(range(_MLA_W)) + [_MLA_W + j for j in range(_R)] + [-1] * (HALF - _R) \
        + [_MLA_W + _R + j for j in range(_R)] + [-1] * (HALF - _R)
    q_idx, k_idx, v_idx = [], [], []
    n_lo = HALF - _R
    n_hi = MLA_NOPE - n_lo
    tail = [-1] * (HALF - _R - n_hi)
    for hd in range(MLA_HEADS):
        qb = hd * hq
        q_idx += [qb + MLA_NOPE + j for j in range(_R)] + [qb + j for j in range(n_lo)] \
            + [qb + MLA_NOPE + _R + j for j in range(_R)] + [qb + n_lo + j for j in range(n_hi)] + tail
        kb = hd * (MLA_NOPE + MLA_V)
        k_idx += [-1] * _R + [kb + j for j in range(n_lo)] + [-1] * _R + [kb + n_lo + j for j in range(n_hi)] + tail
        v_idx += [kb + MLA_NOPE + j for j in range(MLA_V)] + [-1] * (LANES - MLA_V)
    win = _take_cols(w_in, in_idx).astype(BF16)
    wuq = _take_cols(w_uq, q_idx).astype(BF16)
    wuk = _take_cols(w_ukv, k_idx).astype(BF16)
    wuv = _take_cols(w_ukv, v_idx).astype(BF16)
    inv_freq = ROPE_BASE ** (-jnp.arange(0, MLA_ROPE, 2, dtype=F32) / MLA_ROPE)
    zpad = jnp.zeros((HALF - _R,), F32)
    freq = jnp.concatenate([inv_freq, zpad, inv_freq, zpad])[None]
    sgn = jnp.concatenate([-jnp.ones((_R,), F32), zpad, jnp.ones((_R,), F32), zpad])[None]
    nq = MLA_HEADS * LANES
    grid = (b, s // tm)
    out_shape = (jax.ShapeDtypeStruct((b, MLA_HEADS, s, LANES), BF16),
                 jax.ShapeDtypeStruct((b, MLA_HEADS, s, LANES), BF16),
                 jax.ShapeDtypeStruct((b, MLA_HEADS, s, LANES), BF16))
    return pl.pallas_call(
        _mla_proj_kernel, out_shape=out_shape, grid=grid,
        in_specs=[pl.BlockSpec((1, tm, D_MODEL), lambda bi, i: (bi, i, 0)),
                  pl.BlockSpec((1, tm, 1), lambda bi, i: (bi, i, 0)),
                  _const_spec((D_MODEL, _MLA_W + LANES)),
                  _const_spec((1, MLA_Q_RANK)), _const_spec((1, MLA_KV_RANK)),
                  _const_spec((MLA_Q_RANK, nq)), _const_spec((MLA_KV_RANK, nq)),
                  _const_spec((MLA_KV_RANK, nq)),
                  _const_spec((1, LANES)), _const_spec((1, LANES))],
        out_specs=(pl.BlockSpec((1, MLA_HEADS, tm, LANES), lambda bi, i: (bi, 0, i, 0)),
                   pl.BlockSpec((1, MLA_HEADS, tm, LANES), lambda bi, i: (bi, 0, i, 0)),
                   pl.BlockSpec((1, MLA_HEADS, tm, LANES), lambda bi, i: (bi, 0, i, 0))),
        compiler_params=_params("parallel", "parallel"), name="mla_proj",
    )(x, pos3, win, q_norm[None], kv_norm[None], wuq, wuk, wuv, freq, sgn)


def _flash_kernel(q_ref, k_ref, v_ref, o_ref, m_sc, acc_sc, s_sc, *, t, nh):
    i = pl.program_id(2)
    tk = t // 2
    row = lax.broadcasted_iota(I32, (t, tk), 0)
    col = lax.broadcasted_iota(I32, (t, tk), 1)
    m_sc[...] = jnp.full(m_sc.shape, NEG_BIG, F32)
    acc_sc[...] = jnp.zeros(acc_sc.shape, F32)

    def scores(j, hh):
        kc = k_ref[0, hh, pl.ds(pl.multiple_of(j * tk, tk), tk), :]
        return _dot_nt(q_ref[0, hh], kc)

    def step(j, slot, own=None, last=False):
        for hh in range(nh):
            sc = s_sc[slot, hh]
            if own is not None:
                sc = jnp.where(col + own * tk <= row, sc, NEG_BIG)
            if not last:
                s_sc[1 - slot, hh] = scores(j + 1, hh)
            m_prev = m_sc[hh]
            m_new = jnp.maximum(m_prev, jnp.max(sc, axis=-1, keepdims=True))
            a = jnp.exp2(m_prev - m_new)
            p = jnp.concatenate([jnp.exp2(sc[:, c * LANES:(c + 1) * LANES] - m_new)
                                 for c in range(tk // LANES)], axis=1).astype(BF16)
            vc = v_ref[0, hh, pl.ds(pl.multiple_of(j * tk, tk), tk), :]
            acc_sc[hh] = a * acc_sc[hh] + _dot(p, vc)
            m_sc[hh] = m_new

    for hh in range(nh):
        s_sc[0, hh] = scores(0, hh)

    def pair(u, c):
        step(2 * u, 0)
        step(2 * u + 1, 1)
        return c

    lax.fori_loop(0, i, pair, 0)
    step(2 * i, 0, own=0)
    step(2 * i + 1, 1, own=1, last=True)

    lane = lax.broadcasted_iota(I32, (t, LANES), 1)
    for pr in range(nh // 2):
        a0 = acc_sc[2 * pr]
        a1 = acc_sc[2 * pr + 1]
        o0 = a0 / a0[:, HALF:HALF + 1]
        o1 = a1 / a1[:, HALF:HALF + 1]
        o_ref[0, :, pr * LANES:(pr + 1) * LANES] = \
            jnp.where(lane < HALF, o0, pltpu.roll(o1, HALF, 1)).astype(BF16)


MLA_HEADS_PER_STEP = 4


def _flash_mla(q, k, v, t):
    b, h, s, _ = q.shape
    nh = MLA_HEADS_PER_STEP
    return pl.pallas_call(
        functools.partial(_flash_kernel, t=t, nh=nh),
        out_shape=jax.ShapeDtypeStruct((b, s, h * MLA_V), BF16),
        grid=(b, h // nh, s // t),
        in_specs=[pl.BlockSpec((1, nh, t, LANES), lambda bi, p, i: (bi, p, i, 0)),
                  pl.BlockSpec((1, nh, s, LANES), lambda bi, p, i: (bi, p, 0, 0)),
                  pl.BlockSpec((1, nh, s, LANES), lambda bi, p, i: (bi, p, 0, 0))],
        out_specs=pl.BlockSpec((1, t, nh * MLA_V), lambda bi, p, i: (bi, i, p)),
        scratch_shapes=[pltpu.VMEM((nh, t, LANES), F32), pltpu.VMEM((nh, t, LANES), F32),
                        pltpu.VMEM((2, nh, t, t // 2), F32)],
        compiler_params=_params("parallel", "parallel", "arbitrary"), name="mla_flash",
    )(q, k, v)


_DQ0 = 0
_DK0 = DSA_HEADS * LANES
_DV0 = _DK0 + DSA_KV_HEADS * LANES
_DI0 = _DV0 + DSA_KV_HEADS * LANES
_DW0 = _DI0 + IDX_HEADS * LANES
_DN = _DW0 + LANES


def _dsa_proj_kernel(x_ref, w_ref, qhot_ref, krow_ref, wsc_ref, qa_ref, ka_ref, v2_ref, qi_ref, kw_ref, wf_ref):
    xb = x_ref[0].astype(BF16)
    h = _dot(xb, w_ref[...])
    qscale = DSA_HEAD_DIM ** -0.5 * LOG2E
    for hd in range(DSA_HEADS):
        sl = slice(_DQ0 + hd * LANES, _DQ0 + (hd + 1) * LANES)
        qa_ref[0, hd] = (h[:, sl] * qscale + qhot_ref[hd:hd + 1, :]).astype(BF16)
    ones_hi = (lax.broadcasted_iota(I32, (1, LANES), 1) >= HALF).astype(F32)
    for g in range(DSA_KV_HEADS):
        ka_ref[0, g] = (h[:, _DK0 + g * LANES:_DK0 + (g + 1) * LANES] + krow_ref[g:g + 1, :]).astype(BF16)
        v2_ref[0, g] = (h[:, _DV0 + g * LANES:_DV0 + (g + 1) * LANES] + ones_hi).astype(BF16)
    for hd in range(IDX_HEADS):
        qi_ref[0, hd] = (h[:, _DI0 + hd * LANES:_DI0 + (hd + 1) * LANES] * (IDX_DIM ** -0.5)).astype(BF16)
    kw = h[:, _DW0:]
    kw_ref[0] = kw.astype(BF16)
    wf_ref[0] = kw * wsc_ref[...]


def _dsa_proj(x, w_in, rel_bias, tm):
    b, s, _ = x.shape
    o1 = DSA_HEADS * DSA_HEAD_DIM
    o2 = o1 + DSA_KV_HEADS * DSA_HEAD_DIM
    o3 = o2 + DSA_KV_HEADS * DSA_HEAD_DIM
    o4 = o3 + IDX_HEADS * IDX_DIM
    o5 = o4 + IDX_DIM
    idx = []
    for hd in range(DSA_HEADS):
        idx += [hd * DSA_HEAD_DIM + j for j in range(DSA_HEAD_DIM)] + [-1] * HALF
    for g in range(DSA_KV_HEADS):
        idx += [o1 + g * DSA_HEAD_DIM + j for j in range(DSA_HEAD_DIM)] + [-1] * HALF
    for g in range(DSA_KV_HEADS):
        idx += [o2 + g * DSA_HEAD_DIM + j for j in range(DSA_HEAD_DIM)] + [-1] * HALF
    for hd in range(IDX_HEADS):
        idx += [o3 + hd * IDX_DIM + j for j in range(IDX_DIM)] + [-1] * HALF
    idx += [o4 + j for j in range(IDX_DIM)] + [o5 + j for j in range(IDX_HEADS)] + [-1] * (HALF - IDX_HEADS)
    assert len(idx) == _DN
    w = _take_cols(w_in, idx).astype(BF16)
    cfar = rel_bias[REL_BUCKETS - 1] * LOG2E
    chi = cfar.astype(BF16).astype(F32)
    clo = (cfar - chi).astype(BF16).astype(F32)
    qhot = np.zeros((DSA_HEADS, LANES), np.float32)
    for hd in range(DSA_HEADS):
        qhot[hd, HALF + hd % DSA_GROUP] = 1.0
        qhot[hd, HALF + DSA_GROUP + hd % DSA_GROUP] = 1.0
    zpad = jnp.zeros((DSA_KV_HEADS, LANES - HALF - 2 * DSA_GROUP), F32)
    krow = jnp.concatenate([jnp.zeros((DSA_KV_HEADS, HALF), F32), chi.reshape(DSA_KV_HEADS, DSA_GROUP),
                            clo.reshape(DSA_KV_HEADS, DSA_GROUP), zpad], axis=1)
    wsc = np.zeros((1, LANES), np.float32)
    wsc[0, HALF:HALF + IDX_HEADS] = IDX_HEADS ** -0.5
    hspec = lambda n: pl.BlockSpec((1, n, tm, LANES), lambda bi, i: (bi, 0, i, 0))
    tspec = pl.BlockSpec((1, tm, LANES), lambda bi, i: (bi, i, 0))
    out_shape = (jax.ShapeDtypeStruct((b, DSA_HEADS, s, LANES), BF16),
                 jax.ShapeDtypeStruct((b, DSA_KV_HEADS, s, LANES), BF16),
                 jax.ShapeDtypeStruct((b, DSA_KV_HEADS, s, LANES), BF16),
                 jax.ShapeDtypeStruct((b, IDX_HEADS, s, LANES), BF16),
                 jax.ShapeDtypeStruct((b, s, LANES), BF16),
                 jax.ShapeDtypeStruct((b, s, LANES), F32))
    return pl.pallas_call(
        _dsa_proj_kernel, out_shape=out_shape, grid=(b, s // tm),
        in_specs=[pl.BlockSpec((1, tm, D_MODEL), lambda bi, i: (bi, i, 0)),
                  _const_spec((D_MODEL, _DN)), _const_spec((DSA_HEADS, LANES)),
                  _const_spec((DSA_KV_HEADS, LANES)), _const_spec((1, LANES))],
        out_specs=(hspec(DSA_HEADS), hspec(DSA_KV_HEADS), hspec(DSA_KV_HEADS), hspec(IDX_HEADS), tspec, tspec),
        compiler_params=_params("parallel", "parallel"), name="dsa_proj",
    )(x, w, jnp.asarray(qhot), krow, jnp.asarray(wsc))


def _t5_tiles_kernel(rb_ref, o_ref):
    d = pl.program_id(0)
    hd = pl.program_id(1)
    row = lax.broadcasted_iota(I32, (LANES, LANES), 0)
    col = lax.broadcasted_iota(I32, (LANES, LANES), 1)
    n = jnp.maximum(d * LANES + row - col, 0)
    max_exact = REL_BUCKETS // 2
    nf = jnp.maximum(n, 1).astype(F32)
    large = max_exact + (jnp.log(nf / max_exact) / math.log(REL_MAX_DIST / max_exact)
                         * (REL_BUCKETS - max_exact)).astype(I32)
    large = jnp.minimum(large, REL_BUCKETS - 1)
    bucket = jnp.where(n < max_exact, n, large)
    far = rb_ref[REL_BUCKETS - 1, hd]
    acc = jnp.zeros((LANES, LANES), F32)
    for bk in range(REL_BUCKETS - 1):
        acc = jnp.where(bucket == bk, rb_ref[bk, hd] - far, acc)
    o_ref[0, 0] = acc * LOG2E


def _t5_tiles(rel_bias):
    return pl.pallas_call(
        _t5_tiles_kernel,
        out_shape=jax.ShapeDtypeStruct((2, DSA_HEADS, LANES, LANES), F32),
        grid=(2, DSA_HEADS),
        in_specs=[pl.BlockSpec(memory_space=pltpu.SMEM)],
        out_specs=pl.BlockSpec((1, 1, LANES, LANES), lambda d, h: (d, h, 0, 0)),
        compiler_params=_params("parallel", "parallel"), name="t5_tiles",
    )(rel_bias)


_SENT_KEY = int(np.int32(np.uint32(0xFF800000) ^ np.uint32(0x7FFFFFFF)))
_HI16 = -65536
_MIN_NORMAL_BITS = 0x00800000
_BITS_ALWAYS = 23
_INT_MIN = -(2 ** 31)


def _dsa_attn_kernel(qa_ref, qi_ref, wf_ref, ka_ref, v_ref, kw_ref, tb_ref, o_ref,
                     keyt_sc, kb_sc, madd_sc, s_sc, m_sc, acc_sc, cut_sc, thr_sc, cge_sc, tcnt_sc,
                     *, t, topk, s_len):
    i = pl.program_id(1)
    nsub = t // LANES
    kidx = lax.broadcasted_iota(I32, (t, t), 0)
    qidx = lax.broadcasted_iota(I32, (t, t), 1)

    qi = qi_ref[0].reshape(IDX_HEADS * t, LANES)
    wt = wf_ref[0].T
    wrows = [wt[HALF + hd:HALF + hd + 1, :] for hd in range(IDX_HEADS)]

    def scores_t(j0, w):
        kc = kw_ref[0, pl.ds(pl.multiple_of(j0 * t, t), w * t), :]
        d = _dot_nt(kc, qi)
        sc = wrows[0] * jnp.maximum(d[:, 0:t], 0.0)
        for hd in range(1, IDX_HEADS):
            sc = sc + wrows[hd] * jnp.maximum(d[:, hd * t:(hd + 1) * t], 0.0)
        return sc

    def store_keys(j, sc, causal):
        if causal:
            sc = jnp.where(kidx <= qidx, sc, -jnp.inf)
        bits = pltpu.bitcast(sc + 0.0, I32)
        keyt_sc[j] = bits ^ ((bits >> 31) & 0x7FFFFFFF)
        kb_sc[j] = pltpu.bitcast(bits & _HI16, F32).astype(BF16)

    def p1(u, c):
        sc = scores_t(2 * u, 2)
        store_keys(2 * u, sc[0:t], False)
        store_keys(2 * u + 1, sc[t:2 * t], False)
        return c

    lax.fori_loop(0, i // 2, p1, 0)

    @pl.when(i % 2 == 1)
    def _():
        sc = scores_t(i - 1, 2)
        store_keys(i - 1, sc[0:t], False)
        store_keys(i, sc[t:2 * t], True)

    @pl.when(i % 2 == 0)
    def _():
        store_keys(i, scores_t(i, 1), True)

    cacc = 32

    kf = float(topk)

    def bisect(nchunks):
        if nchunks * t <= topk:
            thr_sc[...] = jnp.full(thr_sc.shape, _INT_MIN, I32)
            cge_sc[...] = jnp.full(cge_sc.shape, kf, F32)
            return

        def count_ge(cand):
            acc = jnp.zeros((cacc, t), F32)
            for j in range(nchunks):
                c = jnp.where(keyt_sc[j] >= cand, 1.0, 0.0)
                acc = acc + jnp.sum(c.reshape(t // cacc, cacc, t), axis=0)
            return jnp.sum(acc, axis=0, keepdims=True)

        one_b = jnp.ones((), BF16)
        zero_b = jnp.zeros((), BF16)

        def count_ge16(cand):
            cbits = (cand ^ ((cand >> 31) & 0x7FFFFFFF)) & _HI16
            cbits = jnp.where((cand > 0) & (cand < _MIN_NORMAL_BITS), _MIN_NORMAL_BITS, cbits)
            cf = lax.bitcast_convert_type(cbits, F32)
            cb = jnp.broadcast_to(cf, (16, t)).astype(BF16)
            cb = jnp.broadcast_to(cb[None], (t // 16, 16, t)).reshape(t, t)
            acc = jnp.zeros((cacc, t), BF16)
            for j in range(nchunks):
                c = jnp.where(kb_sc[j] >= cb, one_b, zero_b)
                for k in range(t // cacc):
                    acc = acc + c[k * cacc:(k + 1) * cacc]
            return jnp.sum(acc.astype(F32), axis=0, keepdims=True)

        cnt0 = count_ge16(jnp.zeros((1, t), I32))
        nonneg = cnt0 >= kf
        prefix0 = jnp.where(nonneg, 0, _INT_MIN).astype(I32)
        cge0 = jnp.where(nonneg, cnt0, float(nchunks * t))

        def bit_body(count_fn, bi, carry):
            prefix, cge = carry
            cand = prefix + lax.shift_left(jnp.int32(1), 30 - bi)
            cnt = count_fn(cand)
            take = cnt >= kf
            return jnp.where(take, cand, prefix), jnp.where(take, cnt, cge)

        carry = lax.fori_loop(0, 15, functools.partial(bit_body, count_ge16), (prefix0, cge0))
        carry = lax.fori_loop(15, _BITS_ALWAYS, functools.partial(bit_body, count_ge), carry)

        def more(state):
            bi, prefix, cge = state
            open_ = (cge != kf) & jnp.logical_not(nonneg & (prefix == 0))
            return (bi < 31) & (jnp.max(jnp.where(open_, 1.0, 0.0)) > 0.0)

        def one_bit(state):
            bi, prefix, cge = state
            return (bi + 1,) + bit_body(count_ge, bi, (prefix, cge))

        _, thr, cge = lax.while_loop(more, one_bit, (jnp.int32(_BITS_ALWAYS),) + carry)
        thr_sc[...] = jnp.broadcast_to(thr, thr_sc.shape)
        cge_sc[...] = jnp.broadcast_to(cge, cge_sc.shape)

    for ii in range(s_len // t):
        pl.when(i == ii)(functools.partial(bisect, ii + 1))
    thr = thr_sc[0:1, :]
    cnt_ge = cge_sc[0:1, :]
    excess = jnp.where((cnt_ge > kf) & (thr > _SENT_KEY), 1.0, 0.0)
    cut_sc[...] = jnp.full(cut_sc.shape, s_len, I32)

    @pl.when(jnp.max(excess) > 0.0)
    def _():
        def sweep(j, gt):
            kc = keyt_sc[j]
            tie = jnp.where(kc == thr, 1.0, 0.0)
            tcnt_sc[j] = jnp.broadcast_to(jnp.sum(tie, axis=0, keepdims=True), (8, t))
            g = jnp.where(kc > thr, 1.0, 0.0)
            return gt + jnp.sum(g.reshape(t // cacc, cacc, t), axis=0)

        gt = lax.fori_loop(0, i + 1, sweep, jnp.zeros((cacc, t), F32))
        need = kf - jnp.sum(gt, axis=0, keepdims=True)

        def scan(j, carry):
            run, cstar, base = carry
            new = run + tcnt_sc[j, 0:1, :]
            hit = (run < need) & (new >= need)
            return new, jnp.where(hit, j, cstar), jnp.where(hit, run, base)

        zrow = jnp.zeros((1, t), F32)
        _, cstar, base = lax.fori_loop(0, i + 1, scan, (zrow, jnp.zeros((1, t), I32), zrow))

        def gather(j, tile):
            return jnp.where(cstar == j, keyt_sc[j], tile)

        tile = lax.fori_loop(0, i + 1, gather, jnp.full((t, t), _SENT_KEY, I32))
        tie = jnp.where(tile == thr, 1.0, 0.0)
        need_in = need - base
        rbits = max(1, (t - 1).bit_length())

        def row_body(bi, r):
            cand = r + lax.shift_left(jnp.int32(1), rbits - 1 - bi)
            cnt = jnp.sum(jnp.where(kidx < cand, tie, 0.0), axis=0, keepdims=True)
            return jnp.where(cnt < need_in, cand, r)

        r = lax.fori_loop(0, rbits, row_body, jnp.zeros((1, t), I32))
        cut_sc[...] = jnp.broadcast_to(cstar * t + r, cut_sc.shape)

    cut = cut_sc[0:1, :]

    def to_mask(j, causal):
        kc = keyt_sc[j]
        sel = (kc > thr) | ((kc == thr) & (kidx + j * t <= cut))
        if causal:
            sel = sel & (kidx <= qidx)
        return jnp.where(sel, 0.0, NEG_BIG).T

    m_sc[...] = jnp.full(m_sc.shape, NEG_BIG, F32)
    acc_sc[...] = jnp.zeros(acc_sc.shape, F32)
    gt = DSA_GROUP * t
    near_delta = {"prev": nsub, "diag": 0}

    def attend(j0, parts):
        w = len(parts)
        start = pl.multiple_of(j0 * t, t)

        def qk(g):
            q = qa_ref[0, g * DSA_GROUP:(g + 1) * DSA_GROUP].reshape(gt, LANES)
            return _dot_nt(q, ka_ref[0, g, pl.ds(start, w * t), :])

        s_sc[0, :, 0:w * t] = qk(0)
        for pi, part in enumerate(parts):
            madd_sc[pi] = to_mask(j0 + pi, part == "diag")
        for g in range(DSA_KV_HEADS):
            if g + 1 < DSA_KV_HEADS:
                s_sc[(g + 1) % 2, :, 0:w * t] = qk(g + 1)
            slot = g % 2
            p_rows, a_rows = [], []
            for r in range(DSA_GROUP):
                hd = g * DSA_GROUP + r
                for rs in range(nsub):
                    r0 = r * t + rs * LANES
                    blocks = []
                    for cb in range(w * nsub):
                        part, cs = parts[cb // nsub], cb % nsub
                        blk = s_sc[slot, r0:r0 + LANES, cb * LANES:(cb + 1) * LANES] \
                            + madd_sc[cb // nsub, rs * LANES:(rs + 1) * LANES, cs * LANES:(cs + 1) * LANES]
                        if part in near_delta and near_delta[part] + rs - cs in (0, 1):
                            blk = blk + tb_ref[near_delta[part] + rs - cs, hd]
                        blocks.append(blk)
                    m_prev = m_sc[g, r0:r0 + LANES]
                    m_blk = blocks[0]
                    for blk in blocks[1:]:
                        m_blk = jnp.maximum(m_blk, blk)
                    m_new = jnp.maximum(m_prev, jnp.max(m_blk, axis=-1, keepdims=True))
                    a_rows.append(jnp.exp2(m_prev - m_new))
                    p_rows.append(jnp.concatenate([jnp.exp2(blk - m_new).astype(BF16) for blk in blocks], axis=1))
                    m_sc[g, r0:r0 + LANES] = m_new
            p = jnp.concatenate(p_rows, axis=0)
            a = jnp.concatenate(a_rows, axis=0)
            vc = v_ref[0, g, pl.ds(start, w * t), :]
            acc_sc[g] = a * acc_sc[g] + _dot(p, vc)

    @pl.when(i == 0)
    def _():
        attend(0, ("diag",))

    @pl.when(i >= 1)
    def _():
        nfar = i - 1

        def far_pair(u, c):
            attend(2 * u, ("far", "far"))
            return c

        lax.fori_loop(0, nfar // 2, far_pair, 0)

        @pl.when(nfar % 2 == 1)
        def _():
            attend(i - 2, ("far",))

        attend(i - 1, ("prev", "diag"))

    lane = lax.broadcasted_iota(I32, (t, LANES), 1)
    for g in range(DSA_KV_HEADS):
        acc = acc_sc[g]
        o = acc / acc[:, HALF:HALF + 1]
        for pr in range(DSA_GROUP // 2):
            lo = o[(2 * pr) * t:(2 * pr + 1) * t]
            hi = o[(2 * pr + 1) * t:(2 * pr + 2) * t]
            cblk = g * (DSA_GROUP // 2) + pr
            o_ref[0, :, cblk * LANES:(cblk + 1) * LANES] = \
                jnp.where(lane < HALF, lo, pltpu.roll(hi, HALF, 1)).astype(BF16)


def _dsa_attn(qa, ka, v, qi, kw, wf, tb, t, topk):
    b, _, s, _ = qa.shape
    nch = s // t
    gt = DSA_GROUP * t
    kern = functools.partial(_dsa_attn_kernel, t=t, topk=topk, s_len=s)
    return pl.pallas_call(
        kern, out_shape=jax.ShapeDtypeStruct((b, s, DSA_HEADS * DSA_HEAD_DIM), BF16),
        grid=(b, nch),
        in_specs=[pl.BlockSpec((1, DSA_HEADS, t, LANES), lambda bi, i: (bi, 0, i, 0)),
                  pl.BlockSpec((1, IDX_HEADS, t, LANES), lambda bi, i: (bi, 0, i, 0)),
                  pl.BlockSpec((1, t, LANES), lambda bi, i: (bi, i, 0)),
                  pl.BlockSpec((1, DSA_KV_HEADS, s, LANES), lambda bi, i: (bi, 0, 0, 0)),
                  pl.BlockSpec((1, DSA_KV_HEADS, s, LANES), lambda bi, i: (bi, 0, 0, 0)),
                  pl.BlockSpec((1, s, LANES), lambda bi, i: (bi, 0, 0)),
                  _const_spec((2, DSA_HEADS, LANES, LANES))],
        out_specs=pl.BlockSpec((1, t, DSA_HEADS * DSA_HEAD_DIM), lambda bi, i: (bi, i, 0)),
        scratch_shapes=[pltpu.VMEM((nch, t, t), I32),
                        pltpu.VMEM((nch, t, t), BF16),
                        pltpu.VMEM((2, t, t), F32),
                        pltpu.VMEM((2, gt, 2 * t), F32),
                        pltpu.VMEM((DSA_KV_HEADS, gt, LANES), F32),
                        pltpu.VMEM((DSA_KV_HEADS, gt, LANES), F32),
                        pltpu.VMEM((8, t), I32),
                        pltpu.VMEM((8, t), I32),
                        pltpu.VMEM((8, t), F32),
                        pltpu.VMEM((nch, 8, t), F32)],
        compiler_params=_params("parallel", "arbitrary"), name="dsa_attn",
    )(qa, qi, wf, ka, v, kw, tb)


def _mm_kernel(a_ref, w_ref, o_ref):
    o_ref[...] = _dot(a_ref[...].astype(BF16), w_ref[...]).astype(BF16)


def _mm(a, w, tm):
    m, k = a.shape
    n = w.shape[1]
    return pl.pallas_call(
        _mm_kernel, out_shape=jax.ShapeDtypeStruct((m, n), BF16), grid=(m // tm,),
        in_specs=[pl.BlockSpec((tm, k), lambda i: (i, 0)), _const_spec((k, n))],
        out_specs=pl.BlockSpec((tm, n), lambda i: (i, 0)),
        compiler_params=_params("parallel"), name="mem_kv",
    )(a, w)


def _post_kernel(a_ref, x_ref, wo_ref, wq_ref, kv_ref, wxo_ref, g_ref, b_ref, o_ref, oc_sc):
    y = _dot(a_ref[0], wo_ref[...])
    x1 = _layer_norm(ALPHA * x_ref[0] + y, g_ref[0:1, :], b_ref[0:1, :])
    q = (_dot(x1.astype(BF16), wq_ref[...]) * (XA_HEAD_DIM ** -0.5 * LOG2E)).astype(BF16)
    for hd in range(XA_HEADS):
        sl = slice(hd * XA_HEAD_DIM, (hd + 1) * XA_HEAD_DIM)
        sc = _dot_nt(q[:, sl], kv_ref[0, :, sl])
        m = jnp.max(sc, axis=-1, keepdims=True)
        p = jnp.exp2(sc - m)
        l = jnp.sum(p, axis=-1, keepdims=True)
        vsl = slice(D_MODEL + hd * XA_HEAD_DIM, D_MODEL + (hd + 1) * XA_HEAD_DIM)
        oc_sc[:, sl] = (_dot(p.astype(BF16), kv_ref[0, :, vsl]) / l).astype(BF16)
    y2 = _dot(oc_sc[...], wxo_ref[...])
    o_ref[0] = _layer_norm(ALPHA * x1 + y2, g_ref[1:2, :], b_ref[1:2, :])


def _post(a, x, w_o, w_q, kv, w_xo, g, bta, tm):
    b, s, _ = x.shape
    mlen = kv.shape[1]
    tok = lambda bi, i: (bi, i, 0)
    return pl.pallas_call(
        _post_kernel, out_shape=jax.ShapeDtypeStruct((b, s, D_MODEL), F32), grid=(b, s // tm),
        in_specs=[pl.BlockSpec((1, tm, D_MODEL), tok), pl.BlockSpec((1, tm, D_MODEL), tok),
                  _const_spec((D_MODEL, D_MODEL)), _const_spec((D_MODEL, D_MODEL)),
                  pl.BlockSpec((1, mlen, 2 * D_MODEL), lambda bi, i: (bi, 0, 0)),
                  _const_spec((D_MODEL, D_MODEL)), _const_spec((3, D_MODEL)), _const_spec((3, D_MODEL))],
        out_specs=pl.BlockSpec((1, tm, D_MODEL), tok),
        scratch_shapes=[pltpu.VMEM((tm, D_MODEL), BF16)],
        compiler_params=_params("parallel", "parallel"), name="post_attn",
    )(a, x, w_o, w_q, kv, w_xo, g, bta)


FFN_CHUNK = 256


def _ffn_kernel(x_ref, wi_ref, wd_ref, g_ref, b_ref, o_ref, act_sc):
    x = x_ref[...]
    xb = x.astype(BF16)
    for c in range(FFN_HIDDEN // FFN_CHUNK):
        sl = slice(c * FFN_CHUNK, (c + 1) * FFN_CHUNK)
        gte = _dot(xb, wi_ref[:, sl])
        up = _dot(xb, wi_ref[:, FFN_HIDDEN + c * FFN_CHUNK:FFN_HIDDEN + (c + 1) * FFN_CHUNK])
        act_sc[:, sl] = (gte * jax.nn.sigmoid(gte) * up).astype(BF16)
    y = _dot(act_sc[...], wd_ref[...])
    o_ref[...] = _layer_norm(ALPHA * x + y, g_ref[2:3, :], b_ref[2:3, :])


def _ffn(x2d, w_in, w_down, g, bta, tm):
    n = x2d.shape[0]
    return pl.pallas_call(
        _ffn_kernel, out_shape=jax.ShapeDtypeStruct((n, D_MODEL), F32), grid=(n // tm,),
        in_specs=[pl.BlockSpec((tm, D_MODEL), lambda i: (i, 0)),
                  _const_spec((D_MODEL, 2 * FFN_HIDDEN)), _const_spec((FFN_HIDDEN, D_MODEL)),
                  _const_spec((3, D_MODEL)), _const_spec((3, D_MODEL))],
        out_specs=pl.BlockSpec((tm, D_MODEL), lambda i: (i, 0)),
        scratch_shapes=[pltpu.VMEM((tm, FFN_HIDDEN), BF16)],
        compiler_params=_params("parallel"), name="ffn",
    )(x2d, w_in.astype(BF16), w_down.astype(BF16), g, bta)


def _tile(n, pref):
    t = min(n, pref)
    assert n % t == 0, (n, pref)
    return t


def kernel(x, mem, positions, rel_bias, mla_w_in, mla_q_norm, mla_w_uq, mla_kv_norm, mla_w_ukv, mla_w_o,
           dsa_w_in, dsa_w_o, xa_w_q, xa_w_kv, xa_w_o, ffn_w_in, ffn_w_down, ln_g, ln_b):
    b, s, d = x.shape
    assert d == D_MODEL and s % LANES == 0
    mlen = mem.shape[1]
    topk = min(TOPK_MAX, s // 4)
    t_attn = _tile(s, 256)
    tm_proj = _tile(s, 512)
    tm_post = _tile(s, 1024)
    tm_ffn = _tile(b * s, 1024)
    pos3 = positions.reshape(b, s, 1)
    tb = _t5_tiles(rel_bias)
    for i in range(DEPTH):
        j = i // 2
        if i % 2 == 0:
            q, k, v = _mla_proj(x, pos3, mla_w_in[j], mla_q_norm[j], mla_w_uq[j], mla_kv_norm[j],
                                mla_w_ukv[j], tm_proj)
            a = _flash_mla(q, k, v, _tile(s, 1024))
            w_o = mla_w_o[j]
        else:
            qa, ka, v2, qi, kw, wf = _dsa_proj(x, dsa_w_in[j], rel_bias, tm_proj)
            a = _dsa_attn(qa, ka, v2, qi, kw, wf, tb, t_attn, topk)
            w_o = dsa_w_o[j]
        kv = _mm(mem.reshape(b * mlen, d), xa_w_kv[i].astype(BF16), _tile(b * mlen, 512)).reshape(b, mlen, 2 * d)
        x = _post(a, x, w_o.astype(BF16), xa_w_q[i].astype(BF16), kv, xa_w_o[i].astype(BF16),
                  ln_g[i], ln_b[i], tm_post)
        x = _ffn(x.reshape(b * s, d), ffn_w_in[i], ffn_w_down[i], ln_g[i], ln_b[i], tm_ffn).reshape(b, s, d)
    return x
```

```python
import functools
import math

import numpy as np
import jax
import jax.numpy as jnp
from jax import lax
from jax.experimental import pallas as pl
from jax.experimental.pallas import tpu as pltpu

D_MODEL = 1024
DEPTH = 2
MLA_HEADS = 16
MLA_Q_RANK = 512
MLA_KV_RANK = 256
MLA_NOPE = 64
MLA_ROPE = 32
MLA_V = 64
DSA_HEADS = 16
DSA_KV_HEADS = 4
DSA_GROUP = DSA_HEADS // DSA_KV_HEADS
DSA_HEAD_DIM = 64
IDX_HEADS = 8
IDX_DIM = 64
TOPK_MAX = 256
XA_HEADS = 4
XA_HEAD_DIM = D_MODEL // XA_HEADS
FFN_HIDDEN = -(-8 * D_MODEL // (3 * 256)) * 256
REL_BUCKETS = 32
REL_MAX_DIST = 128
ROPE_BASE = 10000.0
LN_EPS = 1e-5
RMS_EPS = 1e-6
ALPHA = (2 * DEPTH) ** 0.25

LANES = 128
HALF = LANES // 2
VMEM_LIMIT = 56 * 1024 * 1024

LOG2E = math.log2(math.e)
NEG_BIG = -1e30
F32 = jnp.float32
BF16 = jnp.bfloat16
I32 = jnp.int32
_NT = (((1,), (1,)), ((), ()))


def _dot(a, b):
    return jnp.dot(a, b, preferred_element_type=F32)


def _dot_nt(a, b):
    return lax.dot_general(a, b, _NT, preferred_element_type=F32)


def _layer_norm(y, g, b):
    mu = jnp.mean(y, axis=-1, keepdims=True)
    yc = y - mu
    var = jnp.mean(yc * yc, axis=-1, keepdims=True)
    return yc * lax.rsqrt(var + LN_EPS) * g + b


def _rms_norm(y, g):
    return y * lax.rsqrt(jnp.mean(y * y, axis=-1, keepdims=True) + RMS_EPS) * g


def _params(*sem):
    return pltpu.CompilerParams(dimension_semantics=sem, vmem_limit_bytes=VMEM_LIMIT)


def _const_spec(shape):
    nd = len(shape)
    return pl.BlockSpec(shape, lambda *_: (0,) * nd)


def _take_cols(w, idx):
    idx = np.asarray(idx)
    wp = jnp.concatenate([w, jnp.zeros((w.shape[0], 1), w.dtype)], axis=1)
    return wp[:, np.where(idx < 0, w.shape[1], idx)]


_R = MLA_ROPE // 2
_MLA_W = MLA_Q_RANK + MLA_KV_RANK


def _mla_proj_kernel(x_ref, pos_ref, win_ref, qg_ref, kvg_ref, wuq_ref, wuk_ref, wuv_ref,
                     freq_ref, sgn_ref, q_ref, k_ref, v_ref):
    xb = x_ref[0].astype(BF16)
    h = _dot(xb, win_ref[...])
    cq = _rms_norm(h[:, :MLA_Q_RANK], qg_ref[...]).astype(BF16)
    ckv = _rms_norm(h[:, MLA_Q_RANK:_MLA_W], kvg_ref[...]).astype(BF16)
    kr = h[:, _MLA_W:]
    ang = pos_ref[0].astype(F32) * freq_ref[...]
    cos = jnp.cos(ang)
    sin = jnp.sin(ang) * sgn_ref[...]

    def rope(t):
        return t * cos + pltpu.roll(t, HALF, 1) * sin

    kr = rope(kr)
    q_all = _dot(cq, wuq_ref[...])
    k_all = _dot(ckv, wuk_ref[...])
    v_all = _dot(ckv, wuv_ref[...])
    qscale = (MLA_NOPE + MLA_ROPE) ** -0.5 * LOG2E
    ones_hi = (lax.broadcasted_iota(I32, (1, LANES), 1) >= HALF).astype(F32)
    for hd in range(MLA_HEADS):
        sl = slice(hd * LANES, (hd + 1) * LANES)
        q_ref[0, hd] = (rope(q_all[:, sl]) * qscale).astype(BF16)
        k_ref[0, hd] = (k_all[:, sl] + kr).astype(BF16)
        v_ref[0, hd] = (v_all[:, sl] + ones_hi).astype(BF16)


def _mla_proj(x, pos3, w_in, q_norm, w_uq, kv_norm, w_ukv, tm):
    b, s, _ = x.shape
    hq = MLA_NOPE + MLA_ROPE
    in_idx = list(range(_MLA_W)) + [_MLA_W + j for j in range(_R)] + [-1] * (HALF - _R) \
        + [_MLA_W + _R + j for j in range(_R)] + [-1] * (HALF - _R)
    q_idx, k_idx, v_idx = [], [], []
    n_lo = HALF - _R
    n_hi = MLA_NOPE - n_lo
    tail = [-1] * (HALF - _R - n_hi)
    for hd in range(MLA_HEADS):
        qb = hd * hq
        q_idx += [qb + MLA_NOPE + j for j in range(_R)] + [qb + j for j in range(n_lo)] \
            + [qb + MLA_NOPE + _R + j for j in range(_R)] + [qb + n_lo + j for j in range(n_hi)] + tail
        kb = hd * (MLA_NOPE + MLA_V)
        k_idx += [-1] * _R + [kb + j for j in range(n_lo)] + [-1] * _R + [kb + n_lo + j for j in range(n_hi)] + tail
        v_idx += [kb + MLA_NOPE + j for j in range(MLA_V)] + [-1] * (LANES - MLA_V)
    win = _take_cols(w_in, in_idx).astype(BF16)
    wuq = _take_cols(w_uq, q_idx).astype(BF16)
    wuk = _take_cols(w_ukv, k_idx).astype(BF16)
    wuv = _take_cols(w_ukv, v_idx).astype(BF16)
    inv_freq = ROPE_BASE ** (-jnp.arange(0, MLA_ROPE, 2, dtype=F32) / MLA_ROPE)
    zpad = jnp.zeros((HALF - _R,), F32)
    freq = jnp.concatenate([inv_freq, zpad, inv_freq, zpad])[None]
    sgn = jnp.concatenate([-jnp.ones((_R,), F32), zpad, jnp.ones((_R,), F32), zpad])[None]
    nq = MLA_HEADS * LANES
    grid = (b, s // tm)
    out_shape = (jax.ShapeDtypeStruct((b, MLA_HEADS, s, LANES), BF16),
                 jax.ShapeDtypeStruct((b, MLA_HEADS, s, LANES), BF16),
                 jax.ShapeDtypeStruct((b, MLA_HEADS, s, LANES), BF16))
    return pl.pallas_call(
        _mla_proj_kernel, out_shape=out_shape, grid=grid,
        in_specs=[pl.BlockSpec((1, tm, D_MODEL), lambda bi, i: (bi, i, 0)),
                  pl.BlockSpec((1, tm, 1), lambda bi, i: (bi, i, 0)),
                  _const_spec((D_MODEL, _MLA_W + LANES)),
                  _const_spec((1, MLA_Q_RANK)), _const_spec((1, MLA_KV_RANK)),
                  _const_spec((MLA_Q_RANK, nq)), _const_spec((MLA_KV_RANK, nq)),
                  _const_spec((MLA_KV_RANK, nq)),
                  _const_spec((1, LANES)), _const_spec((1, LANES))],
        out_specs=(pl.BlockSpec((1, MLA_HEADS, tm, LANES), lambda bi, i: (bi, 0, i, 0)),
                   pl.BlockSpec((1, MLA_HEADS, tm, LANES), lambda bi, i: (bi, 0, i, 0)),
                   pl.BlockSpec((1, MLA_HEADS, tm, LANES), lambda bi, i: (bi, 0, i, 0))),
        compiler_params=_params("parallel", "parallel"), name="mla_proj",
    )(x, pos3, win, q_norm[None], kv_norm[None], wuq, wuk, wuv, freq, sgn)


def _flash_kernel(q_ref, k_ref, v_ref, o_ref, m_sc, acc_sc, s_sc, *, t, nh):
    i = pl.program_id(2)
    tk = t // 2
    row = lax.broadcasted_iota(I32, (t, tk), 0)
    col = lax.broadcasted_iota(I32, (t, tk), 1)
    m_sc[...] = jnp.full(m_sc.shape, NEG_BIG, F32)
    acc_sc[...] = jnp.zeros(acc_sc.shape, F32)

    def keys(j, hh):
        return k_ref[0, hh, pl.ds(pl.multiple_of(j * tk, tk), tk), :]

    def scores(j, hh):
        return _dot_nt(q_ref[0, hh].reshape(t, LANES), keys(j, hh)).reshape(2, tk, tk)

    def softmax_pv(sc, m_prev, acc_prev, j, hh):
        m_new = jnp.maximum(m_prev, jnp.max(sc, axis=-1, keepdims=True))
        a = jnp.exp2(m_prev - m_new)
        p = jnp.concatenate([jnp.exp2(sc[:, c * LANES:(c + 1) * LANES] - m_new)
                             for c in range(tk // LANES)], axis=1).astype(BF16)
        vc = v_ref[0, hh, pl.ds(pl.multiple_of(j * tk, tk), tk), :]
        return m_new, a * acc_prev + _dot(p, vc)

    def step(j, slot, own=False):
        for hh in range(nh):
            sc = s_sc[slot, hh].reshape(t, tk)
            if own:
                sc = jnp.where((row >= tk) | (col <= row), sc, NEG_BIG)
                s_sc[1 - slot, hh, 1] = _dot_nt(q_ref[0, hh, 1], keys(j + 1, hh))
            else:
                s_sc[1 - slot, hh] = scores(j + 1, hh)
            m_new, acc = softmax_pv(sc, m_sc[hh].reshape(t, LANES), acc_sc[hh].reshape(t, LANES), j, hh)
            m_sc[hh] = m_new.reshape(2, tk, LANES)
            acc_sc[hh] = acc.reshape(2, tk, LANES)

    def last_step(j, slot):
        for hh in range(nh):
            tri = lax.broadcasted_iota(I32, (tk, tk), 1) <= lax.broadcasted_iota(I32, (tk, tk), 0)
            sc = jnp.where(tri, s_sc[slot, hh, 1], NEG_BIG)
            m_sc[hh, 1], acc_sc[hh, 1] = softmax_pv(sc, m_sc[hh, 1], acc_sc[hh, 1], j, hh)

    for hh in range(nh):
        s_sc[0, hh] = scores(0, hh)

    def pair(u, c):
        step(2 * u, 0)
        step(2 * u + 1, 1)
        return c

    lax.fori_loop(0, i, pair, 0)
    step(2 * i, 0, own=True)
    last_step(2 * i + 1, 1)

    lane = lax.broadcasted_iota(I32, (t, LANES), 1)
    for pr in range(nh // 2):
        a0 = acc_sc[2 * pr].reshape(t, LANES)
        a1 = acc_sc[2 * pr + 1].reshape(t, LANES)
        o0 = a0 / a0[:, HALF:HALF + 1]
        o1 = a1 / a1[:, HALF:HALF + 1]
        o_ref[0, :, pr * LANES:(pr + 1) * LANES] = \
            jnp.where(lane < HALF, o0, pltpu.roll(o1, HALF, 1)).astype(BF16)


MLA_HEADS_PER_STEP = 4


def _flash_mla(q, k, v, t):
    b, h, s, _ = q.shape
    nh = MLA_HEADS_PER_STEP
    tk = t // 2
    return pl.pallas_call(
        functools.partial(_flash_kernel, t=t, nh=nh),
        out_shape=jax.ShapeDtypeStruct((b, s, h * MLA_V), BF16),
        grid=(b, h // nh, s // t),
        in_specs=[pl.BlockSpec((1, nh, 2, tk, LANES), lambda bi, p, i: (bi, p, i, 0, 0)),
                  pl.BlockSpec((1, nh, s, LANES), lambda bi, p, i: (bi, p, 0, 0)),
                  pl.BlockSpec((1, nh, s, LANES), lambda bi, p, i: (bi, p, 0, 0))],
        out_specs=pl.BlockSpec((1, t, nh * MLA_V), lambda bi, p, i: (bi, i, p)),
        scratch_shapes=[pltpu.VMEM((nh, 2, tk, LANES), F32), pltpu.VMEM((nh, 2, tk, LANES), F32),
                        pltpu.VMEM((2, nh, 2, tk, tk), F32)],
        compiler_params=_params("parallel", "parallel", "arbitrary"), name="mla_flash",
    )(q.reshape(b, h, s // tk, tk, LANES), k, v)


_DQ0 = 0
_DK0 = DSA_HEADS * LANES
_DV0 = _DK0 + DSA_KV_HEADS * LANES
_DI0 = _DV0 + DSA_KV_HEADS * LANES
_DW0 = _DI0 + IDX_HEADS * LANES
_DN = _DW0 + LANES


def _dsa_proj_kernel(x_ref, w_ref, qhot_ref, krow_ref, wsc_ref, qa_ref, ka_ref, v2_ref, qi_ref, kw_ref, wf_ref):
    xb = x_ref[0].astype(BF16)
    h = _dot(xb, w_ref[...])
    qscale = DSA_HEAD_DIM ** -0.5 * LOG2E
    for hd in range(DSA_HEADS):
        sl = slice(_DQ0 + hd * LANES, _DQ0 + (hd + 1) * LANES)
        qa_ref[0, hd] = (h[:, sl] * qscale + qhot_ref[hd:hd + 1, :]).astype(BF16)
    ones_hi = (lax.broadcasted_iota(I32, (1, LANES), 1) >= HALF).astype(F32)
    for g in range(DSA_KV_HEADS):
        ka_ref[0, g] = (h[:, _DK0 + g * LANES:_DK0 + (g + 1) * LANES] + krow_ref[g:g + 1, :]).astype(BF16)
        v2_ref[0, g] = (h[:, _DV0 + g * LANES:_DV0 + (g + 1) * LANES] + ones_hi).astype(BF16)
    for hd in range(IDX_HEADS):
        qi_ref[0, hd] = (h[:, _DI0 + hd * LANES:_DI0 + (hd + 1) * LANES] * (IDX_DIM ** -0.5)).astype(BF16)
    kw = h[:, _DW0:]
    kw_ref[0] = kw.astype(BF16)
    wf_ref[0] = kw * wsc_ref[...]


def _dsa_proj(x, w_in, rel_bias, tm):
    b, s, _ = x.shape
    o1 = DSA_HEADS * DSA_HEAD_DIM
    o2 = o1 + DSA_KV_HEADS * DSA_HEAD_DIM
    o3 = o2 + DSA_KV_HEADS * DSA_HEAD_DIM
    o4 = o3 + IDX_HEADS * IDX_DIM
    o5 = o4 + IDX_DIM
    idx = []
    for hd in range(DSA_HEADS):
        idx += [hd * DSA_HEAD_DIM + j for j in range(DSA_HEAD_DIM)] + [-1] * HALF
    for g in range(DSA_KV_HEADS):
        idx += [o1 + g * DSA_HEAD_DIM + j for j in range(DSA_HEAD_DIM)] + [-1] * HALF
    for g in range(DSA_KV_HEADS):
        idx += [o2 + g * DSA_HEAD_DIM + j for j in range(DSA_HEAD_DIM)] + [-1] * HALF
    for hd in range(IDX_HEADS):
        idx += [o3 + hd * IDX_DIM + j for j in range(IDX_DIM)] + [-1] * HALF
    idx += [o4 + j for j in range(IDX_DIM)] + [o5 + j for j in range(IDX_HEADS)] + [-1] * (HALF - IDX_HEADS)
    assert len(idx) == _DN
    w = _take_cols(w_in, idx).astype(BF16)
    cfar = rel_bias[REL_BUCKETS - 1] * LOG2E
    chi = cfar.astype(BF16).astype(F32)
    clo = (cfar - chi).astype(BF16).astype(F32)
    qhot = np.zeros((DSA_HEADS, LANES), np.float32)
    for hd in range(DSA_HEADS):
        qhot[hd, HALF + hd % DSA_GROUP] = 1.0
        qhot[hd, HALF + DSA_GROUP + hd % DSA_GROUP] = 1.0
    zpad = jnp.zeros((DSA_KV_HEADS, LANES - HALF - 2 * DSA_GROUP), F32)
    krow = jnp.concatenate([jnp.zeros((DSA_KV_HEADS, HALF), F32), chi.reshape(DSA_KV_HEADS, DSA_GROUP),
                            clo.reshape(DSA_KV_HEADS, DSA_GROUP), zpad], axis=1)
    wsc = np.zeros((1, LANES), np.float32)
    wsc[0, HALF:HALF + IDX_HEADS] = IDX_HEADS ** -0.5
    hspec = lambda n: pl.BlockSpec((1, n, tm, LANES), lambda bi, i: (bi, 0, i, 0))
    tspec = pl.BlockSpec((1, tm, LANES), lambda bi, i: (bi, i, 0))
    out_shape = (jax.ShapeDtypeStruct((b, DSA_HEADS, s, LANES), BF16),
                 jax.ShapeDtypeStruct((b, DSA_KV_HEADS, s, LANES), BF16),
                 jax.ShapeDtypeStruct((b, DSA_KV_HEADS, s, LANES), BF16),
                 jax.ShapeDtypeStruct((b, IDX_HEADS, s, LANES), BF16),
                 jax.ShapeDtypeStruct((b, s, LANES), BF16),
                 jax.ShapeDtypeStruct((b, s, LANES), F32))
    return pl.pallas_call(
        _dsa_proj_kernel, out_shape=out_shape, grid=(b, s // tm),
        in_specs=[pl.BlockSpec((1, tm, D_MODEL), lambda bi, i: (bi, i, 0)),
                  _const_spec((D_MODEL, _DN)), _const_spec((DSA_HEADS, LANES)),
                  _const_spec((DSA_KV_HEADS, LANES)), _const_spec((1, LANES))],
        out_specs=(hspec(DSA_HEADS), hspec(DSA_KV_HEADS), hspec(DSA_KV_HEADS), hspec(IDX_HEADS), tspec, tspec),
        compiler_params=_params("parallel", "parallel"), name="dsa_proj",
    )(x, w, jnp.asarray(qhot), krow, jnp.asarray(wsc))


def _t5_tiles_kernel(rb_ref, o_ref):
    d = pl.program_id(0)
    hd = pl.program_id(1)
    row = lax.broadcasted_iota(I32, (LANES, LANES), 0)
    col = lax.broadcasted_iota(I32, (LANES, LANES), 1)
    n = jnp.maximum(d * LANES + row - col, 0)
    max_exact = REL_BUCKETS // 2
    nf = jnp.maximum(n, 1).astype(F32)
    large = max_exact + (jnp.log(nf / max_exact) / math.log(REL_MAX_DIST / max_exact)
                         * (REL_BUCKETS - max_exact)).astype(I32)
    large = jnp.minimum(large, REL_BUCKETS - 1)
    bucket = jnp.where(n < max_exact, n, large)
    far = rb_ref[REL_BUCKETS - 1, hd]
    acc = jnp.zeros((LANES, LANES), F32)
    for bk in range(REL_BUCKETS - 1):
        acc = jnp.where(bucket == bk, rb_ref[bk, hd] - far, acc)
    o_ref[0, 0] = acc * LOG2E


def _t5_tiles(rel_bias):
    return pl.pallas_call(
        _t5_tiles_kernel,
        out_shape=jax.ShapeDtypeStruct((2, DSA_HEADS, LANES, LANES), F32),
        grid=(2, DSA_HEADS),
        in_specs=[pl.BlockSpec(memory_space=pltpu.SMEM)],
        out_specs=pl.BlockSpec((1, 1, LANES, LANES), lambda d, h: (d, h, 0, 0)),
        compiler_params=_params("parallel", "parallel"), name="t5_tiles",
    )(rel_bias)


_SENT_KEY = int(np.int32(np.uint32(0xFF800000) ^ np.uint32(0x7FFFFFFF)))
_HI16 = -65536
_MIN_NORMAL_BITS = 0x00800000
_BITS_ALWAYS = 23
_INT_MIN = -(2 ** 31)


def _dsa_attn_kernel(qa_ref, qi_ref, wf_ref, ka_ref, v_ref, kw_ref, tb_ref, o_ref,
                     keyt_sc, kb_sc, madd_sc, s_sc, m_sc, acc_sc, cut_sc, thr_sc, cge_sc, tcnt_sc,
                     *, t, topk, s_len):
    i = pl.program_id(1)
    nsub = t // LANES
    kidx = lax.broadcasted_iota(I32, (t, t), 0)
    qidx = lax.broadcasted_iota(I32, (t, t), 1)

    qi = qi_ref[0].reshape(IDX_HEADS * t, LANES)
    wt = wf_ref[0].T
    wrows = [wt[HALF + hd:HALF + hd + 1, :] for hd in range(IDX_HEADS)]

    def scores_t(j0, w):
        kc = kw_ref[0, pl.ds(pl.multiple_of(j0 * t, t), w * t), :]
        d = _dot_nt(kc, qi)
        sc = wrows[0] * jnp.maximum(d[:, 0:t], 0.0)
        for hd in range(1, IDX_HEADS):
            sc = sc + wrows[hd] * jnp.maximum(d[:, hd * t:(hd + 1) * t], 0.0)
        return sc

    def store_keys(j, sc, causal):
        if causal:
            sc = jnp.where(kidx <= qidx, sc, -jnp.inf)
        bits = pltpu.bitcast(sc + 0.0, I32)
        keyt_sc[j] = bits ^ ((bits >> 31) & 0x7FFFFFFF)
        kb_sc[j] = pltpu.bitcast(bits & _HI16, F32).astype(BF16)

    def p1(u, c):
        sc = scores_t(2 * u, 2)
        store_keys(2 * u, sc[0:t], False)
        store_keys(2 * u + 1, sc[t:2 * t], False)
        return c

    lax.fori_loop(0, i // 2, p1, 0)

    @pl.when(i % 2 == 1)
    def _():
        sc = scores_t(i - 1, 2)
        store_keys(i - 1, sc[0:t], False)
        store_keys(i, sc[t:2 * t], True)

    @pl.when(i % 2 == 0)
    def _():
        store_keys(i, scores_t(i, 1), True)

    cacc = 32

    kf = float(topk)

    def bisect(nchunks):
        def count_ge(cand):
            acc = jnp.zeros((cacc, t), F32)
            for j in range(nchunks):
                c = jnp.where(keyt_sc[j] >= cand, 1.0, 0.0)
                acc = acc + jnp.sum(c.reshape(t // cacc, cacc, t), axis=0)
            return jnp.sum(acc, axis=0, keepdims=True)

        one_b = jnp.ones((), BF16)
        zero_b = jnp.zeros((), BF16)

        def count_ge16(cand):
            cbits = (cand ^ ((cand >> 31) & 0x7FFFFFFF)) & _HI16
            cbits = jnp.where((cand > 0) & (cand < _MIN_NORMAL_BITS), _MIN_NORMAL_BITS, cbits)
            cf = lax.bitcast_convert_type(cbits, F32)
            cb = jnp.broadcast_to(cf, (16, t)).astype(BF16)
            cb = jnp.broadcast_to(cb[None], (t // 16, 16, t)).reshape(t, t)
            acc = jnp.zeros((cacc, t), BF16)
            for j in range(nchunks):
                c = jnp.where(kb_sc[j] >= cb, one_b, zero_b)
                for k in range(t // cacc):
                    acc = acc + c[k * cacc:(k + 1) * cacc]
            return jnp.sum(acc.astype(F32), axis=0, keepdims=True)

        cnt0 = count_ge16(jnp.zeros((1, t), I32))
        nonneg = cnt0 >= kf
        prefix0 = jnp.where(nonneg, 0, _INT_MIN).astype(I32)
        cge0 = jnp.where(nonneg, cnt0, float(nchunks * t))

        def bit_body(count_fn, bi, carry):
            prefix, cge = carry
            cand = prefix + lax.shift_left(jnp.int32(1), 30 - bi)
            cnt = count_fn(cand)
            take = cnt >= kf
            return jnp.where(take, cand, prefix), jnp.where(take, cnt, cge)

        carry = lax.fori_loop(0, 15, functools.partial(bit_body, count_ge16), (prefix0, cge0))
        carry = lax.fori_loop(15, _BITS_ALWAYS, functools.partial(bit_body, count_ge), carry)

        def more(state):
            bi, prefix, cge = state
            open_ = (cge != kf) & jnp.logical_not(nonneg & (prefix == 0))
            return (bi < 31) & (jnp.max(jnp.where(open_, 1.0, 0.0)) > 0.0)

        def one_bit(state):
            bi, prefix, cge = state
            return (bi + 1,) + bit_body(count_ge, bi, (prefix, cge))

        _, thr, cge = lax.while_loop(more, one_bit, (jnp.int32(_BITS_ALWAYS),) + carry)
        thr_sc[...] = jnp.broadcast_to(thr, thr_sc.shape)
        cge_sc[...] = jnp.broadcast_to(cge, cge_sc.shape)

    for ii in range(s_len // t):
        pl.when(i == ii)(functools.partial(bisect, ii + 1))
    thr = thr_sc[0:1, :]
    cnt_ge = cge_sc[0:1, :]
    excess = jnp.where((cnt_ge > kf) & (thr > _SENT_KEY), 1.0, 0.0)
    cut_sc[...] = jnp.full(cut_sc.shape, s_len, I32)

    @pl.when(jnp.max(excess) > 0.0)
    def _():
        def sweep(j, gt):
            kc = keyt_sc[j]
            tie = jnp.where(kc == thr, 1.0, 0.0)
            tcnt_sc[j] = jnp.broadcast_to(jnp.sum(tie, axis=0, keepdims=True), (8, t))
            g = jnp.where(kc > thr, 1.0, 0.0)
            return gt + jnp.sum(g.reshape(t // cacc, cacc, t), axis=0)

        gt = lax.fori_loop(0, i + 1, sweep, jnp.zeros((cacc, t), F32))
        need = kf - jnp.sum(gt, axis=0, keepdims=True)

        def scan(j, carry):
            run, cstar, base = carry
            new = run + tcnt_sc[j, 0:1, :]
            hit = (run < need) & (new >= need)
            return new, jnp.where(hit, j, cstar), jnp.where(hit, run, base)

        zrow = jnp.zeros((1, t), F32)
        _, cstar, base = lax.fori_loop(0, i + 1, scan, (zrow, jnp.zeros((1, t), I32), zrow))

        def gather(j, tile):
            return jnp.where(cstar == j, keyt_sc[j], tile)

        tile = lax.fori_loop(0, i + 1, gather, jnp.full((t, t), _SENT_KEY, I32))
        tie = jnp.where(tile == thr, 1.0, 0.0)
        need_in = need - base
        rbits = max(1, (t - 1).bit_length())

        def row_body(bi, r):
            cand = r + lax.shift_left(jnp.int32(1), rbits - 1 - bi)
            cnt = jnp.sum(jnp.where(kidx < cand, tie, 0.0), axis=0, keepdims=True)
            return jnp.where(cnt < need_in, cand, r)

        r = lax.fori_loop(0, rbits, row_body, jnp.zeros((1, t), I32))
        cut_sc[...] = jnp.broadcast_to(cstar * t + r, cut_sc.shape)

    cut = cut_sc[0:1, :]

    def to_mask(j, causal):
        kc = keyt_sc[j]
        sel = (kc > thr) | ((kc == thr) & (kidx + j * t <= cut))
        if causal:
            sel = sel & (kidx <= qidx)
        return jnp.where(sel, 0.0, NEG_BIG).T

    m_sc[...] = jnp.full(m_sc.shape, NEG_BIG, F32)
    acc_sc[...] = jnp.zeros(acc_sc.shape, F32)
    gt = DSA_GROUP * t
    near_delta = {"prev": nsub, "diag": 0}

    def attend(j0, parts):
        w = len(parts)
        start = pl.multiple_of(j0 * t, t)

        def qk(g):
            q = qa_ref[0, g * DSA_GROUP:(g + 1) * DSA_GROUP].reshape(gt, LANES)
            return _dot_nt(q, ka_ref[0, g, pl.ds(start, w * t), :])

        s_sc[0, :, 0:w * t] = qk(0)
        for pi, part in enumerate(parts):
            madd_sc[pi] = to_mask(j0 + pi, part == "diag")
        for g in range(DSA_KV_HEADS):
            if g + 1 < DSA_KV_HEADS:
                s_sc[(g + 1) % 2, :, 0:w * t] = qk(g + 1)
            slot = g % 2
            p_rows, a_rows = [], []
            for r in range(DSA_GROUP):
                hd = g * DSA_GROUP + r
                for rs in range(nsub):
                    r0 = r * t + rs * LANES
                    blocks = []
                    for cb in range(w * nsub):
                        part, cs = parts[cb // nsub], cb % nsub
                        blk = s_sc[slot, r0:r0 + LANES, cb * LANES:(cb + 1) * LANES] \
                            + madd_sc[cb // nsub, rs * LANES:(rs + 1) * LANES, cs * LANES:(cs + 1) * LANES]
                        if part in near_delta and near_delta[part] + rs - cs in (0, 1):
                            blk = blk + tb_ref[near_delta[part] + rs - cs, hd]
                        blocks.append(blk)
                    m_prev = m_sc[g, r0:r0 + LANES]
                    m_blk = blocks[0]
                    for blk in blocks[1:]:
                        m_blk = jnp.maximum(m_blk, blk)
                    m_new = jnp.maximum(m_prev, jnp.max(m_blk, axis=-1, keepdims=True))
                    a_rows.append(jnp.exp2(m_prev - m_new))
                    p_rows.append(jnp.concatenate([jnp.exp2(blk - m_new).astype(BF16) for blk in blocks], axis=1))
                    m_sc[g, r0:r0 + LANES] = m_new
            p = jnp.concatenate(p_rows, axis=0)
            a = jnp.concatenate(a_rows, axis=0)
            vc = v_ref[0, g, pl.ds(start, w * t), :]
            acc_sc[g] = a * acc_sc[g] + _dot(p, vc)

    @pl.when(i == 0)
    def _():
        attend(0, ("diag",))

    @pl.when(i >= 1)
    def _():
        nfar = i - 1

        def far_pair(u, c):
            attend(2 * u, ("far", "far"))
            return c

        lax.fori_loop(0, nfar // 2, far_pair, 0)

        @pl.when(nfar % 2 == 1)
        def _():
            attend(i - 2, ("far",))

        attend(i - 1, ("prev", "diag"))

    lane = lax.broadcasted_iota(I32, (t, LANES), 1)
    for g in range(DSA_KV_HEADS):
        acc = acc_sc[g]
        o = acc / acc[:, HALF:HALF + 1]
        for pr in range(DSA_GROUP // 2):
            lo = o[(2 * pr) * t:(2 * pr + 1) * t]
            hi = o[(2 * pr + 1) * t:(2 * pr + 2) * t]
            cblk = g * (DSA_GROUP // 2) + pr
            o_ref[0, :, cblk * LANES:(cblk + 1) * LANES] = \
                jnp.where(lane < HALF, lo, pltpu.roll(hi, HALF, 1)).astype(BF16)


def _dsa_attn(qa, ka, v, qi, kw, wf, tb, t, topk):
    b, _, s, _ = qa.shape
    nch = s // t
    gt = DSA_GROUP * t
    kern = functools.partial(_dsa_attn_kernel, t=t, topk=topk, s_len=s)
    return pl.pallas_call(
        kern, out_shape=jax.ShapeDtypeStruct((b, s, DSA_HEADS * DSA_HEAD_DIM), BF16),
        grid=(b, nch),
        in_specs=[pl.BlockSpec((1, DSA_HEADS, t, LANES), lambda bi, i: (bi, 0, i, 0)),
                  pl.BlockSpec((1, IDX_HEADS, t, LANES), lambda bi, i: (bi, 0, i, 0)),
                  pl.BlockSpec((1, t, LANES), lambda bi, i: (bi, i, 0)),
                  pl.BlockSpec((1, DSA_KV_HEADS, s, LANES), lambda bi, i: (bi, 0, 0, 0)),
                  pl.BlockSpec((1, DSA_KV_HEADS, s, LANES), lambda bi, i: (bi, 0, 0, 0)),
                  pl.BlockSpec((1, s, LANES), lambda bi, i: (bi, 0, 0)),
                  _const_spec((2, DSA_HEADS, LANES, LANES))],
        out_specs=pl.BlockSpec((1, t, DSA_HEADS * DSA_HEAD_DIM), lambda bi, i: (bi, i, 0)),
        scratch_shapes=[pltpu.VMEM((nch, t, t), I32),
                        pltpu.VMEM((nch, t, t), BF16),
                        pltpu.VMEM((2, t, t), F32),
                        pltpu.VMEM((2, gt, 2 * t), F32),
                        pltpu.VMEM((DSA_KV_HEADS, gt, LANES), F32),
                        pltpu.VMEM((DSA_KV_HEADS, gt, LANES), F32),
                        pltpu.VMEM((8, t), I32),
                        pltpu.VMEM((8, t), I32),
                        pltpu.VMEM((8, t), F32),
                        pltpu.VMEM((nch, 8, t), F32)],
        compiler_params=_params("parallel", "arbitrary"), name="dsa_attn",
    )(qa, qi, wf, ka, v, kw, tb)


def _mm_kernel(a_ref, w_ref, o_ref):
    o_ref[...] = _dot(a_ref[...].astype(BF16), w_ref[...]).astype(BF16)


def _mm(a, w, tm):
    m, k = a.shape
    n = w.shape[1]
    return pl.pallas_call(
        _mm_kernel, out_shape=jax.ShapeDtypeStruct((m, n), BF16), grid=(m // tm,),
        in_specs=[pl.BlockSpec((tm, k), lambda i: (i, 0)), _const_spec((k, n))],
        out_specs=pl.BlockSpec((tm, n), lambda i: (i, 0)),
        compiler_params=_params("parallel"), name="mem_kv",
    )(a, w)


def _post_kernel(a_ref, x_ref, wo_ref, wq_ref, kv_ref, wxo_ref, g_ref, b_ref, o_ref, oc_sc):
    y = _dot(a_ref[0], wo_ref[...])
    x1 = _layer_norm(ALPHA * x_ref[0] + y, g_ref[0:1, :], b_ref[0:1, :])
    q = (_dot(x1.astype(BF16), wq_ref[...]) * (XA_HEAD_DIM ** -0.5 * LOG2E)).astype(BF16)
    for hd in range(XA_HEADS):
        sl = slice(hd * XA_HEAD_DIM, (hd + 1) * XA_HEAD_DIM)
        sc = _dot_nt(q[:, sl], kv_ref[0, :, sl])
        m = jnp.max(sc, axis=-1, keepdims=True)
        p = jnp.exp2(sc - m)
        l = jnp.sum(p, axis=-1, keepdims=True)
        vsl = slice(D_MODEL + hd * XA_HEAD_DIM, D_MODEL + (hd + 1) * XA_HEAD_DIM)
        oc_sc[:, sl] = (_dot(p.astype(BF16), kv_ref[0, :, vsl]) / l).astype(BF16)
    y2 = _dot(oc_sc[...], wxo_ref[...])
    o_ref[0] = _layer_norm(ALPHA * x1 + y2, g_ref[1:2, :], b_ref[1:2, :])


def _post(a, x, w_o, w_q, kv, w_xo, g, bta, tm):
    b, s, _ = x.shape
    mlen = kv.shape[1]
    tok = lambda bi, i: (bi, i, 0)
    return pl.pallas_call(
        _post_kernel, out_shape=jax.ShapeDtypeStruct((b, s, D_MODEL), F32), grid=(b, s // tm),
        in_specs=[pl.BlockSpec((1, tm, D_MODEL), tok), pl.BlockSpec((1, tm, D_MODEL), tok),
                  _const_spec((D_MODEL, D_MODEL)), _const_spec((D_MODEL, D_MODEL)),
                  pl.BlockSpec((1, mlen, 2 * D_MODEL), lambda bi, i: (bi, 0, 0)),
                  _const_spec((D_MODEL, D_MODEL)), _const_spec((3, D_MODEL)), _const_spec((3, D_MODEL))],
        out_specs=pl.BlockSpec((1, tm, D_MODEL), tok),
        scratch_shapes=[pltpu.VMEM((tm, D_MODEL), BF16)],
        compiler_params=_params("parallel", "parallel"), name="post_attn",
    )(a, x, w_o, w_q, kv, w_xo, g, bta)


FFN_CHUNK = 256


def _ffn_kernel(x_ref, wi_ref, wd_ref, g_ref, b_ref, o_ref, act_sc):
    x = x_ref[...]
    xb = x.astype(BF16)
    for c in range(FFN_HIDDEN // FFN_CHUNK):
        sl = slice(c * FFN_CHUNK, (c + 1) * FFN_CHUNK)
        gte = _dot(xb, wi_ref[:, sl])
        up = _dot(xb, wi_ref[:, FFN_HIDDEN + c * FFN_CHUNK:FFN_HIDDEN + (c + 1) * FFN_CHUNK])
        act_sc[:, sl] = (gte * jax.nn.sigmoid(gte) * up).astype(BF16)
    y = _dot(act_sc[...], wd_ref[...])
    o_ref[...] = _layer_norm(ALPHA * x + y, g_ref[2:3, :], b_ref[2:3, :])


def _ffn(x2d, w_in, w_down, g, bta, tm):
    n = x2d.shape[0]
    return pl.pallas_call(
        _ffn_kernel, out_shape=jax.ShapeDtypeStruct((n, D_MODEL), F32), grid=(n // tm,),
        in_specs=[pl.BlockSpec((tm, D_MODEL), lambda i: (i, 0)),
                  _const_spec((D_MODEL, 2 * FFN_HIDDEN)), _const_spec((FFN_HIDDEN, D_MODEL)),
                  _const_spec((3, D_MODEL)), _const_spec((3, D_MODEL))],
        out_specs=pl.BlockSpec((tm, D_MODEL), lambda i: (i, 0)),
        scratch_shapes=[pltpu.VMEM((tm, FFN_HIDDEN), BF16)],
        compiler_params=_params("parallel"), name="ffn",
    )(x2d, w_in.astype(BF16), w_down.astype(BF16), g, bta)


def _tile(n, pref):
    t = min(n, pref)
    assert n % t == 0, (n, pref)
    return t


def kernel(x, mem, positions, rel_bias, mla_w_in, mla_q_norm, mla_w_uq, mla_kv_norm, mla_w_ukv, mla_w_o,
           dsa_w_in, dsa_w_o, xa_w_q, xa_w_kv, xa_w_o, ffn_w_in, ffn_w_down, ln_g, ln_b):
    b, s, d = x.shape
    assert d == D_MODEL and s % LANES == 0
    mlen = mem.shape[1]
    topk = min(TOPK_MAX, s // 4)
    t_attn = _tile(s, 256)
    tm_proj = _tile(s, 512)
    tm_post = _tile(s, 1024)
    tm_ffn = _tile(b * s, 1024)
    pos3 = positions.reshape(b, s, 1)
    tb = _t5_tiles(rel_bias)
    for i in range(DEPTH):
        j = i // 2
        if i % 2 == 0:
            q, k, v = _mla_proj(x, pos3, mla_w_in[j], mla_q_norm[j], mla_w_uq[j], mla_kv_norm[j],
                                mla_w_ukv[j], tm_proj)
            a = _flash_mla(q, k, v, _tile(s, 1024))
            w_o = mla_w_o[j]
        else:
            qa, ka, v2, qi, kw, wf = _dsa_proj(x, dsa_w_in[j], rel_bias, tm_proj)
            a = _dsa_attn(qa, ka, v2, qi, kw, wf, tb, t_attn, topk)
            w_o = dsa_w_o[j]
        kv = _mm(mem.reshape(b * mlen, d), xa_w_kv[i].astype(BF16), _tile(b * mlen, 512)).reshape(b, mlen, 2 * d)
        x = _post(a, x, w_o.astype(BF16), xa_w_q[i].astype(BF16), kv, xa_w_o[i].astype(BF16),
                  ln_g[i], ln_b[i], tm_post)
        x = _ffn(x.reshape(b * s, d), ffn_w_in[i], ffn_w_down[i], ln_g[i], ln_b[i], tm_ffn).reshape(b, s, d)
    return x
```

```python
import functools
import math

import numpy as np
import jax
import jax.numpy as jnp
from jax import lax
from jax.experimental import pallas as pl
from jax.experimental.pallas import tpu as pltpu

D_MODEL = 1024
DEPTH = 2
MLA_HEADS = 16
MLA_Q_RANK = 512
MLA_KV_RANK = 256
MLA_NOPE = 64
MLA_ROPE = 32
MLA_V = 64
DSA_HEADS = 16
DSA_KV_HEADS = 4
DSA_GROUP = DSA_HEADS // DSA_KV_HEADS
DSA_HEAD_DIM = 64
IDX_HEADS = 8
IDX_DIM = 64
TOPK_MAX = 256
XA_HEADS = 4
XA_HEAD_DIM = D_MODEL // XA_HEADS
FFN_HIDDEN = -(-8 * D_MODEL // (3 * 256)) * 256
REL_BUCKETS = 32
REL_MAX_DIST = 128
ROPE_BASE = 10000.0
LN_EPS = 1e-5
RMS_EPS = 1e-6
ALPHA = (2 * DEPTH) ** 0.25

LANES = 128
HALF = LANES // 2
VMEM_LIMIT = 56 * 1024 * 1024

LOG2E = math.log2(math.e)
NEG_BIG = -1e30
F32 = jnp.float32
BF16 = jnp.bfloat16
I32 = jnp.int32
_NT = (((1,), (1,)), ((), ()))


def _dot(a, b):
    return jnp.dot(a, b, preferred_element_type=F32)


def _dot_nt(a, b):
    return lax.dot_general(a, b, _NT, preferred_element_type=F32)


def _layer_norm(y, g, b):
    mu = jnp.mean(y, axis=-1, keepdims=True)
    yc = y - mu
    var = jnp.mean(yc * yc, axis=-1, keepdims=True)
    return yc * lax.rsqrt(var + LN_EPS) * g + b


def _rms_norm(y, g):
    return y * lax.rsqrt(jnp.mean(y * y, axis=-1, keepdims=True) + RMS_EPS) * g


def _params(*sem):
    return pltpu.CompilerParams(dimension_semantics=sem, vmem_limit_bytes=VMEM_LIMIT)


def _const_spec(shape):
    nd = len(shape)
    return pl.BlockSpec(shape, lambda *_: (0,) * nd)


def _take_cols(w, idx):
    idx = np.asarray(idx)
    wp = jnp.concatenate([w, jnp.zeros((w.shape[0], 1), w.dtype)], axis=1)
    return wp[:, np.where(idx < 0, w.shape[1], idx)]


_R = MLA_ROPE // 2
_MLA_W = MLA_Q_RANK + MLA_KV_RANK


def _mla_proj_kernel(x_ref, pos_ref, win_ref, qg_ref, kvg_ref, wuq_ref, wuk_ref, wuv_ref,
                     freq_ref, sgn_ref, q_ref, k_ref, v_ref):
    xb = x_ref[0].astype(BF16)
    h = _dot(xb, win_ref[...])
    cq = _rms_norm(h[:, :MLA_Q_RANK], qg_ref[...]).astype(BF16)
    ckv = _rms_norm(h[:, MLA_Q_RANK:_MLA_W], kvg_ref[...]).astype(BF16)
    kr = h[:, _MLA_W:]
    ang = pos_ref[0].astype(F32) * freq_ref[...]
    cos = jnp.cos(ang)
    sin = jnp.sin(ang) * sgn_ref[...]

    def rope(t):
        return t * cos + pltpu.roll(t, HALF, 1) * sin

    kr = rope(kr)
    q_all = _dot(cq, wuq_ref[...])
    k_all = _dot(ckv, wuk_ref[...])
    v_all = _dot(ckv, wuv_ref[...])
    qscale = (MLA_NOPE + MLA_ROPE) ** -0.5 * LOG2E
    ones_hi = (lax.broadcasted_iota(I32, (1, LANES), 1) >= HALF).astype(F32)
    for hd in range(MLA_HEADS):
        sl = slice(hd * LANES, (hd + 1) * LANES)
        q_ref[0, hd] = (rope(q_all[:, sl]) * qscale).astype(BF16)
        k_ref[0, hd] = (k_all[:, sl] + kr).astype(BF16)
        v_ref[0, hd] = (v_all[:, sl] + ones_hi).astype(BF16)


def _mla_proj(x, pos3, w_in, q_norm, w_uq, kv_norm, w_ukv, tm):
    b, s, _ = x.shape
    hq = MLA_NOPE + MLA_ROPE
    in_idx = list(range(_MLA_W)) + [_MLA_W + j for j in range(_R)] + [-1] * (HALF - _R) \
        + [_MLA_W + _R + j for j in range(_R)] + [-1] * (HALF - _R)
    q_idx, k_idx, v_idx = [], [], []
    n_lo = HALF - _R
    n_hi = MLA_NOPE - n_lo
    tail = [-1] * (HALF - _R - n_hi)
    for hd in range(MLA_HEADS):
        qb = hd * hq
        q_idx += [qb + MLA_NOPE + j for j in range(_R)] + [qb + j for j in range(n_lo)] \
            + [qb + MLA_NOPE + _R + j for j in range(_R)] + [qb + n_lo + j for j in range(n_hi)] + tail
        kb = hd * (MLA_NOPE + MLA_V)
        k_idx += [-1] * _R + [kb + j for j in range(n_lo)] + [-1] * _R + [kb + n_lo + j for j in range(n_hi)] + tail
        v_idx += [kb + MLA_NOPE + j for j in range(MLA_V)] + [-1] * (LANES - MLA_V)
    win = _take_cols(w_in, in_idx).astype(BF16)
    wuq = _take_cols(w_uq, q_idx).astype(BF16)
    wuk = _take_cols(w_ukv, k_idx).astype(BF16)
    wuv = _take_cols(w_ukv, v_idx).astype(BF16)
    inv_freq = ROPE_BASE ** (-jnp.arange(0, MLA_ROPE, 2, dtype=F32) / MLA_ROPE)
    zpad = jnp.zeros((HALF - _R,), F32)
    freq = jnp.concatenate([inv_freq, zpad, inv_freq, zpad])[None]
    sgn = jnp.concatenate([-jnp.ones((_R,), F32), zpad, jnp.ones((_R,), F32), zpad])[None]
    nq = MLA_HEADS * LANES
    grid = (b, s // tm)
    out_shape = (jax.ShapeDtypeStruct((b, MLA_HEADS, s, LANES), BF16),
                 jax.ShapeDtypeStruct((b, MLA_HEADS, s, LANES), BF16),
                 jax.ShapeDtypeStruct((b, MLA_HEADS, s, LANES), BF16))
    return pl.pallas_call(
        _mla_proj_kernel, out_shape=out_shape, grid=grid,
        in_specs=[pl.BlockSpec((1, tm, D_MODEL), lambda bi, i: (bi, i, 0)),
                  pl.BlockSpec((1, tm, 1), lambda bi, i: (bi, i, 0)),
                  _const_spec((D_MODEL, _MLA_W + LANES)),
                  _const_spec((1, MLA_Q_RANK)), _const_spec((1, MLA_KV_RANK)),
                  _const_spec((MLA_Q_RANK, nq)), _const_spec((MLA_KV_RANK, nq)),
                  _const_spec((MLA_KV_RANK, nq)),
                  _const_spec((1, LANES)), _const_spec((1, LANES))],
        out_specs=(pl.BlockSpec((1, MLA_HEADS, tm, LANES), lambda bi, i: (bi, 0, i, 0)),
                   pl.BlockSpec((1, MLA_HEADS, tm, LANES), lambda bi, i: (bi, 0, i, 0)),
                   pl.BlockSpec((1, MLA_HEADS, tm, LANES), lambda bi, i: (bi, 0, i, 0))),
        compiler_params=_params("parallel", "parallel"), name="mla_proj",
    )(x, pos3, win, q_norm[None], kv_norm[None], wuq, wuk, wuv, freq, sgn)


def _flash_kernel(q_ref, k_ref, v_ref, o_ref, m_sc, acc_sc, s_sc, *, t, nh):
    i = pl.program_id(2)
    tk = t // 2
    row = lax.broadcasted_iota(I32, (t, tk), 0)
    col = lax.broadcasted_iota(I32, (t, tk), 1)
    m_sc[...] = jnp.full(m_sc.shape, NEG_BIG, F32)
    acc_sc[...] = jnp.zeros(acc_sc.shape, F32)

    def keys(j, hh):
        return k_ref[0, hh, pl.ds(pl.multiple_of(j * tk, tk), tk), :]

    def scores(j, hh):
        return _dot_nt(q_ref[0, hh].reshape(t, LANES), keys(j, hh)).reshape(2, tk, tk)

    def softmax_pv(sc, m_prev, acc_prev, j, hh):
        m_new = jnp.maximum(m_prev, jnp.max(sc, axis=-1, keepdims=True))
        a = jnp.exp2(m_prev - m_new)
        p = jnp.concatenate([jnp.exp2(sc[:, c * LANES:(c + 1) * LANES] - m_new)
                             for c in range(tk // LANES)], axis=1).astype(BF16)
        vc = v_ref[0, hh, pl.ds(pl.multiple_of(j * tk, tk), tk), :]
        return m_new, a * acc_prev + _dot(p, vc)

    def step(j, slot, own=False):
        for hh in range(nh):
            sc = s_sc[slot, hh].reshape(t, tk)
            if own:
                sc = jnp.where((row >= tk) | (col <= row), sc, NEG_BIG)
                s_sc[1 - slot, hh, 1] = _dot_nt(q_ref[0, hh, 1], keys(j + 1, hh))
            else:
                s_sc[1 - slot, hh] = scores(j + 1, hh)
            m_new, acc = softmax_pv(sc, m_sc[hh].reshape(t, LANES), acc_sc[hh].reshape(t, LANES), j, hh)
            m_sc[hh] = m_new.reshape(2, tk, LANES)
            acc_sc[hh] = acc.reshape(2, tk, LANES)

    def last_step(j, slot):
        for hh in range(nh):
            tri = lax.broadcasted_iota(I32, (tk, tk), 1) <= lax.broadcasted_iota(I32, (tk, tk), 0)
            sc = jnp.where(tri, s_sc[slot, hh, 1], NEG_BIG)
            m_sc[hh, 1], acc_sc[hh, 1] = softmax_pv(sc, m_sc[hh, 1], acc_sc[hh, 1], j, hh)

    for hh in range(nh):
        s_sc[0, hh] = scores(0, hh)

    def pair(u, c):
        step(2 * u, 0)
        step(2 * u + 1, 1)
        return c

    lax.fori_loop(0, i, pair, 0)
    step(2 * i, 0, own=True)
    last_step(2 * i + 1, 1)

    lane = lax.broadcasted_iota(I32, (t, LANES), 1)
    for pr in range(nh // 2):
        a0 = acc_sc[2 * pr].reshape(t, LANES)
        a1 = acc_sc[2 * pr + 1].reshape(t, LANES)
        o0 = a0 / a0[:, HALF:HALF + 1]
        o1 = a1 / a1[:, HALF:HALF + 1]
        o_ref[0, :, pr * LANES:(pr + 1) * LANES] = \
            jnp.where(lane < HALF, o0, pltpu.roll(o1, HALF, 1)).astype(BF16)


MLA_HEADS_PER_STEP = 4


def _flash_mla(q, k, v, t):
    b, h, s, _ = q.shape
    nh = MLA_HEADS_PER_STEP
    tk = t // 2
    return pl.pallas_call(
        functools.partial(_flash_kernel, t=t, nh=nh),
        out_shape=jax.ShapeDtypeStruct((b, s, h * MLA_V), BF16),
        grid=(b, h // nh, s // t),
        in_specs=[pl.BlockSpec((1, nh, 2, tk, LANES), lambda bi, p, i: (bi, p, i, 0, 0)),
                  pl.BlockSpec((1, nh, s, LANES), lambda bi, p, i: (bi, p, 0, 0)),
                  pl.BlockSpec((1, nh, s, LANES), lambda bi, p, i: (bi, p, 0, 0))],
        out_specs=pl.BlockSpec((1, t, nh * MLA_V), lambda bi, p, i: (bi, i, p)),
        scratch_shapes=[pltpu.VMEM((nh, 2, tk, LANES), F32), pltpu.VMEM((nh, 2, tk, LANES), F32),
                        pltpu.VMEM((2, nh, 2, tk, tk), F32)],
        compiler_params=_params("parallel", "parallel", "arbitrary"), name="mla_flash",
    )(q.reshape(b, h, s // tk, tk, LANES), k, v)


_DQ0 = 0
_DK0 = DSA_HEADS * DSA_HEAD_DIM
_DV0 = _DK0 + DSA_KV_HEADS * DSA_HEAD_DIM
_DI0 = _DV0 + DSA_KV_HEADS * DSA_HEAD_DIM
_DW0 = _DI0 + IDX_HEADS * IDX_DIM
_DN = _DW0 + LANES


def _dsa_proj_kernel(x_ref, w_ref, qhot_ref, krow_ref, wsc_ref, qa_ref, ka_ref, v2_ref, qi_ref, kw_ref, wf_ref):
    xb = x_ref[0].astype(BF16)
    h = _dot(xb, w_ref[...])
    lo = lax.broadcasted_iota(I32, (1, LANES), 1) < HALF

    def place(col0, n_heads, scale, fill, out_ref):
        for p in range(n_heads // 2):
            blk = h[:, col0 + p * LANES:col0 + (p + 1) * LANES] * scale
            for e, src in enumerate((blk, pltpu.roll(blk, HALF, 1))):
                hd = 2 * p + e
                out_ref[0, hd] = jnp.where(lo, src, fill(hd)).astype(BF16)

    place(_DQ0, DSA_HEADS, DSA_HEAD_DIM ** -0.5 * LOG2E, lambda hd: qhot_ref[hd:hd + 1, :], qa_ref)
    place(_DK0, DSA_KV_HEADS, 1.0, lambda g: krow_ref[g:g + 1, :], ka_ref)
    place(_DV0, DSA_KV_HEADS, 1.0, lambda g: 1.0, v2_ref)
    place(_DI0, IDX_HEADS, IDX_DIM ** -0.5, lambda hd: 0.0, qi_ref)
    kw = h[:, _DW0:]
    kw_ref[0] = kw.astype(BF16)
    wf_ref[0] = kw * wsc_ref[...]


def _dsa_proj(x, w_in, rel_bias, tm):
    b, s, _ = x.shape
    n_in = w_in.shape[1]
    assert n_in == _DW0 + IDX_DIM + IDX_HEADS
    w = jnp.concatenate([w_in, jnp.zeros((D_MODEL, _DN - n_in), w_in.dtype)], axis=1).astype(BF16)
    cfar = rel_bias[REL_BUCKETS - 1] * LOG2E
    chi = cfar.astype(BF16).astype(F32)
    clo = (cfar - chi).astype(BF16).astype(F32)
    qhot = np.zeros((DSA_HEADS, LANES), np.float32)
    for hd in range(DSA_HEADS):
        qhot[hd, HALF + hd % DSA_GROUP] = 1.0
        qhot[hd, HALF + DSA_GROUP + hd % DSA_GROUP] = 1.0
    zpad = jnp.zeros((DSA_KV_HEADS, LANES - HALF - 2 * DSA_GROUP), F32)
    krow = jnp.concatenate([jnp.zeros((DSA_KV_HEADS, HALF), F32), chi.reshape(DSA_KV_HEADS, DSA_GROUP),
                            clo.reshape(DSA_KV_HEADS, DSA_GROUP), zpad], axis=1)
    wsc = np.zeros((1, LANES), np.float32)
    wsc[0, HALF:HALF + IDX_HEADS] = IDX_HEADS ** -0.5
    hspec = lambda n: pl.BlockSpec((1, n, tm, LANES), lambda bi, i: (bi, 0, i, 0))
    tspec = pl.BlockSpec((1, tm, LANES), lambda bi, i: (bi, i, 0))
    out_shape = (jax.ShapeDtypeStruct((b, DSA_HEADS, s, LANES), BF16),
                 jax.ShapeDtypeStruct((b, DSA_KV_HEADS, s, LANES), BF16),
                 jax.ShapeDtypeStruct((b, DSA_KV_HEADS, s, LANES), BF16),
                 jax.ShapeDtypeStruct((b, IDX_HEADS, s, LANES), BF16),
                 jax.ShapeDtypeStruct((b, s, LANES), BF16),
                 jax.ShapeDtypeStruct((b, s, LANES), F32))
    return pl.pallas_call(
        _dsa_proj_kernel, out_shape=out_shape, grid=(b, s // tm),
        in_specs=[pl.BlockSpec((1, tm, D_MODEL), lambda bi, i: (bi, i, 0)),
                  _const_spec((D_MODEL, _DN)), _const_spec((DSA_HEADS, LANES)),
                  _const_spec((DSA_KV_HEADS, LANES)), _const_spec((1, LANES))],
        out_specs=(hspec(DSA_HEADS), hspec(DSA_KV_HEADS), hspec(DSA_KV_HEADS), hspec(IDX_HEADS), tspec, tspec),
        compiler_params=_params("parallel", "parallel"), name="dsa_proj",
    )(x, w, jnp.asarray(qhot), krow, jnp.asarray(wsc))


def _t5_tiles_kernel(rb_ref, o_ref):
    d = pl.program_id(0)
    hd = pl.program_id(1)
    row = lax.broadcasted_iota(I32, (LANES, LANES), 0)
    col = lax.broadcasted_iota(I32, (LANES, LANES), 1)
    n = jnp.maximum(d * LANES + row - col, 0)
    max_exact = REL_BUCKETS // 2
    nf = jnp.maximum(n, 1).astype(F32)
    large = max_exact + (jnp.log(nf / max_exact) / math.log(REL_MAX_DIST / max_exact)
                         * (REL_BUCKETS - max_exact)).astype(I32)
    large = jnp.minimum(large, REL_BUCKETS - 1)
    bucket = jnp.where(n < max_exact, n, large)
    far = rb_ref[REL_BUCKETS - 1, hd]
    acc = jnp.zeros((LANES, LANES), F32)
    for bk in range(REL_BUCKETS - 1):
        acc = jnp.where(bucket == bk, rb_ref[bk, hd] - far, acc)
    o_ref[0, 0] = acc * LOG2E


def _t5_tiles(rel_bias):
    return pl.pallas_call(
        _t5_tiles_kernel,
        out_shape=jax.ShapeDtypeStruct((2, DSA_HEADS, LANES, LANES), F32),
        grid=(2, DSA_HEADS),
        in_specs=[pl.BlockSpec(memory_space=pltpu.SMEM)],
        out_specs=pl.BlockSpec((1, 1, LANES, LANES), lambda d, h: (d, h, 0, 0)),
        compiler_params=_params("parallel", "parallel"), name="t5_tiles",
    )(rel_bias)


_SENT_KEY = int(np.int32(np.uint32(0xFF800000) ^ np.uint32(0x7FFFFFFF)))
_HI16 = -65536
_MIN_NORMAL_BITS = 0x00800000
_BITS_ALWAYS = 23
_INT_MIN = -(2 ** 31)


def _dsa_attn_kernel(qa_ref, qi_ref, wf_ref, ka_ref, v_ref, kw_ref, tb_ref, o_ref,
                     keyt_sc, kb_sc, madd_sc, s_sc, m_sc, acc_sc, cut_sc, thr_sc, cge_sc, tcnt_sc,
                     *, t, topk, s_len):
    i = pl.program_id(1)
    nsub = t // LANES
    kidx = lax.broadcasted_iota(I32, (t, t), 0)
    qidx = lax.broadcasted_iota(I32, (t, t), 1)

    qi = qi_ref[0].reshape(IDX_HEADS * t, LANES)
    wt = wf_ref[0].T
    wrows = [wt[HALF + hd:HALF + hd + 1, :] for hd in range(IDX_HEADS)]

    def scores_t(j0, w):
        kc = kw_ref[0, pl.ds(pl.multiple_of(j0 * t, t), w * t), :]
        d = _dot_nt(kc, qi)
        sc = wrows[0] * jnp.maximum(d[:, 0:t], 0.0)
        for hd in range(1, IDX_HEADS):
            sc = sc + wrows[hd] * jnp.maximum(d[:, hd * t:(hd + 1) * t], 0.0)
        return sc

    def store_keys(j, sc, causal):
        if causal:
            sc = jnp.where(kidx <= qidx, sc, -jnp.inf)
        bits = pltpu.bitcast(sc + 0.0, I32)
        keyt_sc[j] = bits ^ ((bits >> 31) & 0x7FFFFFFF)
        kb_sc[j] = pltpu.bitcast(bits & _HI16, F32).astype(BF16)

    def p1(u, c):
        sc = scores_t(2 * u, 2)
        store_keys(2 * u, sc[0:t], False)
        store_keys(2 * u + 1, sc[t:2 * t], False)
        return c

    lax.fori_loop(0, i // 2, p1, 0)

    @pl.when(i % 2 == 1)
    def _():
        sc = scores_t(i - 1, 2)
        store_keys(i - 1, sc[0:t], False)
        store_keys(i, sc[t:2 * t], True)

    @pl.when(i % 2 == 0)
    def _():
        store_keys(i, scores_t(i, 1), True)

    cacc = 32

    kf = float(topk)

    def bisect(nchunks):
        def count_ge(cand):
            acc = jnp.zeros((cacc, t), F32)
            for j in range(nchunks):
                c = jnp.where(keyt_sc[j] >= cand, 1.0, 0.0)
                acc = acc + jnp.sum(c.reshape(t // cacc, cacc, t), axis=0)
            return jnp.sum(acc, axis=0, keepdims=True)

        one_b = jnp.ones((), BF16)
        zero_b = jnp.zeros((), BF16)

        def count_ge16(cand):
            cbits = (cand ^ ((cand >> 31) & 0x7FFFFFFF)) & _HI16
            cbits = jnp.where((cand > 0) & (cand < _MIN_NORMAL_BITS), _MIN_NORMAL_BITS, cbits)
            cf = lax.bitcast_convert_type(cbits, F32)
            cb = jnp.broadcast_to(cf, (16, t)).astype(BF16)
            cb = jnp.broadcast_to(cb[None], (t // 16, 16, t)).reshape(t, t)
            acc = jnp.zeros((cacc, t), BF16)
            for j in range(nchunks):
                c = jnp.where(kb_sc[j] >= cb, one_b, zero_b)
                for k in range(t // cacc):
                    acc = acc + c[k * cacc:(k + 1) * cacc]
            return jnp.sum(acc.astype(F32), axis=0, keepdims=True)

        cnt0 = count_ge16(jnp.zeros((1, t), I32))
        nonneg = cnt0 >= kf
        prefix0 = jnp.where(nonneg, 0, _INT_MIN).astype(I32)
        cge0 = jnp.where(nonneg, cnt0, float(nchunks * t))

        def bit_body(count_fn, bi, carry):
            prefix, cge = carry
            cand = prefix + lax.shift_left(jnp.int32(1), 30 - bi)
            cnt = count_fn(cand)
            take = cnt >= kf
            return jnp.where(take, cand, prefix), jnp.where(take, cnt, cge)

        carry = lax.fori_loop(0, 15, functools.partial(bit_body, count_ge16), (prefix0, cge0))
        carry = lax.fori_loop(15, _BITS_ALWAYS, functools.partial(bit_body, count_ge), carry)

        def more(state):
            bi, prefix, cge = state
            open_ = (cge != kf) & jnp.logical_not(nonneg & (prefix == 0))
            return (bi < 31) & (jnp.max(jnp.where(open_, 1.0, 0.0)) > 0.0)

        def one_bit(state):
            bi, prefix, cge = state
            return (bi + 1,) + bit_body(count_ge, bi, (prefix, cge))

        _, thr, cge = lax.while_loop(more, one_bit, (jnp.int32(_BITS_ALWAYS),) + carry)
        thr_sc[...] = jnp.broadcast_to(thr, thr_sc.shape)
        cge_sc[...] = jnp.broadcast_to(cge, cge_sc.shape)

    for ii in range(s_len // t):
        pl.when(i == ii)(functools.partial(bisect, ii + 1))
    thr = thr_sc[0:1, :]
    cnt_ge = cge_sc[0:1, :]
    excess = jnp.where((cnt_ge > kf) & (thr > _SENT_KEY), 1.0, 0.0)
    cut_sc[...] = jnp.full(cut_sc.shape, s_len, I32)

    @pl.when(jnp.max(excess) > 0.0)
    def _():
        def sweep(j, gt):
            kc = keyt_sc[j]
            tie = jnp.where(kc == thr, 1.0, 0.0)
            tcnt_sc[j] = jnp.broadcast_to(jnp.sum(tie, axis=0, keepdims=True), (8, t))
            g = jnp.where(kc > thr, 1.0, 0.0)
            return gt + jnp.sum(g.reshape(t // cacc, cacc, t), axis=0)

        gt = lax.fori_loop(0, i + 1, sweep, jnp.zeros((cacc, t), F32))
        need = kf - jnp.sum(gt, axis=0, keepdims=True)

        def scan(j, carry):
            run, cstar, base = carry
            new = run + tcnt_sc[j, 0:1, :]
            hit = (run < need) & (new >= need)
            return new, jnp.where(hit, j, cstar), jnp.where(hit, run, base)

        zrow = jnp.zeros((1, t), F32)
        _, cstar, base = lax.fori_loop(0, i + 1, scan, (zrow, jnp.zeros((1, t), I32), zrow))

        def gather(j, tile):
            return jnp.where(cstar == j, keyt_sc[j], tile)

        tile = lax.fori_loop(0, i + 1, gather, jnp.full((t, t), _SENT_KEY, I32))
        tie = jnp.where(tile == thr, 1.0, 0.0)
        need_in = need - base
        rbits = max(1, (t - 1).bit_length())

        def row_body(bi, r):
            cand = r + lax.shift_left(jnp.int32(1), rbits - 1 - bi)
            cnt = jnp.sum(jnp.where(kidx < cand, tie, 0.0), axis=0, keepdims=True)
            return jnp.where(cnt < need_in, cand, r)

        r = lax.fori_loop(0, rbits, row_body, jnp.zeros((1, t), I32))
        cut_sc[...] = jnp.broadcast_to(cstar * t + r, cut_sc.shape)

    cut = cut_sc[0:1, :]

    def to_mask(j, causal):
        kc = keyt_sc[j]
        sel = (kc > thr) | ((kc == thr) & (kidx + j * t <= cut))
        if causal:
            sel = sel & (kidx <= qidx)
        return jnp.where(sel, 0.0, NEG_BIG).T

    m_sc[...] = jnp.full(m_sc.shape, NEG_BIG, F32)
    acc_sc[...] = jnp.zeros(acc_sc.shape, F32)
    gt = DSA_GROUP * t
    near_delta = {"prev": nsub, "diag": 0}

    def attend(j0, parts):
        w = len(parts)
        start = pl.multiple_of(j0 * t, t)

        def qk(g):
            q = qa_ref[0, g * DSA_GROUP:(g + 1) * DSA_GROUP].reshape(gt, LANES)
            return _dot_nt(q, ka_ref[0, g, pl.ds(start, w * t), :])

        s_sc[0, :, 0:w * t] = qk(0)
        for pi, part in enumerate(parts):
            madd_sc[pi] = to_mask(j0 + pi, part == "diag")
        for g in range(DSA_KV_HEADS):
            if g + 1 < DSA_KV_HEADS:
                s_sc[(g + 1) % 2, :, 0:w * t] = qk(g + 1)
            slot = g % 2
            p_rows, a_rows = [], []
            for r in range(DSA_GROUP):
                hd = g * DSA_GROUP + r
                for rs in range(nsub):
                    r0 = r * t + rs * LANES
                    blocks = []
                    for cb in range(w * nsub):
                        part, cs = parts[cb // nsub], cb % nsub
                        blk = s_sc[slot, r0:r0 + LANES, cb * LANES:(cb + 1) * LANES] \
                            + madd_sc[cb // nsub, rs * LANES:(rs + 1) * LANES, cs * LANES:(cs + 1) * LANES]
                        if part in near_delta and near_delta[part] + rs - cs in (0, 1):
                            blk = blk + tb_ref[near_delta[part] + rs - cs, hd]
                        blocks.append(blk)
                    m_prev = m_sc[g, r0:r0 + LANES]
                    m_blk = blocks[0]
                    for blk in blocks[1:]:
                        m_blk = jnp.maximum(m_blk, blk)
                    m_new = jnp.maximum(m_prev, jnp.max(m_blk, axis=-1, keepdims=True))
                    a_rows.append(jnp.exp2(m_prev - m_new))
                    p_rows.append(jnp.concatenate([jnp.exp2(blk - m_new).astype(BF16) for blk in blocks], axis=1))
                    m_sc[g, r0:r0 + LANES] = m_new
            p = jnp.concatenate(p_rows, axis=0)
            a = jnp.concatenate(a_rows, axis=0)
            vc = v_ref[0, g, pl.ds(start, w * t), :]
            acc_sc[g] = a * acc_sc[g] + _dot(p, vc)

    @pl.when(i == 0)
    def _():
        attend(0, ("diag",))

    @pl.when(i >= 1)
    def _():
        nfar = i - 1

        def far_pair(u, c):
            attend(2 * u, ("far", "far"))
            return c

        lax.fori_loop(0, nfar // 2, far_pair, 0)

        @pl.when(nfar % 2 == 1)
        def _():
            attend(i - 2, ("far",))

        attend(i - 1, ("prev", "diag"))

    lane = lax.broadcasted_iota(I32, (t, LANES), 1)
    for g in range(DSA_KV_HEADS):
        acc = acc_sc[g]
        o = acc / acc[:, HALF:HALF + 1]
        for pr in range(DSA_GROUP // 2):
            lo = o[(2 * pr) * t:(2 * pr + 1) * t]
            hi = o[(2 * pr + 1) * t:(2 * pr + 2) * t]
            cblk = g * (DSA_GROUP // 2) + pr
            o_ref[0, :, cblk * LANES:(cblk + 1) * LANES] = \
                jnp.where(lane < HALF, lo, pltpu.roll(hi, HALF, 1)).astype(BF16)


def _dsa_attn(qa, ka, v, qi, kw, wf, tb, t, topk):
    b, _, s, _ = qa.shape
    nch = s // t
    gt = DSA_GROUP * t
    kern = functools.partial(_dsa_attn_kernel, t=t, topk=topk, s_len=s)
    return pl.pallas_call(
        kern, out_shape=jax.ShapeDtypeStruct((b, s, DSA_HEADS * DSA_HEAD_DIM), BF16),
        grid=(b, nch),
        in_specs=[pl.BlockSpec((1, DSA_HEADS, t, LANES), lambda bi, i: (bi, 0, i, 0)),
                  pl.BlockSpec((1, IDX_HEADS, t, LANES), lambda bi, i: (bi, 0, i, 0)),
                  pl.BlockSpec((1, t, LANES), lambda bi, i: (bi, i, 0)),
                  pl.BlockSpec((1, DSA_KV_HEADS, s, LANES), lambda bi, i: (bi, 0, 0, 0)),
                  pl.BlockSpec((1, DSA_KV_HEADS, s, LANES), lambda bi, i: (bi, 0, 0, 0)),
                  pl.BlockSpec((1, s, LANES), lambda bi, i: (bi, 0, 0)),
                  _const_spec((2, DSA_HEADS, LANES, LANES))],
        out_specs=pl.BlockSpec((1, t, DSA_HEADS * DSA_HEAD_DIM), lambda bi, i: (bi, i, 0)),
        scratch_shapes=[pltpu.VMEM((nch, t, t), I32),
                        pltpu.VMEM((nch, t, t), BF16),
                        pltpu.VMEM((2, t, t), F32),
                        pltpu.VMEM((2, gt, 2 * t), F32),
                        pltpu.VMEM((DSA_KV_HEADS, gt, LANES), F32),
                        pltpu.VMEM((DSA_KV_HEADS, gt, LANES), F32),
                        pltpu.VMEM((8, t), I32),
                        pltpu.VMEM((8, t), I32),
                        pltpu.VMEM((8, t), F32),
                        pltpu.VMEM((nch, 8, t), F32)],
        compiler_params=_params("parallel", "arbitrary"), name="dsa_attn",
    )(qa, qi, wf, ka, v, kw, tb)


def _mm_kernel(a_ref, w_ref, o_ref):
    o_ref[...] = _dot(a_ref[...].astype(BF16), w_ref[...]).astype(BF16)


def _mm(a, w, tm):
    m, k = a.shape
    n = w.shape[1]
    return pl.pallas_call(
        _mm_kernel, out_shape=jax.ShapeDtypeStruct((m, n), BF16), grid=(m // tm,),
        in_specs=[pl.BlockSpec((tm, k), lambda i: (i, 0)), _const_spec((k, n))],
        out_specs=pl.BlockSpec((tm, n), lambda i: (i, 0)),
        compiler_params=_params("parallel"), name="mem_kv",
    )(a, w)


def _post_kernel(a_ref, x_ref, wo_ref, wq_ref, kv_ref, wxo_ref, g_ref, b_ref, o_ref, oc_sc):
    y = _dot(a_ref[0], wo_ref[...])
    x1 = _layer_norm(ALPHA * x_ref[0] + y, g_ref[0:1, :], b_ref[0:1, :])
    q = (_dot(x1.astype(BF16), wq_ref[...]) * (XA_HEAD_DIM ** -0.5 * LOG2E)).astype(BF16)
    for hd in range(XA_HEADS):
        sl = slice(hd * XA_HEAD_DIM, (hd + 1) * XA_HEAD_DIM)
        sc = _dot_nt(q[:, sl], kv_ref[0, :, sl])
        m = jnp.max(sc, axis=-1, keepdims=True)
        p = jnp.exp2(sc - m)
        l = jnp.sum(p, axis=-1, keepdims=True)
        vsl = slice(D_MODEL + hd * XA_HEAD_DIM, D_MODEL + (hd + 1) * XA_HEAD_DIM)
        oc_sc[:, sl] = (_dot(p.astype(BF16), kv_ref[0, :, vsl]) / l).astype(BF16)
    y2 = _dot(oc_sc[...], wxo_ref[...])
    o_ref[0] = _layer_norm(ALPHA * x1 + y2, g_ref[1:2, :], b_ref[1:2, :])


def _post(a, x, w_o, w_q, kv, w_xo, g, bta, tm):
    b, s, _ = x.shape
    mlen = kv.shape[1]
    tok = lambda bi, i: (bi, i, 0)
    return pl.pallas_call(
        _post_kernel, out_shape=jax.ShapeDtypeStruct((b, s, D_MODEL), F32), grid=(b, s // tm),
        in_specs=[pl.BlockSpec((1, tm, D_MODEL), tok), pl.BlockSpec((1, tm, D_MODEL), tok),
                  _const_spec((D_MODEL, D_MODEL)), _const_spec((D_MODEL, D_MODEL)),
                  pl.BlockSpec((1, mlen, 2 * D_MODEL), lambda bi, i: (bi, 0, 0)),
                  _const_spec((D_MODEL, D_MODEL)), _const_spec((3, D_MODEL)), _const_spec((3, D_MODEL))],
        out_specs=pl.BlockSpec((1, tm, D_MODEL), tok),
        scratch_shapes=[pltpu.VMEM((tm, D_MODEL), BF16)],
        compiler_params=_params("parallel", "parallel"), name="post_attn",
    )(a, x, w_o, w_q, kv, w_xo, g, bta)


FFN_CHUNK = 256


def _ffn_kernel(x_ref, wi_ref, wd_ref, g_ref, b_ref, o_ref, act_sc):
    x = x_ref[...]
    xb = x.astype(BF16)
    for c in range(FFN_HIDDEN // FFN_CHUNK):
        sl = slice(c * FFN_CHUNK, (c + 1) * FFN_CHUNK)
        gte = _dot(xb, wi_ref[:, sl])
        up = _dot(xb, wi_ref[:, FFN_HIDDEN + c * FFN_CHUNK:FFN_HIDDEN + (c + 1) * FFN_CHUNK])
        act_sc[:, sl] = (gte * jax.nn.sigmoid(gte) * up).astype(BF16)
    y = _dot(act_sc[...], wd_ref[...])
    o_ref[...] = _layer_norm(ALPHA * x + y, g_ref[2:3, :], b_ref[2:3, :])


def _ffn(x2d, w_in, w_down, g, bta, tm):
    n = x2d.shape[0]
    return pl.pallas_call(
        _ffn_kernel, out_shape=jax.ShapeDtypeStruct((n, D_MODEL), F32), grid=(n // tm,),
        in_specs=[pl.BlockSpec((tm, D_MODEL), lambda i: (i, 0)),
                  _const_spec((D_MODEL, 2 * FFN_HIDDEN)), _const_spec((FFN_HIDDEN, D_MODEL)),
                  _const_spec((3, D_MODEL)), _const_spec((3, D_MODEL))],
        out_specs=pl.BlockSpec((tm, D_MODEL), lambda i: (i, 0)),
        scratch_shapes=[pltpu.VMEM((tm, FFN_HIDDEN), BF16)],
        compiler_params=_params("parallel"), name="ffn",
    )(x2d, w_in.astype(BF16), w_down.astype(BF16), g, bta)


def _tile(n, pref):
    t = min(n, pref)
    assert n % t == 0, (n, pref)
    return t


def kernel(x, mem, positions, rel_bias, mla_w_in, mla_q_norm, mla_w_uq, mla_kv_norm, mla_w_ukv, mla_w_o,
           dsa_w_in, dsa_w_o, xa_w_q, xa_w_kv, xa_w_o, ffn_w_in, ffn_w_down, ln_g, ln_b):
    b, s, d = x.shape
    assert d == D_MODEL and s % LANES == 0
    mlen = mem.shape[1]
    topk = min(TOPK_MAX, s // 4)
    t_attn = _tile(s, 256)
    tm_proj = _tile(s, 512)
    tm_post = _tile(s, 1024)
    tm_ffn = _tile(b * s, 1024)
    pos3 = positions.reshape(b, s, 1)
    tb = _t5_tiles(rel_bias)
    for i in range(DEPTH):
        j = i // 2
        if i % 2 == 0:
            q, k, v = _mla_proj(x, pos3, mla_w_in[j], mla_q_norm[j], mla_w_uq[j], mla_kv_norm[j],
                                mla_w_ukv[j], tm_proj)
            a = _flash_mla(q, k, v, _tile(s, 1024))
            w_o = mla_w_o[j]
        else:
            qa, ka, v2, qi, kw, wf = _dsa_proj(x, dsa_w_in[j], rel_bias, tm_proj)
            a = _dsa_attn(qa, ka, v2, qi, kw, wf, tb, t_attn, topk)
            w_o = dsa_w_o[j]
        kv = _mm(mem.reshape(b * mlen, d), xa_w_kv[i].astype(BF16), _tile(b * mlen, 512)).reshape(b, mlen, 2 * d)
        x = _post(a, x, w_o.astype(BF16), xa_w_q[i].astype(BF16), kv, xa_w_o[i].astype(BF16),
                  ln_g[i], ln_b[i], tm_post)
        x = _ffn(x.reshape(b * s, d), ffn_w_in[i], ffn_w_down[i], ln_g[i], ln_b[i], tm_ffn).reshape(b, s, d)
    return x
```
